```python
import jax, jax.numpy as jnp
from jax import lax
import numpy as np

D_MODEL = 1024
BATCH = 8
SEQ = 4096
DEPTH = 1

GRID_W = 64
CTX_LEN = 256
D_MIX = D_MODEL
EPS = 1e-6
ML_W = D_MIX // 2
ML_HEADS = 4
ML_DH = ML_W // ML_HEADS
ML_CHUNK = 64
CONV_W = 3
ML_COLS = 4 * ML_W + 4 * ML_HEADS
RW_W = D_MIX - ML_W
RW_DH = 64
RW_HEADS = RW_W // RW_DH
RW_LORA_W = 64
RW_LORA_A = 64
RW_LORA_G = 160
RW_LN_EPS = 64e-5
RW_COLS = 3 * RW_W + 2 * RW_LORA_W + 2 * RW_LORA_A + RW_LORA_G
RW_SPLITS = (RW_W, 2 * RW_W, 3 * RW_W, 3 * RW_W + 2 * RW_LORA_W, 3 * RW_W + 2 * RW_LORA_W + 2 * RW_LORA_A)
P_IN = ML_COLS + RW_COLS
N_EXPERTS = 16
EC_FACTOR = 2
D_EXPERT = 2816

kernel_name = "hybrid_mlstm_rwkv7_ecmoe_dit"

F32 = jnp.float32


def rmsnorm(x, g):
    xf = x.astype(F32)
    y = xf * lax.rsqrt(jnp.mean(xf * xf, axis=-1, keepdims=True) + EPS)
    return (y * g).astype(x.dtype)


def centred_dwconv(z, w, b):
    K = w.shape[0]
    p = K // 2
    T = z.shape[1]
    zp = jnp.pad(z, ((0, 0), (p, p), (0, 0)))
    out = zp[:, 0:T] * w[0]
    for i in range(1, K):
        out = out + zp[:, i:i + T] * w[i]
    return out + b


def token_shift(z, mu_prev, mu_next):
    prev = jnp.pad(z, ((0, 0), (1, 0), (0, 0)))[:, :-1]
    nxt = jnp.pad(z, ((0, 0), (0, 1), (0, 0)))[:, 1:]
    return z + mu_prev * (prev - z) + mu_next * (nxt - z)


def to_col_major(z, rows):
    B, T, C = z.shape
    return z.reshape(B, rows, GRID_W, C).transpose(0, 2, 1, 3).reshape(B, T, C)


def from_col_major(z, rows):
    B, T, C = z.shape
    return z.reshape(B, GRID_W, rows, C).transpose(0, 2, 1, 3).reshape(B, T, C)


def _chunks(a, nc):
    B, H, T = a.shape[:3]
    return jnp.moveaxis(a.reshape(B, H, nc, T // nc, *a.shape[3:]), 2, 0)


def mlstm_chunkwise(q, k, v, log_i, log_f, state):
    T = q.shape[2]
    L = ML_CHUNK
    nc = T // L
    causal = jnp.tril(jnp.ones((L, L), bool))

    def step(carry, inp):
        C, n, m = carry
        qb, kb, vb, li, lf = inp
        b = jnp.cumsum(lf, axis=-1)
        Dm = jnp.where(causal, b[..., :, None] - b[..., None, :] + li[..., None, :], -jnp.inf)
        inter = b + m[..., None]
        m_t = jnp.maximum(inter, jnp.max(Dm, axis=-1))
        Wts = jnp.exp(Dm - m_t[..., None]) * jnp.einsum('bhld,bhsd->bhls', qb, kb)
        w_inter = jnp.exp(inter - m_t)
        num = Wts @ vb + w_inter[..., None] * jnp.einsum('bhvk,bhlk->bhlv', C, qb)
        den = jnp.sum(Wts, axis=-1) + w_inter * jnp.einsum('bhk,bhlk->bhl', n, qb)
        h = num / jnp.maximum(jnp.abs(den), jnp.exp(-m_t))[..., None]
        bL = b[..., -1]
        dec = bL[..., None] - b + li
        m_new = jnp.maximum(bL + m, jnp.max(dec, axis=-1))
        wk = jnp.exp(dec - m_new[..., None])
        s_old = jnp.exp(bL + m - m_new)
        C_new = s_old[..., None, None] * C + jnp.einsum('bhl,bhlv,bhlk->bhvk', wk, vb, kb)
        n_new = s_old[..., None] * n + jnp.einsum('bhl,bhlk->bhk', wk, kb)
        return (C_new, n_new, m_new), h

    xs = (_chunks(q, nc), _chunks(k, nc), _chunks(v, nc), _chunks(log_i, nc), _chunks(log_f, nc))
    state, hs = lax.scan(step, state, xs)
    B, H, _, d = q.shape
    return jnp.moveaxis(hs, 0, 2).reshape(B, H, T, d), state


def mlstm_group(u, conv_w, conv_b, gate_b, norm_g, states):
    B, T, _ = u.shape
    u = u.astype(F32)
    zqk, zv, zo, zg = jnp.split(u, (2 * ML_W, 3 * ML_W, 4 * ML_W), axis=-1)
    qk = jax.nn.silu(centred_dwconv(zqk, conv_w, conv_b))
    heads = lambda a: a.reshape(B, T, ML_HEADS, ML_DH).transpose(0, 2, 1, 3)
    q = heads(qk[..., :ML_W])
    k = heads(qk[..., ML_W:]) * (ML_DH ** -0.5)
    v = heads(zv)
    g = (zg + gate_b).reshape(B, T, 4, ML_HEADS).transpose(2, 0, 3, 1)
    li_f, lf_f = g[0], jax.nn.log_sigmoid(g[1])
    li_b, lf_b = g[2], jax.nn.log_sigmoid(g[3])
    flip = lambda a: jnp.flip(a, axis=2)
    h_f, st_f = mlstm_chunkwise(q, k, v, li_f, lf_f, states[0])
    h_b, st_b = mlstm_chunkwise(flip(q), flip(k), flip(v), flip(li_b), flip(lf_b), states[1])
    h = h_f + flip(h_b)
    h = h * lax.rsqrt(jnp.mean(h * h, axis=-1, keepdims=True) + EPS) * norm_g.reshape(ML_HEADS, 1, ML_DH)
    h = h.transpose(0, 2, 1, 3).reshape(B, T, ML_W) * jax.nn.sigmoid(zo)
    return h, (st_f, st_b)


def rwkv7_scan(r, w, k, v, kk, a, S0, reverse):
    def step(S, inp):
        r_t, w_t, k_t, v_t, kk_t, a_t = inp
        sa = jnp.einsum('bhvk,bhk->bhv', S, -kk_t)
        S = S * w_t[:, :, None, :] + sa[..., None] * (kk_t * a_t)[:, :, None, :] + v_t[..., None] * k_t[:, :, None, :]
        return S, jnp.einsum('bhvk,bhk->bhv', S, r_t)
    xs = tuple(jnp.moveaxis(t, 1, 0) for t in (r, w, k, v, kk, a))
    S, ys = lax.scan(step, S0, xs, reverse=reverse)
    return jnp.moveaxis(ys, 0, 1), S


def rwkv_group(u, mu, w0, w_up, a0, a_up, g_up, k_k, k_a, r_k, ln_g, ln_b, states):
    B, T, _ = u.shape
    u = token_shift(u.astype(F32), mu[0], mu[1])
    r, k, v, wd, ad, gd = jnp.split(u, RW_SPLITS, axis=-1)
    wd = wd.reshape(B, T, 2, RW_LORA_W)
    ad = ad.reshape(B, T, 2, RW_LORA_A)
    log_w = -jax.nn.softplus(-(w0 + jnp.einsum('btjr,jrc->btjc', jnp.tanh(wd), w_up))) - 0.5
    decay = jnp.exp(-jnp.exp(log_w))
    a = jax.nn.sigmoid(a0 + jnp.einsum('btjr,jrc->btjc', ad, a_up))
    g = jax.nn.sigmoid(gd) @ g_up
    hd = lambda t: t.reshape(*t.shape[:-1], RW_HEADS, RW_DH)
    kk = hd(k * k_k)
    kk = kk / jnp.maximum(jnp.sqrt(jnp.sum(kk * kk, axis=-1, keepdims=True)), 1e-12)
    k_dir = hd(k[:, :, None] * (1 + (a - 1) * k_a))
    a, decay, r, v = hd(a), hd(decay), hd(r), hd(v)
    y_f, s_f = rwkv7_scan(r, decay[:, :, 0], k_dir[:, :, 0], v, kk, a[:, :, 0], states[0], False)
    y_b, s_b = rwkv7_scan(r, decay[:, :, 1], k_dir[:, :, 1], v, kk, a[:, :, 1], states[1], True)
    y = y_f + y_b
    mean = jnp.mean(y, axis=-1, keepdims=True)
    var = jnp.mean((y - mean) ** 2, axis=-1, keepdims=True)
    y = (y - mean) * lax.rsqrt(var + RW_LN_EPS) * hd(ln_g) + hd(ln_b)
    bonus = jnp.sum(jnp.sum(r[:, :, None] * k_dir * r_k, axis=-1, keepdims=True), axis=2) * v
    return (y + bonus).reshape(B, T, RW_W) * g, (s_f, s_b)


def expert_choice_ffn(h, router_w, w1, w3, w2):
    B, T, D = h.shape
    cap = EC_FACTOR * T // N_EXPERTS
    aff = jax.nn.softmax((h @ router_w).astype(F32), axis=-1)
    gate, idx = lax.top_k(jnp.swapaxes(aff, 1, 2), cap)
    xs = jax.vmap(lambda hb, ib: hb[ib])(h, idx)
    hid = jax.nn.silu(jnp.einsum('becd,edf->becf', xs, w1)) * jnp.einsum('becd,edf->becf', xs, w3)
    ys = jnp.einsum('becf,efd->becd', hid, w2) * gate[..., None].astype(h.dtype)
    return jax.vmap(lambda yb, ib: jnp.zeros((T, D), yb.dtype).at[ib.reshape(-1)].add(yb.reshape(-1, D)))(ys, idx)


def mlstm_zero_state(B):
    z = (jnp.zeros((B, ML_HEADS, ML_DH, ML_DH), F32), jnp.zeros((B, ML_HEADS, ML_DH), F32), jnp.zeros((B, ML_HEADS), F32))
    return (z, z)


def rwkv_zero_state(B):
    S = jnp.zeros((B, RW_HEADS, RW_DH, RW_DH), F32)
    return (S, S)


def setup_inputs(seed: int = 0) -> dict:
    key = jax.random.key(seed)
    ks = iter(jax.random.split(key, 40))
    nrm = lambda shape, s: jax.random.normal(next(ks), shape, F32) * s
    L = DEPTH
    f_bias = jnp.linspace(3.0, 6.0, ML_HEADS, dtype=F32)
    zb = jnp.zeros((ML_HEADS,), F32)
    gate_base = jnp.concatenate([zb, f_bias, zb, f_bias])
    return {
        "x": nrm((BATCH, SEQ, D_MODEL), 1.0),
        "c": nrm((BATCH, D_MODEL), 1.0),
        "ctx": nrm((BATCH, CTX_LEN, D_MODEL), 1.0),
        "c_ctx": nrm((D_MODEL,), 1.0),
        "mod_w": nrm((L, D_MODEL, 6 * D_MODEL), 0.5 * D_MODEL ** -0.5),
        "mod_b": nrm((L, 6 * D_MODEL), 0.02),
        "norm1_g": 1.0 + nrm((L, D_MODEL), 0.05),
        "w_in": nrm((L, D_MODEL, P_IN), D_MODEL ** -0.5),
        "ml_conv_w": nrm((L, CONV_W, 2 * ML_W), CONV_W ** -0.5),
        "ml_conv_b": nrm((L, 2 * ML_W), 0.02),
        "ml_gate_b": gate_base + nrm((L, 4 * ML_HEADS), 0.1),
        "ml_norm_g": 1.0 + nrm((L, ML_W), 0.05),
        "rw_mu": jax.random.uniform(next(ks), (L, 2, RW_COLS), F32, 0.05, 0.45),
        "rw_w0": jnp.linspace(-6.0, 1.0, RW_W, dtype=F32) + nrm((L, 2, RW_W), 0.1),
        "rw_w_up": nrm((L, 2, RW_LORA_W, RW_W), 0.1 * RW_LORA_W ** -0.5),
        "rw_a0": nrm((L, 2, RW_W), 0.1),
        "rw_a_up": nrm((L, 2, RW_LORA_A, RW_W), 0.1 * RW_LORA_A ** -0.5),
        "rw_g_up": nrm((L, RW_LORA_G, RW_W), RW_LORA_G ** -0.5),
        "rw_k_k": 0.85 + nrm((L, RW_W), 0.05),
        "rw_k_a": 1.0 + nrm((L, RW_W), 0.05),
        "rw_r_k": nrm((L, RW_HEADS, RW_DH), 0.1),
        "rw_ln_g": 1.0 + nrm((L, RW_W), 0.05),
        "rw_ln_b": nrm((L, RW_W), 0.02),
        "w_out": nrm((L, D_MIX, D_MODEL), D_MIX ** -0.5),
        "norm2_g": 1.0 + nrm((L, D_MODEL), 0.05),
        "router_w": nrm((L, D_MODEL, N_EXPERTS), D_MODEL ** -0.5),
        "exp_w1": nrm((L, N_EXPERTS, D_MODEL, D_EXPERT), D_MODEL ** -0.5),
        "exp_w3": nrm((L, N_EXPERTS, D_MODEL, D_EXPERT), D_MODEL ** -0.5),
        "exp_w2": nrm((L, N_EXPERTS, D_EXPERT, D_MODEL), D_EXPERT ** -0.5),
        "final_g": 1.0 + nrm((D_MODEL,), 0.05),
    }


def reference(x, c, ctx, c_ctx, mod_w, mod_b, norm1_g, w_in, ml_conv_w, ml_conv_b, ml_gate_b, ml_norm_g,
              rw_mu, rw_w0, rw_w_up, rw_a0, rw_a_up, rw_g_up, rw_k_k, rw_k_a, rw_r_k, rw_ln_g, rw_ln_b,
              w_out, norm2_g, router_w, exp_w1, exp_w3, exp_w2, final_g):
    B, n, _ = x.shape
    rows = n // GRID_W
    h, hc = x, ctx
    for l in range(DEPTH):
        last = l == DEPTH - 1
        m_lat = (jax.nn.silu(c) @ mod_w[l] + mod_b[l])[:, None, :]
        m_ctx = (jax.nn.silu(c_ctx) @ mod_w[l] + mod_b[l])[None, None, :]
        sh1, sc1, g1, sh2, sc2, g2 = jnp.split(m_lat, 6, axis=-1)
        csh1, csc1, cg1, csh2, csc2, cg2 = jnp.split(m_ctx, 6, axis=-1)
        u_c = (rmsnorm(hc, norm1_g[l]) * (1 + csc1) + csh1) @ w_in[l]
        u_l = (rmsnorm(h, norm1_g[l]) * (1 + sc1) + sh1) @ w_in[l]
        ml_args = (ml_conv_w[l], ml_conv_b[l], ml_gate_b[l], ml_norm_g[l])
        rw_args = (rw_mu[l], rw_w0[l], rw_w_up[l], rw_a0[l], rw_a_up[l], rw_g_up[l], rw_k_k[l], rw_k_a[l],
                   rw_r_k[l], rw_ln_g[l], rw_ln_b[l])
        ml_c, ml_st = mlstm_group(u_c[..., :ML_COLS], *ml_args, mlstm_zero_state(B))
        rw_c, rw_st = rwkv_group(u_c[..., ML_COLS:], *rw_args, rwkv_zero_state(B))
        ml_l, _ = mlstm_group(u_l[..., :ML_COLS], *ml_args, ml_st)
        rw_l, _ = rwkv_group(to_col_major(u_l[..., ML_COLS:], rows), *rw_args, rw_st)
        rw_l = from_col_major(rw_l, rows)
        mix_l = jnp.concatenate([ml_l, rw_l], axis=-1).astype(h.dtype) @ w_out[l]
        h = h + g1 * mix_l
        h = h + g2 * expert_choice_ffn(rmsnorm(h, norm2_g[l]) * (1 + sc2) + sh2, router_w[l], exp_w1[l], exp_w3[l], exp_w2[l])
        if not last:
            mix_c = jnp.concatenate([ml_c, rw_c], axis=-1).astype(hc.dtype) @ w_out[l]
            hc = hc + cg1 * mix_c
            hc = hc + cg2 * expert_choice_ffn(rmsnorm(hc, norm2_g[l]) * (1 + csc2) + csh2, router_w[l], exp_w1[l], exp_w3[l], exp_w2[l])
    return rmsnorm(h, final_g)
```

```python
import functools

import jax
import jax.numpy as jnp
from jax import lax
from jax.experimental import pallas as pl
from jax.experimental.pallas import tpu as pltpu

F32 = jnp.float32
BF16 = jnp.bfloat16
I32 = jnp.int32

D_MODEL = 1024
GRID_W = 64
EPS = 1e-6
ML_W = 512
ML_HEADS = 4
ML_DH = 128
ML_GATES = 16
ML_N = 4 * ML_W + 128
RW_W = 512
RW_HEADS = 8
RW_DH = 64
RW_LORA = 64
RW_LORA_G = 160
RW_LN_EPS = 64e-5
RW_COLS = 3 * RW_W + 2 * RW_LORA + 2 * RW_LORA + RW_LORA_G
RW_N = 2048
RW_PAIRS = RW_HEADS // 2
N_EXPERTS = 16
EC_FACTOR = 2
D_EXPERT = 2816

LANES = 128
SUBLANES = 8
TM = 256
RW_CHUNK = 64
RW_PAIRS_PER_STEP = 4
FFN_TF = 256
VMEM_LIMIT = 56 * 1024 * 1024

NN = (((1,), (0,)), ((), ()))
NT = (((1,), (1,)), ((), ()))
TN = (((0,), (0,)), ((), ()))


def _cparams(n_axes):
    return pltpu.CompilerParams(
        dimension_semantics=("arbitrary",) * n_axes, vmem_limit_bytes=VMEM_LIMIT)


def _dot(a, b, dn=NN):
    return lax.dot_general(a, b, dn, preferred_element_type=F32)


def _split2(a):
    hi = a.astype(BF16)
    lo = (a - hi.astype(F32)).astype(BF16)
    return hi, lo


def _split3(a):
    hi = a.astype(BF16)
    r1 = a - hi.astype(F32)
    mid = r1.astype(BF16)
    lo = (r1 - mid.astype(F32)).astype(BF16)
    return hi, mid, lo


def _dot3(a, b, dn=NN):
    ah, al = _split2(a)
    bh, bl = _split2(b)
    return _dot(ah, bh, dn) + (_dot(ah, bl, dn) + _dot(al, bh, dn))


def _dot3w(a, whi, wlo):
    ah, al = _split2(a)
    return _dot(ah, whi) + (_dot(al, whi) + _dot(ah, wlo))


def _dot_sel(sel_bf16, b, dn=NN):
    bh, bm, bl = _split3(b)
    return _dot(sel_bf16, bh, dn) + (_dot(sel_bf16, bm, dn) + _dot(sel_bf16, bl, dn))


def _dot_rsel(a, sel_bf16, dn=NN):
    ah, am, al = _split3(a)
    return _dot(ah, sel_bf16, dn) + (_dot(am, sel_bf16, dn) + _dot(al, sel_bf16, dn))


def _silu(x):
    return x * jax.nn.sigmoid(x)


def _log_sigmoid(x):
    return jnp.minimum(x, 0.0) - jnp.log1p(jnp.exp(-jnp.abs(x)))


def _rmsnorm(x, g):
    return x * lax.rsqrt(jnp.mean(x * x, axis=-1, keepdims=True) + EPS) * g


def _tri(n, lower, strict, dtype):
    r = lax.broadcasted_iota(I32, (n, n), 0)
    c = lax.broadcasted_iota(I32, (n, n), 1)
    if lower:
        m = (c < r) if strict else (c <= r)
    else:
        m = (c > r) if strict else (c >= r)
    return m if dtype is None else m.astype(dtype)


def _shift_rows(z, prev_row, next_row):
    n = z.shape[0]
    row = lax.broadcasted_iota(I32, z.shape, 0)
    zm1 = jnp.where(row == 0, prev_row, pltpu.roll(z, 1, 0))
    zp1 = jnp.where(row == n - 1, next_row, pltpu.roll(z, n - 1, 0))
    return zm1, zp1


def _mod_kernel(c_ref, w_ref, b_ref, o_ref):
    o_ref[...] = _dot3(_silu(c_ref[...]), w_ref[...]) + b_ref[...]


def _mod_call(cc, mod_w, mod_b):
    n = mod_w.shape[1]
    tn = 1024
    return pl.pallas_call(
        _mod_kernel,
        grid=(n // tn,),
        in_specs=[pl.BlockSpec((16, D_MODEL), lambda j: (0, 0)),
                  pl.BlockSpec((D_MODEL, tn), lambda j: (0, j)),
                  pl.BlockSpec((1, tn), lambda j: (0, j))],
        out_specs=pl.BlockSpec((16, tn), lambda j: (0, j)),
        out_shape=jax.ShapeDtypeStruct((16, n), F32),
        compiler_params=_cparams(1),
        name="mod",
    )(cc, mod_w, mod_b)


def _proj_ml_kernel(ctx_ref, x_ref, sh_ref, sc_ref, g_ref, whi_ref, wlo_ref, o_ref, *, nct):
    i = pl.program_id(1)
    xin = jnp.where(i < nct, ctx_ref[0], x_ref[0])
    xn = _rmsnorm(xin, g_ref[...]) * (1.0 + sc_ref[0]) + sh_ref[0]
    o_ref[0] = _dot3w(xn, whi_ref[...], wlo_ref[...])


def _proj_rw_kernel(ctx_ref, x_ref, sh_ref, sc_ref, g_ref, whi_ref, wlo_ref, o_ref, *, nct, rows):
    i = pl.program_id(1)
    xr = x_ref[0]
    ncb = TM // rows
    xcm = jnp.concatenate([xr[:, c * D_MODEL:(c + 1) * D_MODEL] for c in range(ncb)], axis=0)
    xin = jnp.where(i < nct, ctx_ref[0], xcm)
    xn = _rmsnorm(xin, g_ref[...]) * (1.0 + sc_ref[0]) + sh_ref[0]
    o_ref[0] = _dot3w(xn, whi_ref[...], wlo_ref[...])


def _proj_call(kind, x, ctx, mods3, norm_g, whi, wlo):
    B, T, _ = x.shape
    nct = ctx.shape[1] // TM
    nlt = T // TM
    n = whi.shape[1]
    rows = T // GRID_W
    mrow = lambda b, i: jnp.where(i < nct, B, b)
    ctx_spec = pl.BlockSpec((1, TM, D_MODEL), lambda b, i: (b, jnp.minimum(i, nct - 1), 0))
    if kind == "ml":
        body = functools.partial(_proj_ml_kernel, nct=nct)
        x_in = x
        x_spec = pl.BlockSpec((1, TM, D_MODEL), lambda b, i: (b, jnp.maximum(i - nct, 0), 0))
    else:
        body = functools.partial(_proj_rw_kernel, nct=nct, rows=rows)
        x_in = x.reshape(B, rows, GRID_W * D_MODEL)
        x_spec = pl.BlockSpec((1, rows, (TM // rows) * D_MODEL),
                              lambda b, i: (b, 0, jnp.maximum(i - nct, 0)))
    return pl.pallas_call(
        body,
        grid=(B, nct + nlt),
        in_specs=[ctx_spec, x_spec,
                  pl.BlockSpec((1, 1, D_MODEL), lambda b, i: (mrow(b, i), 0, 0)),
                  pl.BlockSpec((1, 1, D_MODEL), lambda b, i: (mrow(b, i), 0, 1)),
                  pl.BlockSpec((1, D_MODEL), lambda b, i: (0, 0)),
                  pl.BlockSpec((D_MODEL, n), lambda b, i: (0, 0)),
                  pl.BlockSpec((D_MODEL, n), lambda b, i: (0, 0))],
        out_specs=pl.BlockSpec((1, TM, n), lambda b, i: (b, i, 0)),
        out_shape=jax.ShapeDtypeStruct((B, (nct + nlt) * TM, n), F32),
        compiler_params=_cparams(2),
        name="proj_" + kind,
    )(ctx, x_in, mods3, mods3, norm_g, whi, wlo)


def _seq_chunk(d, j, nct, ntot):
    bwd = jnp.where(j < nct, nct - 1 - j, ntot - 1 - (j - nct))
    return jnp.where(d == 0, j, bwd)


def _mlstm_kernel(u_ref, hp_ref, hn_ref, cw_ref, cb_ref, gb_ref, o_ref, c_ref, n_ref, m_ref,
                  *, nct, ntot):
    d = pl.program_id(1)
    j = pl.program_id(2)
    c = _seq_chunk(d, j, nct, ntot)
    L = TM

    @pl.when(j == 0)
    def _():
        c_ref[...] = jnp.zeros_like(c_ref)
        n_ref[...] = jnp.zeros_like(n_ref)
        m_ref[...] = jnp.zeros_like(m_ref)

    seg_first = (c == 0) | (c == nct)
    seg_last = (c == nct - 1) | (c == ntot - 1)
    zqk = u_ref[0, :, 0:2 * ML_W]
    prev_row = jnp.where(seg_first, 0.0, hp_ref[0, SUBLANES - 1:SUBLANES, :])
    next_row = jnp.where(seg_last, 0.0, hn_ref[0, 0:1, :])
    zm1, zp1 = _shift_rows(zqk, prev_row, next_row)
    cw = cw_ref[...]
    qk = _silu(zm1 * cw[0:1] + zqk * cw[1:2] + zp1 * cw[2:3] + cb_ref[...])
    zv = u_ref[0, :, 2 * ML_W:3 * ML_W]

    g = u_ref[0, :, 4 * ML_W:4 * ML_W + LANES] + gb_ref[...]
    lane = lax.broadcasted_iota(I32, (L, LANES), 1)
    is_f = ((lane >= 4) & (lane < 8)) | ((lane >= 12) & (lane < 16))
    q_all = jnp.where(is_f, _log_sigmoid(g), g)
    causal = jnp.where(d == 0, _tri(L, True, False, F32), _tri(L, False, False, F32))
    b_all = _dot_sel(causal.astype(BF16), q_all)
    q_t = q_all.T
    b_t = b_all.T
    keep = causal > 0.5

    def lane_pick(a, idx):
        return jnp.sum(jnp.where(lane == idx, a, 0.0), axis=-1, keepdims=True)

    def row_pick(a_t, h):
        return jnp.where(d == 0, a_t[h:h + 1, :], a_t[8 + h:9 + h, :])

    for h in range(ML_HEADS):
        sl = slice(h * ML_DH, (h + 1) * ML_DH)
        q = qk[:, sl]
        k = qk[:, ML_W + h * ML_DH:ML_W + (h + 1) * ML_DH] * (ML_DH ** -0.5)
        v = zv[:, sl]
        li_col = lane_pick(q_all, h + 8 * d)
        b_col = lane_pick(b_all, 4 + h + 8 * d)
        li_row = row_pick(q_t, h)
        b_row = row_pick(b_t, 4 + h)
        m_prev = m_ref[h, 0:1, 0:1]
        C = c_ref[h]
        n_row = n_ref[h, 0:1, :]

        dm = jnp.where(keep, b_col - b_row + li_row, -jnp.inf)
        inter = b_col + m_prev
        m_t = jnp.maximum(inter, jnp.max(dm, axis=-1, keepdims=True))
        wts = jnp.exp(dm - m_t) * _dot3(q, k, NT)
        w_inter = jnp.exp(inter - m_t)
        num = _dot3(wts, v) + w_inter * _dot3(q, C, NT)
        den = (jnp.sum(wts, axis=-1, keepdims=True)
               + w_inter * jnp.sum(q * n_row, axis=-1, keepdims=True))
        o_ref[0, 0, :, sl] = num / jnp.maximum(jnp.abs(den), jnp.exp(-m_t))

        b_end = jnp.where(d == 0, b_col[L - 1:L, :], b_col[0:1, :])
        dec = b_end - b_col + li_col
        m_new = jnp.maximum(b_end + m_prev, jnp.max(dec, axis=0, keepdims=True))
        wk = jnp.exp(dec - m_new)
        s_old = jnp.exp(b_end + m_prev - m_new)
        c_ref[h] = s_old * C + _dot3(v * wk, k, TN)
        n_ref[h] = jnp.broadcast_to(
            s_old * n_row + jnp.sum(wk * k, axis=0, keepdims=True), (SUBLANES, ML_DH))
        m_ref[h] = jnp.broadcast_to(m_new, (SUBLANES, LANES))


def _mlstm_call(u_ml, conv_w, conv_b, gate_b, B, nct, nlt):
    ntot = nct + nlt
    hb = TM // SUBLANES
    nrb = ntot * hb
    chunk = lambda d, j: _seq_chunk(d, j, nct, ntot)
    out_blk = lambda d, j: jnp.where(j < nct, d * (nlt - 1), chunk(d, j) - nct)
    return pl.pallas_call(
        functools.partial(_mlstm_kernel, nct=nct, ntot=ntot),
        grid=(B, 2, ntot),
        in_specs=[
            pl.BlockSpec((1, TM, ML_N), lambda b, d, j: (b, chunk(d, j), 0)),
            pl.BlockSpec((1, SUBLANES, 2 * ML_W),
                         lambda b, d, j: (b, jnp.maximum(chunk(d, j) * hb - 1, 0), 0)),
            pl.BlockSpec((1, SUBLANES, 2 * ML_W),
                         lambda b, d, j: (b, jnp.minimum((chunk(d, j) + 1) * hb, nrb - 1), 0)),
            pl.BlockSpec((3, 2 * ML_W), lambda b, d, j: (0, 0)),
            pl.BlockSpec((1, 2 * ML_W), lambda b, d, j: (0, 0)),
            pl.BlockSpec((1, LANES), lambda b, d, j: (0, 0)),
        ],
        out_specs=pl.BlockSpec((1, 1, TM, ML_W), lambda b, d, j: (d, b, out_blk(d, j), 0)),
        out_shape=jax.ShapeDtypeStruct((2, B, nlt * TM, ML_W), F32),
        scratch_shapes=[pltpu.VMEM((ML_HEADS, ML_DH, ML_DH), F32),
                        pltpu.VMEM((ML_HEADS, SUBLANES, ML_DH), F32),
                        pltpu.VMEM((ML_HEADS, SUBLANES, LANES), F32)],
        compiler_params=_cparams(3),
        name="mlstm",
    )(u_ml, u_ml, u_ml, conv_w, conv_b, gate_b)


def _rw_pre_kernel(u_ref, hp_ref, hn_ref, mu_ref, w0_ref, wup_ref, a0_ref, aup_ref, gup_ref,
                   kk_ref, ka_ref, rk_ref, scan_ref, gb_ref, *, nct, ntot):
    c = pl.program_id(1)
    seg_first = (c == 0) | (c == nct)
    seg_last = (c == nct - 1) | (c == ntot - 1)
    z = u_ref[0]
    prev_row = jnp.where(seg_first, 0.0, hp_ref[0, SUBLANES - 1:SUBLANES, :])
    next_row = jnp.where(seg_last, 0.0, hn_ref[0, 0:1, :])
    zm1, zp1 = _shift_rows(z, prev_row, next_row)
    mu = mu_ref[...]
    xs = z + mu[0:1] * (zm1 - z) + mu[1:2] * (zp1 - z)

    r = xs[:, 0:RW_W]
    k = xs[:, RW_W:2 * RW_W]
    v = xs[:, 2 * RW_W:3 * RW_W]
    wd = xs[:, 3 * RW_W:3 * RW_W + 2 * RW_LORA]
    ad = xs[:, 3 * RW_W + 2 * RW_LORA:3 * RW_W + 4 * RW_LORA]
    gd = xs[:, 3 * RW_W + 4 * RW_LORA:RW_N]

    w_pre = w0_ref[...] + _dot3(jnp.tanh(wd), wup_ref[...])
    log_w = -jax.nn.softplus(-w_pre) - 0.5
    lw = -jnp.exp(log_w)
    a = jax.nn.sigmoid(a0_ref[...] + _dot3(ad, aup_ref[...]))
    g = _dot3(jax.nn.sigmoid(gd), gup_ref[...])

    hr = lax.broadcasted_iota(I32, (RW_W, RW_W), 0) // RW_DH
    hc = lax.broadcasted_iota(I32, (RW_W, RW_W), 1) // RW_DH
    head_ones = (hr == hc).astype(BF16)
    kk = k * kk_ref[...]
    ss = _dot_rsel(kk * kk, head_ones)
    kk = kk / jnp.maximum(jnp.sqrt(ss), 1e-12)
    ka = ka_ref[...]
    k_f = k * (1.0 + (a[:, 0:RW_W] - 1.0) * ka)
    k_b = k * (1.0 + (a[:, RW_W:2 * RW_W] - 1.0) * ka)
    rk = rk_ref[...]
    bonus = (_dot_rsel(r * k_f * rk, head_ones) + _dot_rsel(r * k_b * rk, head_ones)) * v

    cols = (r, v, kk, lw[:, 0:RW_W], k_f, kk * a[:, 0:RW_W],
            lw[:, RW_W:2 * RW_W], k_b, kk * a[:, RW_W:2 * RW_W])
    for p in range(RW_PAIRS):
        for qi, arr in enumerate(cols):
            scan_ref[0, p, :, qi * LANES:(qi + 1) * LANES] = arr[:, p * LANES:(p + 1) * LANES]
    gb_ref[0, :, 0:RW_W] = g
    gb_ref[0, :, RW_W:2 * RW_W] = bonus


def _rw_pre_call(u_rw, mu, w0, wup, a0, aup, gup, k_k, k_a, r_k, B, nct, nlt):
    ntot = nct + nlt
    hb = TM // SUBLANES
    nrb = ntot * hb
    full = lambda shape: pl.BlockSpec(shape, lambda b, c: (0,) * len(shape))
    return pl.pallas_call(
        functools.partial(_rw_pre_kernel, nct=nct, ntot=ntot),
        grid=(B, ntot),
        in_specs=[
            pl.BlockSpec((1, TM, RW_N), lambda b, c: (b, c, 0)),
            pl.BlockSpec((1, SUBLANES, RW_N), lambda b, c: (b, jnp.maximum(c * hb - 1, 0), 0)),
            pl.BlockSpec((1, SUBLANES, RW_N),
                         lambda b, c: (b, jnp.minimum((c + 1) * hb, nrb - 1), 0)),
            full((2, RW_N)), full((1, 2 * RW_W)), full((2 * RW_LORA, 2 * RW_W)),
            full((1, 2 * RW_W)), full((2 * RW_LORA, 2 * RW_W)), full((256, RW_W)),
            full((1, RW_W)), full((1, RW_W)), full((1, RW_W)),
        ],
        out_specs=[
            pl.BlockSpec((1, RW_PAIRS, TM, 9 * LANES), lambda b, c: (b, 0, c, 0)),
            pl.BlockSpec((1, TM, 2 * RW_W), lambda b, c: (b, jnp.maximum(c - nct, 0), 0)),
        ],
        out_shape=[jax.ShapeDtypeStruct((B, RW_PAIRS, ntot * TM, 9 * LANES), F32),
                   jax.ShapeDtypeStruct((B, nlt * TM, 2 * RW_W), F32)],
        compiler_params=_cparams(2),
        name="rw_pre",
    )(u_rw, u_rw, u_rw, mu, w0, wup, a0, aup, gup, k_k, k_a, r_k)


def _rw_chunks(blks, states, lowers):
    n = len(blks)
    L = blks[0].shape[0]
    idx = range(n)
    hidx = [(i, h2) for i in idx for h2 in range(2)]
    col = lambda i, c: blks[i][:, c * LANES:(c + 1) * LANES]
    r, v, kk, lw, kd, bh = ([col(i, c) for i in idx] for c in range(6))
    tri = {lo: (_tri(L, lo, False, None), _tri(L, lo, True, None)) for lo in set(lowers)}
    incl = [tri[lo][0] for lo in lowers]
    strict = [tri[lo][1] for lo in lowers]
    logp = [_dot_sel(incl[i].astype(BF16), lw[i]) for i in idx]
    logp_end = [logp[i][L - 1:L, :] if lowers[i] else logp[i][0:1, :] for i in idx]
    p_inv = [jnp.exp(-logp[i]) for i in idx]
    a_t = [-kk[i] * jnp.exp(logp[i] - lw[i]) for i in idx]
    r_t = [r[i] * jnp.exp(logp[i]) for i in idx]
    bk = [jnp.concatenate([bh[i] * p_inv[i], kd[i] * p_inv[i]], axis=0) for i in idx]
    vv = [jnp.concatenate([v[i], v[i]], axis=0) for i in idx]
    lane = lax.broadcasted_iota(I32, (L, LANES), 1)
    in_head = (lane < RW_DH, lane >= RW_DH)
    eye = jnp.where(_tri(L, True, False, None) & _tri(L, False, False, None), 1.0, 0.0)
    row2 = lax.broadcasted_iota(I32, (L, 2 * L), 0)
    col2 = lax.broadcasted_iota(I32, (L, 2 * L), 1)
    src2 = jnp.where(col2 >= L, col2 - L, col2)
    ak_mask = {lo: ((src2 < row2) if lo else (src2 > row2)) & (col2 >= L) for lo in set(lowers)}
    r_mask = {lo: (src2 <= row2) if lo else (src2 >= row2) for lo in set(lowers)}

    a_s = [_dot3(a_t[i], states[i], NT) for i in idx]
    r_s = [_dot3(r_t[i], states[i], NT) for i in idx]
    g_a = [_dot3(jnp.where(in_head[h], a_t[i], 0.0), bk[i], NT) for i, h in hidx]
    g_r = [_dot3(jnp.where(in_head[h], r_t[i], 0.0), bk[i], NT) for i, h in hidx]
    n_mat = [jnp.where(strict[i], g_a[q][:, 0:L], 0.0) for q, (i, h) in enumerate(hidx)]
    w_rhs = [a_s[i] + _dot3(jnp.where(ak_mask[lowers[i]], g_a[q], 0.0), vv[i])
             for q, (i, h) in enumerate(hidx)]
    hq = range(len(hidx))
    inv = [eye + n_mat[q] for q in hq]
    pw = [_dot3(n_mat[q], n_mat[q]) for q in hq]
    span = 4
    while span < L:
        inv = [inv[q] + _dot3(inv[q], pw[q]) for q in hq]
        pw = [_dot3(pw[q], pw[q]) for q in hq]
        span *= 2
    inv = [inv[q] + _dot3(inv[q], pw[q]) for q in hq]
    u_h = [_dot3(inv[q], w_rhs[q]) for q in hq]
    u = [jnp.where(in_head[0], u_h[2 * i], u_h[2 * i + 1]) for i in idx]
    uv = [jnp.concatenate([u[i], v[i]], axis=0) for i in idx]
    y_h = [r_s[i] + _dot3(jnp.where(r_mask[lowers[i]], g_r[q], 0.0), uv[i])
           for q, (i, h) in enumerate(hidx)]
    y = [jnp.where(in_head[0], y_h[2 * i], y_h[2 * i + 1]) for i in idx]

    rr = lax.broadcasted_iota(I32, (LANES, LANES), 0) // RW_DH
    cc = lax.broadcasted_iota(I32, (LANES, LANES), 1) // RW_DH
    s_new = []
    for i in idx:
        to_end = jnp.exp(logp_end[i] - logp[i])
        bk_end = jnp.concatenate([bh[i] * to_end, kd[i] * to_end], axis=0)
        s_new.append(jnp.where(
            rr == cc, states[i] * jnp.exp(logp_end[i]) + _dot3(uv[i], bk_end, TN), 0.0))
    return y, s_new


def _rw_scan_kernel(f_ref, b_ref, yf_ref, yb_ref, s_ref):
    j = pl.program_id(2)

    @pl.when(j == 0)
    def _():
        s_ref[...] = jnp.zeros_like(s_ref)

    blks, states, lowers = [], [], []
    for p in range(RW_PAIRS_PER_STEP):
        fb = f_ref[0, p]
        bb = b_ref[0, p]
        blks += [fb[:, 0:6 * LANES],
                 jnp.concatenate([bb[:, 0:3 * LANES], bb[:, 6 * LANES:9 * LANES]], axis=1)]
        states += [s_ref[p, 0], s_ref[p, 1]]
        lowers += [True, False]
    y, s_new = _rw_chunks(blks, states, lowers)
    for p in range(RW_PAIRS_PER_STEP):
        yf_ref[0, p] = y[2 * p]
        yb_ref[0, p] = y[2 * p + 1]
        s_ref[p, 0] = s_new[2 * p]
        s_ref[p, 1] = s_new[2 * p + 1]


def _rw_scan_call(scan_in, B, t_ctx, t_lat):
    L = RW_CHUNK
    nct, nlt = t_ctx // L, t_lat // L
    ntot = nct + nlt
    fchunk = lambda j: j
    bchunk = lambda j: _seq_chunk(1, j, nct, ntot)
    pps = RW_PAIRS_PER_STEP
    in_blk = (1, pps, L, 9 * LANES)
    out_blk = (1, pps, L, LANES)
    out_shape = jax.ShapeDtypeStruct((B, RW_PAIRS, t_lat, LANES), F32)
    return pl.pallas_call(
        _rw_scan_kernel,
        grid=(B, RW_PAIRS // pps, ntot),
        in_specs=[pl.BlockSpec(in_blk, lambda b, p, j: (b, p, fchunk(j), 0)),
                  pl.BlockSpec(in_blk, lambda b, p, j: (b, p, bchunk(j), 0))],
        out_specs=[
            pl.BlockSpec(out_blk, lambda b, p, j: (b, p, jnp.maximum(fchunk(j) - nct, 0), 0)),
            pl.BlockSpec(out_blk,
                         lambda b, p, j: (b, p, jnp.where(j < nct, nlt - 1, bchunk(j) - nct), 0)),
        ],
        out_shape=[out_shape, out_shape],
        scratch_shapes=[pltpu.VMEM((pps, 2, LANES, LANES), F32)],
        compiler_params=_cparams(3),
        name="rw_scan",
    )(scan_in, scan_in)


def _mix_out_kernel(x_ref, h_ref, zo_ref, yf_ref, yb_ref, gb_ref, g1_ref, sh2_ref, sc2_ref,
                    mlg_ref, lng_ref, lnb_ref, n2g_ref, whi_ref, wlo_ref, rw_ref,
                    h1_ref, hn_ref, aff_ref):
    hm = h_ref[0, 0] + h_ref[1, 0]
    parts = []
    for h in range(ML_HEADS):
        hh = hm[:, h * ML_DH:(h + 1) * ML_DH]
        parts.append(hh * lax.rsqrt(jnp.mean(hh * hh, axis=-1, keepdims=True) + EPS))
    ml = jnp.concatenate(parts, axis=1) * mlg_ref[...] * jax.nn.sigmoid(zo_ref[0])

    nr = TM // GRID_W
    yf = yf_ref[0]
    yb = yb_ref[0]
    y = jnp.concatenate(
        [jnp.concatenate([yf[p, :, jr * LANES:(jr + 1) * LANES] + yb[p, :, jr * LANES:(jr + 1) * LANES]
                          for p in range(RW_PAIRS)], axis=1) for jr in range(nr)], axis=0)
    gbm = gb_ref[0]
    gbr = jnp.concatenate([gbm[:, jr * 2 * RW_W:(jr + 1) * 2 * RW_W] for jr in range(nr)], axis=0)
    hr = lax.broadcasted_iota(I32, (RW_W, RW_W), 0) // RW_DH
    hc = lax.broadcasted_iota(I32, (RW_W, RW_W), 1) // RW_DH
    head_ones = (hr == hc).astype(BF16)
    mean = _dot_rsel(y, head_ones) * (1.0 / RW_DH)
    dy = y - mean
    var = _dot_rsel(dy * dy, head_ones) * (1.0 / RW_DH)
    rw = dy * lax.rsqrt(var + RW_LN_EPS) * lng_ref[...] + lnb_ref[...]
    rw = (rw + gbr[:, RW_W:2 * RW_W]) * gbr[:, 0:RW_W]

    mix = _dot3w(jnp.concatenate([ml, rw], axis=1), whi_ref[...], wlo_ref[...])
    h1 = x_ref[0] + g1_ref[0] * mix
    h1_ref[0] = h1
    hn = _rmsnorm(h1, n2g_ref[...]) * (1.0 + sc2_ref[0]) + sh2_ref[0]
    hn_ref[0] = hn.astype(BF16)
    logits = _dot3(hn, rw_ref[...])
    lane = lax.broadcasted_iota(I32, logits.shape, 1)
    logits = jnp.where(lane < N_EXPERTS, logits, -jnp.inf)
    e = jnp.exp(logits - jnp.max(logits, axis=-1, keepdims=True))
    aff = e / jnp.sum(e, axis=-1, keepdims=True)
    aff_ref[0] = aff.T[0:N_EXPERTS, :]


def _mix_out_call(x, h_ml, u_ml, y_f, y_b, gb, mods3, ml_norm_g, ln_g, ln_b, norm2_g,
                  whi, wlo, router_pad, nct):
    B, T, _ = x.shape
    nlt = T // TM
    rows = T // GRID_W
    nr = TM // GRID_W
    yv = lambda y: y.reshape(B, RW_PAIRS, GRID_W, rows * LANES)
    gbv = gb.reshape(B, GRID_W, rows * 2 * RW_W)
    row1 = lambda shape: pl.BlockSpec(shape, lambda b, i: (0,) * len(shape))
    mod = lambda k: pl.BlockSpec((1, 1, D_MODEL), lambda b, i: (b, 0, k))
    return pl.pallas_call(
        _mix_out_kernel,
        grid=(B, nlt),
        in_specs=[
            pl.BlockSpec((1, TM, D_MODEL), lambda b, i: (b, i, 0)),
            pl.BlockSpec((2, 1, TM, ML_W), lambda b, i: (0, b, i, 0)),
            pl.BlockSpec((1, TM, ML_W), lambda b, i: (b, i + nct, 3)),
            pl.BlockSpec((1, RW_PAIRS, GRID_W, nr * LANES), lambda b, i: (b, 0, 0, i)),
            pl.BlockSpec((1, RW_PAIRS, GRID_W, nr * LANES), lambda b, i: (b, 0, 0, i)),
            pl.BlockSpec((1, GRID_W, nr * 2 * RW_W), lambda b, i: (b, 0, i)),
            mod(2), mod(3), mod(4),
            row1((1, ML_W)), row1((1, RW_W)), row1((1, RW_W)), row1((1, D_MODEL)),
            row1((D_MODEL, D_MODEL)), row1((D_MODEL, D_MODEL)), row1((D_MODEL, LANES)),
        ],
        out_specs=[
            pl.BlockSpec((1, TM, D_MODEL), lambda b, i: (b, i, 0)),
            pl.BlockSpec((1, TM, D_MODEL), lambda b, i: (b, i, 0)),
            pl.BlockSpec((1, N_EXPERTS, TM), lambda b, i: (b, 0, i)),
        ],
        out_shape=[jax.ShapeDtypeStruct((B, T, D_MODEL), F32),
                   jax.ShapeDtypeStruct((B, T, D_MODEL), BF16),
                   jax.ShapeDtypeStruct((B, N_EXPERTS, T), F32)],
        compiler_params=_cparams(2),
        name="mix_out",
    )(x, h_ml, u_ml, yv(y_f), yv(y_b), gbv, mods3, mods3, mods3,
      ml_norm_g, ln_g, ln_b, norm2_g, whi, wlo, router_pad)


def _route_kernel(aff_ref, pos_ref, *, cap):
    a = aff_ref[0]
    T = a.shape[1]
    as_f32 = lambda bits: lax.bitcast_convert_type(bits, F32)

    def body(i, thr):
        cand = thr | jnp.left_shift(jnp.int32(1), 30 - i)
        cnt = jnp.sum(jnp.where(a >= as_f32(cand), 1.0, 0.0), axis=1, keepdims=True)
        return jnp.where(cnt >= cap, cand, thr)

    thr = lax.fori_loop(0, 31, body, jnp.zeros((N_EXPERTS, 1), I32))
    gt = a >= as_f32(thr + 1)
    eq = (a >= as_f32(thr)) & jnp.logical_not(gt)
    need = cap - jnp.sum(jnp.where(gt, 1.0, 0.0), axis=1, keepdims=True)
    tri = _tri(TM, False, False, BF16)

    def prefix_excl(mask):
        outs = []
        carry = jnp.zeros((N_EXPERTS, 1), F32)
        for blk in range(T // TM):
            seg = jnp.where(mask[:, blk * TM:(blk + 1) * TM], 1.0, 0.0)
            inc = _dot(seg.astype(BF16), tri)
            outs.append(inc - seg + carry)
            carry = carry + jnp.sum(seg, axis=1, keepdims=True)
        return jnp.concatenate(outs, axis=1)

    chosen = gt | (eq & (prefix_excl(eq) < need))
    pos_ref[0] = jnp.where(chosen, prefix_excl(chosen), -1.0).astype(I32)


def _route_call(aff_t, cap):
    B, E, T = aff_t.shape
    return pl.pallas_call(
        functools.partial(_route_kernel, cap=cap),
        grid=(B,),
        in_specs=[pl.BlockSpec((1, E, T), lambda b: (b, 0, 0))],
        out_specs=pl.BlockSpec((1, E, T), lambda b: (b, 0, 0)),
        out_shape=jax.ShapeDtypeStruct((B, E, T), I32),
        compiler_params=_cparams(1),
        name="route",
    )(aff_t)


def _gather_kernel(pos_ref, hn_ref, o_ref, *, cap):
    T = hn_ref.shape[1]
    slot = lax.broadcasted_iota(I32, (cap, T), 0)
    onehot = jnp.where(pos_ref[0] == slot, 1.0, 0.0).astype(BF16)
    o_ref[0] = _dot(onehot, hn_ref[0]).astype(BF16)


def _gather_call(pos, hn, cap):
    B, E, T = pos.shape
    return pl.pallas_call(
        functools.partial(_gather_kernel, cap=cap),
        grid=(B, E),
        in_specs=[pl.BlockSpec((1, 1, T), lambda b, e: (b * E + e, 0, 0)),
                  pl.BlockSpec((1, T, D_MODEL), lambda b, e: (b, 0, 0))],
        out_specs=pl.BlockSpec((1, cap, D_MODEL), lambda b, e: (e, b, 0)),
        out_shape=jax.ShapeDtypeStruct((E, B * cap, D_MODEL), BF16),
        compiler_params=_cparams(2),
        name="gather",
    )(pos.reshape(B * E, 1, T), hn)


def _ffn_kernel(x_ref, w1_ref, w3_ref, w2_ref, o_ref, acc_ref):
    f = pl.program_id(2)

    @pl.when(f == 0)
    def _():
        acc_ref[...] = jnp.zeros_like(acc_ref)

    x = x_ref[0]
    h1 = _dot(x, w1_ref[0].astype(BF16))
    h3 = _dot(x, w3_ref[0].astype(BF16))
    hid = (_silu(h1) * h3).astype(BF16)
    acc_ref[...] += _dot(hid, w2_ref[0].astype(BF16))

    @pl.when(f == pl.num_programs(2) - 1)
    def _():
        o_ref[0] = acc_ref[...].astype(BF16)


def _ffn_call(xg, w1, w3, w2):
    E, M, _ = xg.shape
    tm = min(M, 2048)
    nf = D_EXPERT // FFN_TF
    return pl.pallas_call(
        _ffn_kernel,
        grid=(E, M // tm, nf),
        in_specs=[pl.BlockSpec((1, tm, D_MODEL), lambda e, m, f: (e, m, 0)),
                  pl.BlockSpec((1, D_MODEL, FFN_TF), lambda e, m, f: (e, 0, f)),
                  pl.BlockSpec((1, D_MODEL, FFN_TF), lambda e, m, f: (e, 0, f)),
                  pl.BlockSpec((1, FFN_TF, D_MODEL), lambda e, m, f: (e, f, 0))],
        out_specs=pl.BlockSpec((1, tm, D_MODEL), lambda e, m, f: (e, m, 0)),
        out_shape=jax.ShapeDtypeStruct((E, M, D_MODEL), BF16),
        scratch_shapes=[pltpu.VMEM((tm, D_MODEL), F32)],
        compiler_params=_cparams(3),
        name="ffn",
    )(xg, w1, w3, w2)


def _combine_kernel(pos_ref, aff_ref, y_ref, h1_ref, g2_ref, fg_ref, o_ref, *, cap):
    pos = pos_ref[0]
    aff = aff_ref[0]
    slot = lax.broadcasted_iota(I32, (cap, TM), 0)
    moe = jnp.zeros((TM, D_MODEL), F32)
    for e in range(N_EXPERTS):
        w = jnp.where(pos[e:e + 1, :] == slot, aff[e:e + 1, :], 0.0).astype(BF16)
        moe = moe + _dot(w, y_ref[e], TN)
    h2 = h1_ref[0] + g2_ref[0] * moe
    o_ref[0] = _rmsnorm(h2, fg_ref[...])


def _combine_call(pos, aff_t, ys, h1, mods3, final_g, cap):
    B, E, T = pos.shape
    return pl.pallas_call(
        functools.partial(_combine_kernel, cap=cap),
        grid=(B, T // TM),
        in_specs=[pl.BlockSpec((1, E, TM), lambda b, i: (b, 0, i)),
                  pl.BlockSpec((1, E, TM), lambda b, i: (b, 0, i)),
                  pl.BlockSpec((E, cap, D_MODEL), lambda b, i: (0, b, 0)),
                  pl.BlockSpec((1, TM, D_MODEL), lambda b, i: (b, i, 0)),
                  pl.BlockSpec((1, 1, D_MODEL), lambda b, i: (b, 0, 5)),
                  pl.BlockSpec((1, D_MODEL), lambda b, i: (0, 0))],
        out_specs=pl.BlockSpec((1, TM, D_MODEL), lambda b, i: (b, i, 0)),
        out_shape=jax.ShapeDtypeStruct((B, T, D_MODEL), F32),
        compiler_params=_cparams(2),
        name="combine",
    )(pos, aff_t, ys, h1, mods3, final_g)


def _pad_cols(w, n):
    return jnp.pad(w, ((0, 0), (0, n - w.shape[1])))


def _hi_lo(w):
    hi = w.astype(BF16)
    return hi, (w - hi.astype(F32)).astype(BF16)


def _both_dirs(up):
    z = jnp.zeros_like(up[0])
    return jnp.concatenate([jnp.concatenate([up[0], z], axis=1),
                            jnp.concatenate([z, up[1]], axis=1)], axis=0)


def kernel(x, c, ctx, c_ctx, mod_w, mod_b, norm1_g, w_in, ml_conv_w, ml_conv_b, ml_gate_b, ml_norm_g,
           rw_mu, rw_w0, rw_w_up, rw_a0, rw_a_up, rw_g_up, rw_k_k, rw_k_a, rw_r_k, rw_ln_g, rw_ln_b,
           w_out, norm2_g, router_w, exp_w1, exp_w3, exp_w2, final_g):
    B, T, D = x.shape
    t_ctx = ctx.shape[1]
    assert D == D_MODEL and T % TM == 0 and t_ctx % TM == 0 and TM % (T // GRID_W) == 0
    assert mod_w.shape[0] == 1 and B < 16
    nct, nlt = t_ctx // TM, T // TM
    cap = EC_FACTOR * T // N_EXPERTS
    ml_cols = 4 * ML_W + ML_GATES

    cc = jnp.concatenate([c, c_ctx[None, :], jnp.zeros((16 - B - 1, D), F32)], axis=0)
    mods = _mod_call(cc, mod_w[0], mod_b)
    mods3 = mods.reshape(16, 1, 6 * D)
    g1n = norm1_g.reshape(1, D)

    w_ml_hi, w_ml_lo = _hi_lo(_pad_cols(w_in[0, :, :ml_cols], ML_N))
    w_rw_hi, w_rw_lo = _hi_lo(_pad_cols(w_in[0, :, ml_cols:], RW_N))
    u_ml = _proj_call("ml", x, ctx, mods3, g1n, w_ml_hi, w_ml_lo)
    u_rw = _proj_call("rw", x, ctx, mods3, g1n, w_rw_hi, w_rw_lo)

    h_ml = _mlstm_call(u_ml, ml_conv_w[0], ml_conv_b, _pad_cols(ml_gate_b, LANES), B, nct, nlt)

    gup = jnp.pad(rw_g_up[0], ((0, 256 - RW_LORA_G), (0, 0)))
    scan_in, gb = _rw_pre_call(
        u_rw, _pad_cols(rw_mu[0], RW_N), rw_w0[0].reshape(1, 2 * RW_W), _both_dirs(rw_w_up[0]),
        rw_a0[0].reshape(1, 2 * RW_W), _both_dirs(rw_a_up[0]), gup,
        rw_k_k, rw_k_a, rw_r_k[0].reshape(1, RW_W), B, nct, nlt)
    y_f, y_b = _rw_scan_call(scan_in, B, t_ctx, T)

    wo_hi, wo_lo = _hi_lo(w_out[0])
    h1, hn, aff_t = _mix_out_call(
        x, h_ml, u_ml, y_f, y_b, gb, mods3, ml_norm_g, rw_ln_g, rw_ln_b, norm2_g,
        wo_hi, wo_lo, _pad_cols(router_w[0], LANES), nct)

    pos = _route_call(aff_t, cap)
    xg = _gather_call(pos, hn, cap)
    ys = _ffn_call(xg, exp_w1[0], exp_w3[0], exp_w2[0])
    return _combine_call(pos, aff_t, ys, h1, mods3, final_g.reshape(1, D), cap)
```

```python
import functools

import jax
import jax.numpy as jnp
from jax import lax
from jax.experimental import pallas as pl
from jax.experimental.pallas import tpu as pltpu

F32 = jnp.float32
BF16 = jnp.bfloat16
I32 = jnp.int32

D_MODEL = 1024
GRID_W = 64
EPS = 1e-6
ML_W = 512
ML_HEADS = 4
ML_DH = 128
ML_GATES = 16
ML_N = 4 * ML_W + 128
RW_W = 512
RW_HEADS = 8
RW_DH = 64
RW_LORA = 64
RW_LORA_G = 160
RW_LN_EPS = 64e-5
RW_COLS = 3 * RW_W + 2 * RW_LORA + 2 * RW_LORA + RW_LORA_G
RW_N = 2048
RW_PAIRS = RW_HEADS // 2
N_EXPERTS = 16
EC_FACTOR = 2
D_EXPERT = 2816

LANES = 128
SUBLANES = 8
TM = 256
RW_CHUNK = 64
RW_PAIRS_PER_STEP = 4
RW_INV_BASE = 4
FFN_TF = 256
VMEM_LIMIT = 56 * 1024 * 1024

NN = (((1,), (0,)), ((), ()))
NT = (((1,), (1,)), ((), ()))
TN = (((0,), (0,)), ((), ()))


def _cparams(n_axes):
    return pltpu.CompilerParams(
        dimension_semantics=("arbitrary",) * n_axes, vmem_limit_bytes=VMEM_LIMIT)


def _dot(a, b, dn=NN):
    return lax.dot_general(a, b, dn, preferred_element_type=F32)


def _split2(a):
    hi = a.astype(BF16)
    lo = (a - hi.astype(F32)).astype(BF16)
    return hi, lo


def _split3(a):
    hi = a.astype(BF16)
    r1 = a - hi.astype(F32)
    mid = r1.astype(BF16)
    lo = (r1 - mid.astype(F32)).astype(BF16)
    return hi, mid, lo


def _dot1(a, b, dn=NN):
    return _dot(a.astype(BF16), b.astype(BF16), dn)


def _dot3(a, b, dn=NN):
    ah, al = _split2(a)
    bh, bl = _split2(b)
    return _dot(ah, bh, dn) + (_dot(ah, bl, dn) + _dot(al, bh, dn))


def _dot_sel(sel_bf16, b, dn=NN):
    bh, bm, bl = _split3(b)
    return _dot(sel_bf16, bh, dn) + (_dot(sel_bf16, bm, dn) + _dot(sel_bf16, bl, dn))


def _dot_rsel(a, sel_bf16, dn=NN):
    ah, am, al = _split3(a)
    return _dot(ah, sel_bf16, dn) + (_dot(am, sel_bf16, dn) + _dot(al, sel_bf16, dn))


def _silu(x):
    return x * jax.nn.sigmoid(x)


def _log_sigmoid(x):
    return jnp.minimum(x, 0.0) - jnp.log1p(jnp.exp(-jnp.abs(x)))


def _rmsnorm(x, g):
    return x * lax.rsqrt(jnp.mean(x * x, axis=-1, keepdims=True) + EPS) * g


def _tri(n, lower, strict, dtype):
    r = lax.broadcasted_iota(I32, (n, n), 0)
    c = lax.broadcasted_iota(I32, (n, n), 1)
    if lower:
        m = (c < r) if strict else (c <= r)
    else:
        m = (c > r) if strict else (c >= r)
    return m if dtype is None else m.astype(dtype)


def _shift_rows(z, prev_row, next_row):
    n = z.shape[0]
    row = lax.broadcasted_iota(I32, z.shape, 0)
    zm1 = jnp.where(row == 0, prev_row, pltpu.roll(z, 1, 0))
    zp1 = jnp.where(row == n - 1, next_row, pltpu.roll(z, n - 1, 0))
    return zm1, zp1


def _mod_kernel(c_ref, w_ref, b_ref, o_ref):
    o_ref[...] = _dot3(_silu(c_ref[...]), w_ref[...]) + b_ref[...]


def _mod_call(cc, mod_w, mod_b):
    n = mod_w.shape[1]
    tn = 1024
    return pl.pallas_call(
        _mod_kernel,
        grid=(n // tn,),
        in_specs=[pl.BlockSpec((16, D_MODEL), lambda j: (0, 0)),
                  pl.BlockSpec((D_MODEL, tn), lambda j: (0, j)),
                  pl.BlockSpec((1, tn), lambda j: (0, j))],
        out_specs=pl.BlockSpec((16, tn), lambda j: (0, j)),
        out_shape=jax.ShapeDtypeStruct((16, n), F32),
        compiler_params=_cparams(1),
        name="mod",
    )(cc, mod_w, mod_b)


def _proj_ml_kernel(ctx_ref, x_ref, sh_ref, sc_ref, g_ref, w_ref, o_ref, *, nct):
    i = pl.program_id(1)
    xin = jnp.where(i < nct, ctx_ref[0], x_ref[0])
    xn = _rmsnorm(xin, g_ref[...]) * (1.0 + sc_ref[0]) + sh_ref[0]
    o_ref[0] = _dot(xn.astype(BF16), w_ref[...])


def _proj_rw_kernel(ctx_ref, x_ref, sh_ref, sc_ref, g_ref, w_ref, o_ref, *, nct, rows):
    i = pl.program_id(1)
    xr = x_ref[0]
    ncb = TM // rows
    xcm = jnp.concatenate([xr[:, c * D_MODEL:(c + 1) * D_MODEL] for c in range(ncb)], axis=0)
    xin = jnp.where(i < nct, ctx_ref[0], xcm)
    xn = _rmsnorm(xin, g_ref[...]) * (1.0 + sc_ref[0]) + sh_ref[0]
    o_ref[0] = _dot(xn.astype(BF16), w_ref[...])


def _proj_call(kind, x, ctx, mods3, norm_g, w):
    B, T, _ = x.shape
    nct = ctx.shape[1] // TM
    nlt = T // TM
    n = w.shape[1]
    rows = T // GRID_W
    mrow = lambda b, i: jnp.where(i < nct, B, b)
    ctx_spec = pl.BlockSpec((1, TM, D_MODEL), lambda b, i: (b, jnp.minimum(i, nct - 1), 0))
    if kind == "ml":
        body = functools.partial(_proj_ml_kernel, nct=nct)
        x_in = x
        x_spec = pl.BlockSpec((1, TM, D_MODEL), lambda b, i: (b, jnp.maximum(i - nct, 0), 0))
    else:
        body = functools.partial(_proj_rw_kernel, nct=nct, rows=rows)
        x_in = x.reshape(B, rows, GRID_W * D_MODEL)
        x_spec = pl.BlockSpec((1, rows, (TM // rows) * D_MODEL),
                              lambda b, i: (b, 0, jnp.maximum(i - nct, 0)))
    return pl.pallas_call(
        body,
        grid=(B, nct + nlt),
        in_specs=[ctx_spec, x_spec,
                  pl.BlockSpec((1, 1, D_MODEL), lambda b, i: (mrow(b, i), 0, 0)),
                  pl.BlockSpec((1, 1, D_MODEL), lambda b, i: (mrow(b, i), 0, 1)),
                  pl.BlockSpec((1, D_MODEL), lambda b, i: (0, 0)),
                  pl.BlockSpec((D_MODEL, n), lambda b, i: (0, 0))],
        out_specs=pl.BlockSpec((1, TM, n), lambda b, i: (b, i, 0)),
        out_shape=jax.ShapeDtypeStruct((B, (nct + nlt) * TM, n), F32),
        compiler_params=_cparams(2),
        name="proj_" + kind,
    )(ctx, x_in, mods3, mods3, norm_g, w)


def _seq_chunk(d, j, nct, ntot):
    bwd = jnp.where(j < nct, nct - 1 - j, ntot - 1 - (j - nct))
    return jnp.where(d == 0, j, bwd)


def _mlstm_kernel(u_ref, hp_ref, hn_ref, cw_ref, cb_ref, gb_ref, o_ref, c_ref, n_ref, m_ref,
                  *, nct, ntot):
    d = pl.program_id(1)
    j = pl.program_id(2)
    c = _seq_chunk(d, j, nct, ntot)
    L = TM

    @pl.when(j == 0)
    def _():
        c_ref[...] = jnp.zeros_like(c_ref)
        n_ref[...] = jnp.zeros_like(n_ref)
        m_ref[...] = jnp.zeros_like(m_ref)

    seg_first = (c == 0) | (c == nct)
    seg_last = (c == nct - 1) | (c == ntot - 1)
    zqk = u_ref[0, :, 0:2 * ML_W]
    prev_row = jnp.where(seg_first, 0.0, hp_ref[0, SUBLANES - 1:SUBLANES, :])
    next_row = jnp.where(seg_last, 0.0, hn_ref[0, 0:1, :])
    zm1, zp1 = _shift_rows(zqk, prev_row, next_row)
    cw = cw_ref[...]
    qk = _silu(zm1 * cw[0:1] + zqk * cw[1:2] + zp1 * cw[2:3] + cb_ref[...])
    zv = u_ref[0, :, 2 * ML_W:3 * ML_W]

    g = u_ref[0, :, 4 * ML_W:4 * ML_W + LANES] + gb_ref[...]
    lane = lax.broadcasted_iota(I32, (L, LANES), 1)
    is_f = ((lane >= 4) & (lane < 8)) | ((lane >= 12) & (lane < 16))
    q_all = jnp.where(is_f, _log_sigmoid(g), g)
    causal = jnp.where(d == 0, _tri(L, True, False, F32), _tri(L, False, False, F32))
    b_all = _dot_sel(causal.astype(BF16), q_all)
    q_t = q_all.T
    b_t = b_all.T
    keep = causal > 0.5

    def lane_pick(a, idx):
        return jnp.sum(jnp.where(lane == idx, a, 0.0), axis=-1, keepdims=True)

    def row_pick(a_t, h):
        return jnp.where(d == 0, a_t[h:h + 1, :], a_t[8 + h:9 + h, :])

    hs = range(ML_HEADS)
    sl = [slice(h * ML_DH, (h + 1) * ML_DH) for h in hs]
    qf = [qk[:, sl[h]] for h in hs]
    q = [qf[h].astype(BF16) for h in hs]
    k = [qk[:, ML_W + h * ML_DH:ML_W + (h + 1) * ML_DH] * (ML_DH ** -0.5) for h in hs]
    kb = [k[h].astype(BF16) for h in hs]
    v = [zv[:, sl[h]] for h in hs]
    li_col = [lane_pick(q_all, h + 8 * d) for h in hs]
    b_col = [lane_pick(b_all, 4 + h + 8 * d) for h in hs]
    li_row = [row_pick(q_t, h) for h in hs]
    b_row = [row_pick(b_t, 4 + h) for h in hs]
    m_prev = [m_ref[h, 0:1, 0:1] for h in hs]
    C = [c_ref[h] for h in hs]
    n_row = [n_ref[h, 0:1, :] for h in hs]

    qk_s = [_dot(q[h], kb[h], NT) for h in hs]
    q_c = [_dot(q[h], C[h].astype(BF16), NT) for h in hs]
    dm = [jnp.where(keep, b_col[h] - b_row[h] + li_row[h], -jnp.inf) for h in hs]
    inter = [b_col[h] + m_prev[h] for h in hs]
    m_t = [jnp.maximum(inter[h], jnp.max(dm[h], axis=-1, keepdims=True)) for h in hs]
    wts = [jnp.exp(dm[h] - m_t[h]) * qk_s[h] for h in hs]
    w_inter = [jnp.exp(inter[h] - m_t[h]) for h in hs]
    num = [_dot(wts[h].astype(BF16), v[h].astype(BF16)) + w_inter[h] * q_c[h] for h in hs]
    for h in hs:
        den = (jnp.sum(wts[h], axis=-1, keepdims=True)
               + w_inter[h] * jnp.sum(qf[h] * n_row[h], axis=-1, keepdims=True))
        o_ref[0, 0, :, sl[h]] = num[h] / jnp.maximum(jnp.abs(den), jnp.exp(-m_t[h]))

    b_end = [jnp.where(d == 0, b_col[h][L - 1:L, :], b_col[h][0:1, :]) for h in hs]
    dec = [b_end[h] - b_col[h] + li_col[h] for h in hs]
    m_new = [jnp.maximum(b_end[h] + m_prev[h], jnp.max(dec[h], axis=0, keepdims=True)) for h in hs]
    wk = [jnp.exp(dec[h] - m_new[h]) for h in hs]
    s_old = [jnp.exp(b_end[h] + m_prev[h] - m_new[h]) for h in hs]
    c_upd = [_dot((v[h] * wk[h]).astype(BF16), kb[h], TN) for h in hs]
    for h in hs:
        c_ref[h] = s_old[h] * C[h] + c_upd[h]
        n_ref[h] = jnp.broadcast_to(
            s_old[h] * n_row[h] + jnp.sum(wk[h] * k[h], axis=0, keepdims=True), (SUBLANES, ML_DH))
        m_ref[h] = jnp.broadcast_to(m_new[h], (SUBLANES, LANES))


def _mlstm_call(u_ml, conv_w, conv_b, gate_b, B, nct, nlt):
    ntot = nct + nlt
    hb = TM // SUBLANES
    nrb = ntot * hb
    chunk = lambda d, j: _seq_chunk(d, j, nct, ntot)
    out_blk = lambda d, j: jnp.where(j < nct, d * (nlt - 1), chunk(d, j) - nct)
    return pl.pallas_call(
        functools.partial(_mlstm_kernel, nct=nct, ntot=ntot),
        grid=(B, 2, ntot),
        in_specs=[
            pl.BlockSpec((1, TM, ML_N), lambda b, d, j: (b, chunk(d, j), 0)),
            pl.BlockSpec((1, SUBLANES, 2 * ML_W),
                         lambda b, d, j: (b, jnp.maximum(chunk(d, j) * hb - 1, 0), 0)),
            pl.BlockSpec((1, SUBLANES, 2 * ML_W),
                         lambda b, d, j: (b, jnp.minimum((chunk(d, j) + 1) * hb, nrb - 1), 0)),
            pl.BlockSpec((3, 2 * ML_W), lambda b, d, j: (0, 0)),
            pl.BlockSpec((1, 2 * ML_W), lambda b, d, j: (0, 0)),
            pl.BlockSpec((1, LANES), lambda b, d, j: (0, 0)),
        ],
        out_specs=pl.BlockSpec((1, 1, TM, ML_W), lambda b, d, j: (d, b, out_blk(d, j), 0)),
        out_shape=jax.ShapeDtypeStruct((2, B, nlt * TM, ML_W), F32),
        scratch_shapes=[pltpu.VMEM((ML_HEADS, ML_DH, ML_DH), F32),
                        pltpu.VMEM((ML_HEADS, SUBLANES, ML_DH), F32),
                        pltpu.VMEM((ML_HEADS, SUBLANES, LANES), F32)],
        compiler_params=_cparams(3),
        name="mlstm",
    )(u_ml, u_ml, u_ml, conv_w, conv_b, gate_b)


def _rw_pre_kernel(u_ref, hp_ref, hn_ref, mu_ref, w0_ref, wup_ref, a0_ref, aup_ref, gup_ref,
                   kk_ref, ka_ref, rk_ref, scan_ref, gb_ref, *, nct, ntot):
    c = pl.program_id(1)
    seg_first = (c == 0) | (c == nct)
    seg_last = (c == nct - 1) | (c == ntot - 1)
    z = u_ref[0]
    prev_row = jnp.where(seg_first, 0.0, hp_ref[0, SUBLANES - 1:SUBLANES, :])
    next_row = jnp.where(seg_last, 0.0, hn_ref[0, 0:1, :])
    zm1, zp1 = _shift_rows(z, prev_row, next_row)
    mu = mu_ref[...]
    xs = z + mu[0:1] * (zm1 - z) + mu[1:2] * (zp1 - z)

    r = xs[:, 0:RW_W]
    k = xs[:, RW_W:2 * RW_W]
    v = xs[:, 2 * RW_W:3 * RW_W]
    wd = xs[:, 3 * RW_W:3 * RW_W + 2 * RW_LORA]
    ad = xs[:, 3 * RW_W + 2 * RW_LORA:3 * RW_W + 4 * RW_LORA]
    gd = xs[:, 3 * RW_W + 4 * RW_LORA:RW_N]

    w_pre = w0_ref[...] + _dot1(jnp.tanh(wd), wup_ref[...])
    log_w = -jax.nn.softplus(-w_pre) - 0.5
    lw = -jnp.exp(log_w)
    a = jax.nn.sigmoid(a0_ref[...] + _dot1(ad, aup_ref[...]))
    g = _dot1(jax.nn.sigmoid(gd), gup_ref[...])

    hr = lax.broadcasted_iota(I32, (RW_W, RW_W), 0) // RW_DH
    hc = lax.broadcasted_iota(I32, (RW_W, RW_W), 1) // RW_DH
    head_ones = (hr == hc).astype(BF16)
    kk = k * kk_ref[...]
    ss = _dot_rsel(kk * kk, head_ones)
    kk = kk / jnp.maximum(jnp.sqrt(ss), 1e-12)
    ka = ka_ref[...]
    k_f = k * (1.0 + (a[:, 0:RW_W] - 1.0) * ka)
    k_b = k * (1.0 + (a[:, RW_W:2 * RW_W] - 1.0) * ka)
    rk = rk_ref[...]
    bonus = (_dot_rsel(r * k_f * rk, head_ones) + _dot_rsel(r * k_b * rk, head_ones)) * v

    cols = (r, v, kk, lw[:, 0:RW_W], k_f, kk * a[:, 0:RW_W],
            lw[:, RW_W:2 * RW_W], k_b, kk * a[:, RW_W:2 * RW_W])
    for p in range(RW_PAIRS):
        for qi, arr in enumerate(cols):
            scan_ref[0, p, :, qi * LANES:(qi + 1) * LANES] = arr[:, p * LANES:(p + 1) * LANES]
    gb_ref[0, :, 0:RW_W] = g
    gb_ref[0, :, RW_W:2 * RW_W] = bonus


def _rw_pre_call(u_rw, mu, w0, wup, a0, aup, gup, k_k, k_a, r_k, B, nct, nlt):
    ntot = nct + nlt
    hb = TM // SUBLANES
    nrb = ntot * hb
    full = lambda shape: pl.BlockSpec(shape, lambda b, c: (0,) * len(shape))
    return pl.pallas_call(
        functools.partial(_rw_pre_kernel, nct=nct, ntot=ntot),
        grid=(B, ntot),
        in_specs=[
            pl.BlockSpec((1, TM, RW_N), lambda b, c: (b, c, 0)),
            pl.BlockSpec((1, SUBLANES, RW_N), lambda b, c: (b, jnp.maximum(c * hb - 1, 0), 0)),
            pl.BlockSpec((1, SUBLANES, RW_N),
                         lambda b, c: (b, jnp.minimum((c + 1) * hb, nrb - 1), 0)),
            full((2, RW_N)), full((1, 2 * RW_W)), full((2 * RW_LORA, 2 * RW_W)),
            full((1, 2 * RW_W)), full((2 * RW_LORA, 2 * RW_W)), full((256, RW_W)),
            full((1, RW_W)), full((1, RW_W)), full((1, RW_W)),
        ],
        out_specs=[
            pl.BlockSpec((1, RW_PAIRS, TM, 9 * LANES), lambda b, c: (b, 0, c, 0)),
            pl.BlockSpec((1, TM, 2 * RW_W), lambda b, c: (b, jnp.maximum(c - nct, 0), 0)),
        ],
        out_shape=[jax.ShapeDtypeStruct((B, RW_PAIRS, ntot * TM, 9 * LANES), F32),
                   jax.ShapeDtypeStruct((B, nlt * TM, 2 * RW_W), F32)],
        compiler_params=_cparams(2),
        name="rw_pre",
    )(u_rw, u_rw, u_rw, mu, w0, wup, a0, aup, gup, k_k, k_a, r_k)


def _rw_chunks(blks, states, lowers):
    n = len(blks)
    L = blks[0].shape[0]
    idx = range(n)
    hidx = [(i, h2) for i in idx for h2 in range(2)]
    col = lambda i, c: blks[i][:, c * LANES:(c + 1) * LANES]
    r, v, kk, lw, kd, bh = ([col(i, c) for i in idx] for c in range(6))
    tri = {lo: (_tri(L, lo, False, None), _tri(L, lo, True, None)) for lo in set(lowers)}
    incl = [tri[lo][0] for lo in lowers]
    strict = [tri[lo][1] for lo in lowers]
    logp = [_dot_sel(incl[i].astype(BF16), lw[i]) for i in idx]
    logp_end = [logp[i][L - 1:L, :] if lowers[i] else logp[i][0:1, :] for i in idx]
    p_inv = [jnp.exp(-logp[i]) for i in idx]
    a_t = [-kk[i] * jnp.exp(logp[i] - lw[i]) for i in idx]
    r_t = [r[i] * jnp.exp(logp[i]) for i in idx]
    bk = [jnp.concatenate([bh[i] * p_inv[i], kd[i] * p_inv[i]], axis=0) for i in idx]
    vv = [jnp.concatenate([v[i], v[i]], axis=0) for i in idx]
    lane = lax.broadcasted_iota(I32, (L, LANES), 1)
    in_head = (lane < RW_DH, lane >= RW_DH)
    eye = jnp.where(_tri(L, True, False, None) & _tri(L, False, False, None), 1.0, 0.0)
    row2 = lax.broadcasted_iota(I32, (L, 2 * L), 0)
    col2 = lax.broadcasted_iota(I32, (L, 2 * L), 1)
    src2 = jnp.where(col2 >= L, col2 - L, col2)
    ak_mask = {lo: ((src2 < row2) if lo else (src2 > row2)) & (col2 >= L) for lo in set(lowers)}
    r_mask = {lo: (src2 <= row2) if lo else (src2 >= row2) for lo in set(lowers)}

    a_s = [_dot1(a_t[i], states[i], NT) for i in idx]
    r_s = [_dot1(r_t[i], states[i], NT) for i in idx]
    g_a = [_dot1(jnp.where(in_head[h], a_t[i], 0.0), bk[i], NT) for i, h in hidx]
    g_r = [_dot1(jnp.where(in_head[h], r_t[i], 0.0), bk[i], NT) for i, h in hidx]
    n_mat = [jnp.where(strict[i], g_a[q][:, 0:L], 0.0) for q, (i, h) in enumerate(hidx)]
    w_rhs = [a_s[i] + _dot1(jnp.where(ak_mask[lowers[i]], g_a[q], 0.0), vv[i])
             for q, (i, h) in enumerate(hidx)]
    hq = range(len(hidx))
    rowb = lax.broadcasted_iota(I32, (L, L), 0)
    colb = lax.broadcasted_iota(I32, (L, L), 1)
    base = RW_INV_BASE
    n_d = [jnp.where(rowb // base == colb // base, n_mat[q], 0.0) for q in hq]
    inv = [eye + n_d[q] for q in hq]
    n_d2 = [_dot1(n_d[q], n_d[q]) for q in hq]
    inv = [inv[q] + _dot1(inv[q], n_d2[q]) for q in hq]
    blk = base
    while blk < L:
        link = (rowb // (2 * blk) == colb // (2 * blk)) & (rowb // blk != colb // blk)
        t_m = [_dot1(inv[q], jnp.where(link, n_mat[q], 0.0)) for q in hq]
        inv = [inv[q] + _dot1(t_m[q], inv[q]) for q in hq]
        blk *= 2
    u_h = [_dot1(inv[q], w_rhs[q]) for q in hq]
    u = [jnp.where(in_head[0], u_h[2 * i], u_h[2 * i + 1]) for i in idx]
    uv = [jnp.concatenate([u[i], v[i]], axis=0) for i in idx]
    y_h = [r_s[i] + _dot1(jnp.where(r_mask[lowers[i]], g_r[q], 0.0), uv[i])
           for q, (i, h) in enumerate(hidx)]
    y = [jnp.where(in_head[0], y_h[2 * i], y_h[2 * i + 1]) for i in idx]

    rr = lax.broadcasted_iota(I32, (LANES, LANES), 0) // RW_DH
    cc = lax.broadcasted_iota(I32, (LANES, LANES), 1) // RW_DH
    s_new = []
    for i in idx:
        to_end = jnp.exp(logp_end[i] - logp[i])
        bk_end = jnp.concatenate([bh[i] * to_end, kd[i] * to_end], axis=0)
        s_new.append(jnp.where(
            rr == cc, states[i] * jnp.exp(logp_end[i]) + _dot1(uv[i], bk_end, TN), 0.0))
    return y, s_new


def _rw_scan_kernel(f_ref, b_ref, yf_ref, yb_ref, s_ref):
    j = pl.program_id(2)

    @pl.when(j == 0)
    def _():
        s_ref[...] = jnp.zeros_like(s_ref)

    blks, states, lowers = [], [], []
    for p in range(RW_PAIRS_PER_STEP):
        fb = f_ref[0, p]
        bb = b_ref[0, p]
        blks += [fb[:, 0:6 * LANES],
                 jnp.concatenate([bb[:, 0:3 * LANES], bb[:, 6 * LANES:9 * LANES]], axis=1)]
        states += [s_ref[p, 0], s_ref[p, 1]]
        lowers += [True, False]
    y, s_new = _rw_chunks(blks, states, lowers)
    for p in range(RW_PAIRS_PER_STEP):
        yf_ref[0, p] = y[2 * p]
        yb_ref[0, p] = y[2 * p + 1]
        s_ref[p, 0] = s_new[2 * p]
        s_ref[p, 1] = s_new[2 * p + 1]


def _rw_scan_call(scan_in, B, t_ctx, t_lat):
    L = RW_CHUNK
    nct, nlt = t_ctx // L, t_lat // L
    ntot = nct + nlt
    fchunk = lambda j: j
    bchunk = lambda j: _seq_chunk(1, j, nct, ntot)
    pps = RW_PAIRS_PER_STEP
    in_blk = (1, pps, L, 9 * LANES)
    out_blk = (1, pps, L, LANES)
    out_shape = jax.ShapeDtypeStruct((B, RW_PAIRS, t_lat, LANES), F32)
    return pl.pallas_call(
        _rw_scan_kernel,
        grid=(B, RW_PAIRS // pps, ntot),
        in_specs=[pl.BlockSpec(in_blk, lambda b, p, j: (b, p, fchunk(j), 0)),
                  pl.BlockSpec(in_blk, lambda b, p, j: (b, p, bchunk(j), 0))],
        out_specs=[
            pl.BlockSpec(out_blk, lambda b, p, j: (b, p, jnp.maximum(fchunk(j) - nct, 0), 0)),
            pl.BlockSpec(out_blk,
                         lambda b, p, j: (b, p, jnp.where(j < nct, nlt - 1, bchunk(j) - nct), 0)),
        ],
        out_shape=[out_shape, out_shape],
        scratch_shapes=[pltpu.VMEM((pps, 2, LANES, LANES), F32)],
        compiler_params=_cparams(3),
        name="rw_scan",
    )(scan_in, scan_in)


def _mix_out_kernel(x_ref, h_ref, zo_ref, yf_ref, yb_ref, gb_ref, g1_ref, sh2_ref, sc2_ref,
                    mlg_ref, lng_ref, lnb_ref, n2g_ref, wo_ref, rw_ref,
                    h1_ref, hn_ref, aff_ref):
    hm = h_ref[0, 0] + h_ref[1, 0]
    parts = []
    for h in range(ML_HEADS):
        hh = hm[:, h * ML_DH:(h + 1) * ML_DH]
        parts.append(hh * lax.rsqrt(jnp.mean(hh * hh, axis=-1, keepdims=True) + EPS))
    ml = jnp.concatenate(parts, axis=1) * mlg_ref[...] * jax.nn.sigmoid(zo_ref[0])

    nr = TM // GRID_W
    yf = yf_ref[0]
    yb = yb_ref[0]
    y = jnp.concatenate(
        [jnp.concatenate([yf[p, :, jr * LANES:(jr + 1) * LANES] + yb[p, :, jr * LANES:(jr + 1) * LANES]
                          for p in range(RW_PAIRS)], axis=1) for jr in range(nr)], axis=0)
    gbm = gb_ref[0]
    gbr = jnp.concatenate([gbm[:, jr * 2 * RW_W:(jr + 1) * 2 * RW_W] for jr in range(nr)], axis=0)
    hr = lax.broadcasted_iota(I32, (RW_W, RW_W), 0) // RW_DH
    hc = lax.broadcasted_iota(I32, (RW_W, RW_W), 1) // RW_DH
    head_ones = (hr == hc).astype(BF16)
    mean = _dot_rsel(y, head_ones) * (1.0 / RW_DH)
    dy = y - mean
    var = _dot_rsel(dy * dy, head_ones) * (1.0 / RW_DH)
    rw = dy * lax.rsqrt(var + RW_LN_EPS) * lng_ref[...] + lnb_ref[...]
    rw = (rw + gbr[:, RW_W:2 * RW_W]) * gbr[:, 0:RW_W]

    mix = _dot(jnp.concatenate([ml, rw], axis=1).astype(BF16), wo_ref[...])
    h1 = x_ref[0] + g1_ref[0] * mix
    h1_ref[0] = h1
    hn = _rmsnorm(h1, n2g_ref[...]) * (1.0 + sc2_ref[0]) + sh2_ref[0]
    hn_ref[0] = hn.astype(BF16)
    logits = _dot3(hn, rw_ref[...])
    lane = lax.broadcasted_iota(I32, logits.shape, 1)
    logits = jnp.where(lane < N_EXPERTS, logits, -jnp.inf)
    e = jnp.exp(logits - jnp.max(logits, axis=-1, keepdims=True))
    aff = e / jnp.sum(e, axis=-1, keepdims=True)
    aff_ref[0] = aff.T[0:N_EXPERTS, :]


def _mix_out_call(x, h_ml, u_ml, y_f, y_b, gb, mods3, ml_norm_g, ln_g, ln_b, norm2_g,
                  wo, router_pad, nct):
    B, T, _ = x.shape
    nlt = T // TM
    rows = T // GRID_W
    nr = TM // GRID_W
    yv = lambda y: y.reshape(B, RW_PAIRS, GRID_W, rows * LANES)
    gbv = gb.reshape(B, GRID_W, rows * 2 * RW_W)
    row1 = lambda shape: pl.BlockSpec(shape, lambda b, i: (0,) * len(shape))
    mod = lambda k: pl.BlockSpec((1, 1, D_MODEL), lambda b, i: (b, 0, k))
    return pl.pallas_call(
        _mix_out_kernel,
        grid=(B, nlt),
        in_specs=[
            pl.BlockSpec((1, TM, D_MODEL), lambda b, i: (b, i, 0)),
            pl.BlockSpec((2, 1, TM, ML_W), lambda b, i: (0, b, i, 0)),
            pl.BlockSpec((1, TM, ML_W), lambda b, i: (b, i + nct, 3)),
            pl.BlockSpec((1, RW_PAIRS, GRID_W, nr * LANES), lambda b, i: (b, 0, 0, i)),
            pl.BlockSpec((1, RW_PAIRS, GRID_W, nr * LANES), lambda b, i: (b, 0, 0, i)),
            pl.BlockSpec((1, GRID_W, nr * 2 * RW_W), lambda b, i: (b, 0, i)),
            mod(2), mod(3), mod(4),
            row1((1, ML_W)), row1((1, RW_W)), row1((1, RW_W)), row1((1, D_MODEL)),
            row1((D_MODEL, D_MODEL)), row1((D_MODEL, LANES)),
        ],
        out_specs=[
            pl.BlockSpec((1, TM, D_MODEL), lambda b, i: (b, i, 0)),
            pl.BlockSpec((1, TM, D_MODEL), lambda b, i: (b, i, 0)),
            pl.BlockSpec((1, N_EXPERTS, TM), lambda b, i: (b, 0, i)),
        ],
        out_shape=[jax.ShapeDtypeStruct((B, T, D_MODEL), F32),
                   jax.ShapeDtypeStruct((B, T, D_MODEL), BF16),
                   jax.ShapeDtypeStruct((B, N_EXPERTS, T), F32)],
        compiler_params=_cparams(2),
        name="mix_out",
    )(x, h_ml, u_ml, yv(y_f), yv(y_b), gbv, mods3, mods3, mods3,
      ml_norm_g, ln_g, ln_b, norm2_g, wo, router_pad)


def _route_kernel(aff_ref, pos_ref, *, cap):
    a = aff_ref[0]
    T = a.shape[1]
    as_f32 = lambda bits: lax.bitcast_convert_type(bits, F32)

    def body(i, thr):
        cand = thr | jnp.left_shift(jnp.int32(1), 30 - i)
        cnt = jnp.sum(jnp.where(a >= as_f32(cand), 1.0, 0.0), axis=1, keepdims=True)
        return jnp.where(cnt >= cap, cand, thr)

    thr = lax.fori_loop(0, 31, body, jnp.zeros((N_EXPERTS, 1), I32))
    gt = a >= as_f32(thr + 1)
    eq = (a >= as_f32(thr)) & jnp.logical_not(gt)
    need = cap - jnp.sum(jnp.where(gt, 1.0, 0.0), axis=1, keepdims=True)
    tri = _tri(TM, False, False, BF16)

    def prefix_excl(mask):
        outs = []
        carry = jnp.zeros((N_EXPERTS, 1), F32)
        for blk in range(T // TM):
            seg = jnp.where(mask[:, blk * TM:(blk + 1) * TM], 1.0, 0.0)
            inc = _dot(seg.astype(BF16), tri)
            outs.append(inc - seg + carry)
            carry = carry + jnp.sum(seg, axis=1, keepdims=True)
        return jnp.concatenate(outs, axis=1)

    chosen = gt | (eq & (prefix_excl(eq) < need))
    pos_ref[0] = jnp.where(chosen, prefix_excl(chosen), -1.0).astype(I32)


def _route_call(aff_t, cap):
    B, E, T = aff_t.shape
    return pl.pallas_call(
        functools.partial(_route_kernel, cap=cap),
        grid=(B,),
        in_specs=[pl.BlockSpec((1, E, T), lambda b: (b, 0, 0))],
        out_specs=pl.BlockSpec((1, E, T), lambda b: (b, 0, 0)),
        out_shape=jax.ShapeDtypeStruct((B, E, T), I32),
        compiler_params=_cparams(1),
        name="route",
    )(aff_t)


def _gather_kernel(pos_ref, hn_ref, o_ref, *, cap):
    T = hn_ref.shape[1]
    tk = min(T, 512)
    slot = lax.broadcasted_iota(I32, (cap, tk), 0)
    acc = jnp.zeros((cap, D_MODEL), F32)
    for kc in range(T // tk):
        onehot = jnp.where(pos_ref[0, :, kc * tk:(kc + 1) * tk] == slot, 1.0, 0.0).astype(BF16)
        acc = acc + _dot(onehot, hn_ref[0, kc * tk:(kc + 1) * tk, :])
    o_ref[0] = acc.astype(BF16)


def _gather_call(pos, hn, cap):
    B, E, T = pos.shape
    return pl.pallas_call(
        functools.partial(_gather_kernel, cap=cap),
        grid=(B, E),
        in_specs=[pl.BlockSpec((1, 1, T), lambda b, e: (b * E + e, 0, 0)),
                  pl.BlockSpec((1, T, D_MODEL), lambda b, e: (b, 0, 0))],
        out_specs=pl.BlockSpec((1, cap, D_MODEL), lambda b, e: (e, b, 0)),
        out_shape=jax.ShapeDtypeStruct((E, B * cap, D_MODEL), BF16),
        compiler_params=_cparams(2),
        name="gather",
    )(pos.reshape(B * E, 1, T), hn)


def _ffn_kernel(x_ref, w1_ref, w3_ref, w2_ref, o_ref, acc_ref):
    f = pl.program_id(2)

    @pl.when(f == 0)
    def _():
        acc_ref[...] = jnp.zeros_like(acc_ref)

    x = x_ref[0]
    h1 = _dot(x, w1_ref[0].astype(BF16))
    h3 = _dot(x, w3_ref[0].astype(BF16))
    hid = (_silu(h1) * h3).astype(BF16)
    acc_ref[...] += _dot(hid, w2_ref[0].astype(BF16))

    @pl.when(f == pl.num_programs(2) - 1)
    def _():
        o_ref[0] = acc_ref[...].astype(BF16)


def _ffn_call(xg, w1, w3, w2):
    E, M, _ = xg.shape
    tm = min(M, 2048)
    nf = D_EXPERT // FFN_TF
    return pl.pallas_call(
        _ffn_kernel,
        grid=(E, M // tm, nf),
        in_specs=[pl.BlockSpec((1, tm, D_MODEL), lambda e, m, f: (e, m, 0)),
                  pl.BlockSpec((1, D_MODEL, FFN_TF), lambda e, m, f: (e, 0, f)),
                  pl.BlockSpec((1, D_MODEL, FFN_TF), lambda e, m, f: (e, 0, f)),
                  pl.BlockSpec((1, FFN_TF, D_MODEL), lambda e, m, f: (e, f, 0))],
        out_specs=pl.BlockSpec((1, tm, D_MODEL), lambda e, m, f: (e, m, 0)),
        out_shape=jax.ShapeDtypeStruct((E, M, D_MODEL), BF16),
        scratch_shapes=[pltpu.VMEM((tm, D_MODEL), F32)],
        compiler_params=_cparams(3),
        name="ffn",
    )(xg, w1, w3, w2)


def _combine_kernel(pos_ref, aff_ref, y_ref, h1_ref, g2_ref, fg_ref, o_ref, *, cap):
    pos = pos_ref[0]
    aff = aff_ref[0]
    slot = lax.broadcasted_iota(I32, (cap, TM), 0)
    moe = jnp.zeros((TM, D_MODEL), F32)
    for e in range(N_EXPERTS):
        w = jnp.where(pos[e:e + 1, :] == slot, aff[e:e + 1, :], 0.0).astype(BF16)
        moe = moe + _dot(w, y_ref[e], TN)
    h2 = h1_ref[0] + g2_ref[0] * moe
    o_ref[0] = _rmsnorm(h2, fg_ref[...])


def _combine_call(pos, aff_t, ys, h1, mods3, final_g, cap):
    B, E, T = pos.shape
    return pl.pallas_call(
        functools.partial(_combine_kernel, cap=cap),
        grid=(B, T // TM),
        in_specs=[pl.BlockSpec((1, E, TM), lambda b, i: (b, 0, i)),
                  pl.BlockSpec((1, E, TM), lambda b, i: (b, 0, i)),
                  pl.BlockSpec((E, cap, D_MODEL), lambda b, i: (0, b, 0)),
                  pl.BlockSpec((1, TM, D_MODEL), lambda b, i: (b, i, 0)),
                  pl.BlockSpec((1, 1, D_MODEL), lambda b, i: (b, 0, 5)),
                  pl.BlockSpec((1, D_MODEL), lambda b, i: (0, 0))],
        out_specs=pl.BlockSpec((1, TM, D_MODEL), lambda b, i: (b, i, 0)),
        out_shape=jax.ShapeDtypeStruct((B, T, D_MODEL), F32),
        compiler_params=_cparams(2),
        name="combine",
    )(pos, aff_t, ys, h1, mods3, final_g)


def _pad_cols(w, n):
    return jnp.pad(w, ((0, 0), (0, n - w.shape[1])))


def _both_dirs(up):
    z = jnp.zeros_like(up[0])
    return jnp.concatenate([jnp.concatenate([up[0], z], axis=1),
                            jnp.concatenate([z, up[1]], axis=1)], axis=0)


def kernel(x, c, ctx, c_ctx, mod_w, mod_b, norm1_g, w_in, ml_conv_w, ml_conv_b, ml_gate_b, ml_norm_g,
           rw_mu, rw_w0, rw_w_up, rw_a0, rw_a_up, rw_g_up, rw_k_k, rw_k_a, rw_r_k, rw_ln_g, rw_ln_b,
           w_out, norm2_g, router_w, exp_w1, exp_w3, exp_w2, final_g):
    B, T, D = x.shape
    t_ctx = ctx.shape[1]
    assert D == D_MODEL and T % TM == 0 and t_ctx % TM == 0 and TM % (T // GRID_W) == 0
    assert mod_w.shape[0] == 1 and B < 16
    nct, nlt = t_ctx // TM, T // TM
    cap = EC_FACTOR * T // N_EXPERTS
    ml_cols = 4 * ML_W + ML_GATES

    cc = jnp.concatenate([c, c_ctx[None, :], jnp.zeros((16 - B - 1, D), F32)], axis=0)
    mods = _mod_call(cc, mod_w[0], mod_b)
    mods3 = mods.reshape(16, 1, 6 * D)
    g1n = norm1_g.reshape(1, D)

    w_ml = _pad_cols(w_in[0, :, :ml_cols], ML_N).astype(BF16)
    w_rw = _pad_cols(w_in[0, :, ml_cols:], RW_N).astype(BF16)
    u_ml = _proj_call("ml", x, ctx, mods3, g1n, w_ml)
    u_rw = _proj_call("rw", x, ctx, mods3, g1n, w_rw)

    h_ml = _mlstm_call(u_ml, ml_conv_w[0], ml_conv_b, _pad_cols(ml_gate_b, LANES), B, nct, nlt)

    gup = jnp.pad(rw_g_up[0], ((0, 256 - RW_LORA_G), (0, 0)))
    scan_in, gb = _rw_pre_call(
        u_rw, _pad_cols(rw_mu[0], RW_N), rw_w0[0].reshape(1, 2 * RW_W), _both_dirs(rw_w_up[0]),
        rw_a0[0].reshape(1, 2 * RW_W), _both_dirs(rw_a_up[0]), gup,
        rw_k_k, rw_k_a, rw_r_k[0].reshape(1, RW_W), B, nct, nlt)
    y_f, y_b = _rw_scan_call(scan_in, B, t_ctx, T)

    h1, hn, aff_t = _mix_out_call(
        x, h_ml, u_ml, y_f, y_b, gb, mods3, ml_norm_g, rw_ln_g, rw_ln_b, norm2_g,
        w_out[0].astype(BF16), _pad_cols(router_w[0], LANES), nct)

    pos = _route_call(aff_t, cap)
    xg = _gather_call(pos, hn, cap)
    ys = _ffn_call(xg, exp_w1[0], exp_w3[0], exp_w2[0])
    return _combine_call(pos, aff_t, ys, h1, mods3, final_g.reshape(1, D), cap)
```

```python
import functools

import jax
import jax.numpy as jnp
from jax import lax
from jax.experimental import pallas as pl
from jax.experimental.pallas import tpu as pltpu

F32 = jnp.float32
BF16 = jnp.bfloat16
I32 = jnp.int32

D_MODEL = 1024
GRID_W = 64
EPS = 1e-6
ML_W = 512
ML_HEADS = 4
ML_DH = 128
ML_GATES = 16
ML_N = 4 * ML_W + 128
RW_W = 512
RW_HEADS = 8
RW_DH = 64
RW_LORA = 64
RW_LORA_G = 160
RW_LN_EPS = 64e-5
RW_COLS = 3 * RW_W + 2 * RW_LORA + 2 * RW_LORA + RW_LORA_G
RW_N = 2048
RW_PAIRS = RW_HEADS // 2
N_EXPERTS = 16
EC_FACTOR = 2
D_EXPERT = 2816

LANES = 128
SUBLANES = 8
TM = 256
RW_CHUNK = 64
RW_PAIRS_PER_STEP = 4
RW_BATCH_PER_STEP = 2
RW_INV_BASE = 4
FFN_TF = 256
BF16_ROWS = 16
GATHER_WIN = 64
COMBINE_WIN = 80
VMEM_LIMIT = 56 * 1024 * 1024

NN = (((1,), (0,)), ((), ()))
NT = (((1,), (1,)), ((), ()))
TN = (((0,), (0,)), ((), ()))


def _cparams(n_axes):
    return pltpu.CompilerParams(
        dimension_semantics=("arbitrary",) * n_axes, vmem_limit_bytes=VMEM_LIMIT)


def _dot(a, b, dn=NN):
    return lax.dot_general(a, b, dn, preferred_element_type=F32)


def _split2(a):
    hi = a.astype(BF16)
    lo = (a - hi.astype(F32)).astype(BF16)
    return hi, lo


def _split3(a):
    hi = a.astype(BF16)
    r1 = a - hi.astype(F32)
    mid = r1.astype(BF16)
    lo = (r1 - mid.astype(F32)).astype(BF16)
    return hi, mid, lo


def _dot1(a, b, dn=NN):
    return _dot(a.astype(BF16), b.astype(BF16), dn)


def _dot3(a, b, dn=NN):
    ah, al = _split2(a)
    bh, bl = _split2(b)
    return _dot(ah, bh, dn) + (_dot(ah, bl, dn) + _dot(al, bh, dn))


def _dot_sel(sel_bf16, b, dn=NN):
    bh, bm, bl = _split3(b)
    return _dot(sel_bf16, bh, dn) + (_dot(sel_bf16, bm, dn) + _dot(sel_bf16, bl, dn))


def _dot_rsel(a, sel_bf16, dn=NN):
    ah, am, al = _split3(a)
    return _dot(ah, sel_bf16, dn) + (_dot(am, sel_bf16, dn) + _dot(al, sel_bf16, dn))


def _silu(x):
    return x * jax.nn.sigmoid(x)


def _log_sigmoid(x):
    return jnp.minimum(x, 0.0) - jnp.log1p(jnp.exp(-jnp.abs(x)))


def _rmsnorm(x, g):
    return x * lax.rsqrt(jnp.mean(x * x, axis=-1, keepdims=True) + EPS) * g


def _tri(n, lower, strict, dtype):
    r = lax.broadcasted_iota(I32, (n, n), 0)
    c = lax.broadcasted_iota(I32, (n, n), 1)
    if lower:
        m = (c < r) if strict else (c <= r)
    else:
        m = (c > r) if strict else (c >= r)
    return m if dtype is None else m.astype(dtype)


def _shift_rows(z, prev_row, next_row):
    n = z.shape[0]
    row = lax.broadcasted_iota(I32, z.shape, 0)
    zm1 = jnp.where(row == 0, prev_row, pltpu.roll(z, 1, 0))
    zp1 = jnp.where(row == n - 1, next_row, pltpu.roll(z, n - 1, 0))
    return zm1, zp1


def _mod_kernel(c_ref, w_ref, b_ref, o_ref):
    o_ref[...] = _dot3(_silu(c_ref[...]), w_ref[...]) + b_ref[...]


def _mod_call(cc, mod_w, mod_b):
    n = mod_w.shape[1]
    tn = 1024
    return pl.pallas_call(
        _mod_kernel,
        grid=(n // tn,),
        in_specs=[pl.BlockSpec((16, D_MODEL), lambda j: (0, 0)),
                  pl.BlockSpec((D_MODEL, tn), lambda j: (0, j)),
                  pl.BlockSpec((1, tn), lambda j: (0, j))],
        out_specs=pl.BlockSpec((16, tn), lambda j: (0, j)),
        out_shape=jax.ShapeDtypeStruct((16, n), F32),
        compiler_params=_cparams(1),
        name="mod",
    )(cc, mod_w, mod_b)


def _proj_ml_kernel(ctx_ref, x_ref, sh_ref, sc_ref, g_ref, w_ref, o_ref, *, nct):
    i = pl.program_id(1)
    xin = jnp.where(i < nct, ctx_ref[0], x_ref[0])
    xn = _rmsnorm(xin, g_ref[...]) * (1.0 + sc_ref[0]) + sh_ref[0]
    o_ref[0] = _dot(xn.astype(BF16), w_ref[...])


def _proj_rw_kernel(ctx_ref, x_ref, sh_ref, sc_ref, g_ref, w_ref, o_ref, *, nct, rows):
    i = pl.program_id(1)
    xr = x_ref[0]
    ncb = TM // rows
    xcm = jnp.concatenate([xr[:, c * D_MODEL:(c + 1) * D_MODEL] for c in range(ncb)], axis=0)
    xin = jnp.where(i < nct, ctx_ref[0], xcm)
    xn = _rmsnorm(xin, g_ref[...]) * (1.0 + sc_ref[0]) + sh_ref[0]
    o_ref[0] = _dot(xn.astype(BF16), w_ref[...])


def _proj_call(kind, x, ctx, mods3, norm_g, w):
    B, T, _ = x.shape
    nct = ctx.shape[1] // TM
    nlt = T // TM
    n = w.shape[1]
    rows = T // GRID_W
    mrow = lambda b, i: jnp.where(i < nct, B, b)
    ctx_spec = pl.BlockSpec((1, TM, D_MODEL), lambda b, i: (b, jnp.minimum(i, nct - 1), 0))
    if kind == "ml":
        body = functools.partial(_proj_ml_kernel, nct=nct)
        x_in = x
        x_spec = pl.BlockSpec((1, TM, D_MODEL), lambda b, i: (b, jnp.maximum(i - nct, 0), 0))
    else:
        body = functools.partial(_proj_rw_kernel, nct=nct, rows=rows)
        x_in = x.reshape(B, rows, GRID_W * D_MODEL)
        x_spec = pl.BlockSpec((1, rows, (TM // rows) * D_MODEL),
                              lambda b, i: (b, 0, jnp.maximum(i - nct, 0)))
    return pl.pallas_call(
        body,
        grid=(B, nct + nlt),
        in_specs=[ctx_spec, x_spec,
                  pl.BlockSpec((1, 1, D_MODEL), lambda b, i: (mrow(b, i), 0, 0)),
                  pl.BlockSpec((1, 1, D_MODEL), lambda b, i: (mrow(b, i), 0, 1)),
                  pl.BlockSpec((1, D_MODEL), lambda b, i: (0, 0)),
                  pl.BlockSpec((D_MODEL, n), lambda b, i: (0, 0))],
        out_specs=pl.BlockSpec((1, TM, n), lambda b, i: (b, i, 0)),
        out_shape=jax.ShapeDtypeStruct((B, (nct + nlt) * TM, n), F32),
        compiler_params=_cparams(2),
        name="proj_" + kind,
    )(ctx, x_in, mods3, mods3, norm_g, w)


def _seq_chunk(d, j, nct, ntot):
    bwd = jnp.where(j < nct, nct - 1 - j, ntot - 1 - (j - nct))
    return jnp.where(d == 0, j, bwd)


def _mlstm_kernel(u_ref, hp_ref, hn_ref, cw_ref, cb_ref, gb_ref, o_ref, c_ref, n_ref, m_ref,
                  *, nct, ntot):
    d = pl.program_id(1)
    j = pl.program_id(2)
    c = _seq_chunk(d, j, nct, ntot)
    L = TM

    @pl.when(j == 0)
    def _():
        c_ref[...] = jnp.zeros_like(c_ref)
        n_ref[...] = jnp.zeros_like(n_ref)
        m_ref[...] = jnp.zeros_like(m_ref)

    seg_first = (c == 0) | (c == nct)
    seg_last = (c == nct - 1) | (c == ntot - 1)
    zqk = u_ref[0, :, 0:2 * ML_W]
    prev_row = jnp.where(seg_first, 0.0, hp_ref[0, SUBLANES - 1:SUBLANES, :])
    next_row = jnp.where(seg_last, 0.0, hn_ref[0, 0:1, :])
    zm1, zp1 = _shift_rows(zqk, prev_row, next_row)
    cw = cw_ref[...]
    qk = _silu(zm1 * cw[0:1] + zqk * cw[1:2] + zp1 * cw[2:3] + cb_ref[...])
    zv = u_ref[0, :, 2 * ML_W:3 * ML_W]

    g = u_ref[0, :, 4 * ML_W:4 * ML_W + LANES] + gb_ref[...]
    lane = lax.broadcasted_iota(I32, (L, LANES), 1)
    is_f = ((lane >= 4) & (lane < 8)) | ((lane >= 12) & (lane < 16))
    q_all = jnp.where(is_f, _log_sigmoid(g), g)
    causal = jnp.where(d == 0, _tri(L, True, False, F32), _tri(L, False, False, F32))
    b_all = _dot_sel(causal.astype(BF16), q_all)
    q_t = q_all.T
    b_t = b_all.T
    keep = causal > 0.5

    def lane_pick(a, idx):
        return jnp.sum(jnp.where(lane == idx, a, 0.0), axis=-1, keepdims=True)

    def row_pick(a_t, h):
        return jnp.where(d == 0, a_t[h:h + 1, :], a_t[8 + h:9 + h, :])

    hs = range(ML_HEADS)
    sl = [slice(h * ML_DH, (h + 1) * ML_DH) for h in hs]
    qf = [qk[:, sl[h]] for h in hs]
    q = [qf[h].astype(BF16) for h in hs]
    k = [qk[:, ML_W + h * ML_DH:ML_W + (h + 1) * ML_DH] * (ML_DH ** -0.5) for h in hs]
    kb = [k[h].astype(BF16) for h in hs]
    v = [zv[:, sl[h]] for h in hs]
    li_col = [lane_pick(q_all, h + 8 * d) for h in hs]
    b_col = [lane_pick(b_all, 4 + h + 8 * d) for h in hs]
    li_row = [row_pick(q_t, h) for h in hs]
    b_row = [row_pick(b_t, 4 + h) for h in hs]
    m_prev = [m_ref[h, 0:1, 0:1] for h in hs]
    C = [c_ref[h] for h in hs]
    n_row = [n_ref[h, 0:1, :] for h in hs]

    qk_s = [_dot(q[h], kb[h], NT) for h in hs]
    q_c = [_dot(q[h], C[h].astype(BF16), NT) for h in hs]
    dm = [jnp.where(keep, b_col[h] - b_row[h] + li_row[h], -jnp.inf) for h in hs]
    inter = [b_col[h] + m_prev[h] for h in hs]
    m_t = [jnp.maximum(inter[h], jnp.max(dm[h], axis=-1, keepdims=True)) for h in hs]
    wts = [jnp.exp(dm[h] - m_t[h]) * qk_s[h] for h in hs]
    w_inter = [jnp.exp(inter[h] - m_t[h]) for h in hs]
    num = [_dot(wts[h].astype(BF16), v[h].astype(BF16)) + w_inter[h] * q_c[h] for h in hs]
    for h in hs:
        den = (jnp.sum(wts[h], axis=-1, keepdims=True)
               + w_inter[h] * jnp.sum(qf[h] * n_row[h], axis=-1, keepdims=True))
        o_ref[0, 0, :, sl[h]] = num[h] / jnp.maximum(jnp.abs(den), jnp.exp(-m_t[h]))

    b_end = [jnp.where(d == 0, b_col[h][L - 1:L, :], b_col[h][0:1, :]) for h in hs]
    dec = [b_end[h] - b_col[h] + li_col[h] for h in hs]
    m_new = [jnp.maximum(b_end[h] + m_prev[h], jnp.max(dec[h], axis=0, keepdims=True)) for h in hs]
    wk = [jnp.exp(dec[h] - m_new[h]) for h in hs]
    s_old = [jnp.exp(b_end[h] + m_prev[h] - m_new[h]) for h in hs]
    c_upd = [_dot((v[h] * wk[h]).astype(BF16), kb[h], TN) for h in hs]
    for h in hs:
        c_ref[h] = s_old[h] * C[h] + c_upd[h]
        n_ref[h] = jnp.broadcast_to(
            s_old[h] * n_row[h] + jnp.sum(wk[h] * k[h], axis=0, keepdims=True), (SUBLANES, ML_DH))
        m_ref[h] = jnp.broadcast_to(m_new[h], (SUBLANES, LANES))


def _mlstm_call(u_ml, conv_w, conv_b, gate_b, B, nct, nlt):
    ntot = nct + nlt
    hb = TM // SUBLANES
    nrb = ntot * hb
    chunk = lambda d, j: _seq_chunk(d, j, nct, ntot)
    out_blk = lambda d, j: jnp.where(j < nct, d * (nlt - 1), chunk(d, j) - nct)
    return pl.pallas_call(
        functools.partial(_mlstm_kernel, nct=nct, ntot=ntot),
        grid=(B, 2, ntot),
        in_specs=[
            pl.BlockSpec((1, TM, ML_N), lambda b, d, j: (b, chunk(d, j), 0)),
            pl.BlockSpec((1, SUBLANES, 2 * ML_W),
                         lambda b, d, j: (b, jnp.maximum(chunk(d, j) * hb - 1, 0), 0)),
            pl.BlockSpec((1, SUBLANES, 2 * ML_W),
                         lambda b, d, j: (b, jnp.minimum((chunk(d, j) + 1) * hb, nrb - 1), 0)),
            pl.BlockSpec((3, 2 * ML_W), lambda b, d, j: (0, 0)),
            pl.BlockSpec((1, 2 * ML_W), lambda b, d, j: (0, 0)),
            pl.BlockSpec((1, LANES), lambda b, d, j: (0, 0)),
        ],
        out_specs=pl.BlockSpec((1, 1, TM, ML_W), lambda b, d, j: (d, b, out_blk(d, j), 0)),
        out_shape=jax.ShapeDtypeStruct((2, B, nlt * TM, ML_W), F32),
        scratch_shapes=[pltpu.VMEM((ML_HEADS, ML_DH, ML_DH), F32),
                        pltpu.VMEM((ML_HEADS, SUBLANES, ML_DH), F32),
                        pltpu.VMEM((ML_HEADS, SUBLANES, LANES), F32)],
        compiler_params=_cparams(3),
        name="mlstm",
    )(u_ml, u_ml, u_ml, conv_w, conv_b, gate_b)


def _rw_pre_kernel(u_ref, hp_ref, hn_ref, mu_ref, w0_ref, wup_ref, a0_ref, aup_ref, gup_ref,
                   kk_ref, ka_ref, rk_ref, scan_ref, gb_ref, *, nct, ntot):
    c = pl.program_id(1)
    seg_first = (c == 0) | (c == nct)
    seg_last = (c == nct - 1) | (c == ntot - 1)
    z = u_ref[0]
    prev_row = jnp.where(seg_first, 0.0, hp_ref[0, SUBLANES - 1:SUBLANES, :])
    next_row = jnp.where(seg_last, 0.0, hn_ref[0, 0:1, :])
    zm1, zp1 = _shift_rows(z, prev_row, next_row)
    mu = mu_ref[...]
    xs = z + mu[0:1] * (zm1 - z) + mu[1:2] * (zp1 - z)

    r = xs[:, 0:RW_W]
    k = xs[:, RW_W:2 * RW_W]
    v = xs[:, 2 * RW_W:3 * RW_W]
    wd = xs[:, 3 * RW_W:3 * RW_W + 2 * RW_LORA]
    ad = xs[:, 3 * RW_W + 2 * RW_LORA:3 * RW_W + 4 * RW_LORA]
    gd = xs[:, 3 * RW_W + 4 * RW_LORA:RW_N]

    w_pre = w0_ref[...] + _dot1(jnp.tanh(wd), wup_ref[...])
    log_w = -jax.nn.softplus(-w_pre) - 0.5
    lw = -jnp.exp(log_w)
    a = jax.nn.sigmoid(a0_ref[...] + _dot1(ad, aup_ref[...]))
    g = _dot1(jax.nn.sigmoid(gd), gup_ref[...])

    hr = lax.broadcasted_iota(I32, (RW_W, RW_W), 0) // RW_DH
    hc = lax.broadcasted_iota(I32, (RW_W, RW_W), 1) // RW_DH
    head_ones = (hr == hc).astype(BF16)
    kk = k * kk_ref[...]
    ss = _dot_rsel(kk * kk, head_ones)
    kk = kk / jnp.maximum(jnp.sqrt(ss), 1e-12)
    ka = ka_ref[...]
    k_f = k * (1.0 + (a[:, 0:RW_W] - 1.0) * ka)
    k_b = k * (1.0 + (a[:, RW_W:2 * RW_W] - 1.0) * ka)
    rk = rk_ref[...]
    bonus = (_dot_rsel(r * k_f * rk, head_ones) + _dot_rsel(r * k_b * rk, head_ones)) * v

    cols = (r, v, kk, lw[:, 0:RW_W], k_f, kk * a[:, 0:RW_W],
            lw[:, RW_W:2 * RW_W], k_b, kk * a[:, RW_W:2 * RW_W])
    for p in range(RW_PAIRS):
        for qi, arr in enumerate(cols):
            scan_ref[0, p, :, qi * LANES:(qi + 1) * LANES] = arr[:, p * LANES:(p + 1) * LANES]
    gb_ref[0, :, 0:RW_W] = g
    gb_ref[0, :, RW_W:2 * RW_W] = bonus


def _rw_pre_call(u_rw, mu, w0, wup, a0, aup, gup, k_k, k_a, r_k, B, nct, nlt):
    ntot = nct + nlt
    hb = TM // SUBLANES
    nrb = ntot * hb
    full = lambda shape: pl.BlockSpec(shape, lambda b, c: (0,) * len(shape))
    return pl.pallas_call(
        functools.partial(_rw_pre_kernel, nct=nct, ntot=ntot),
        grid=(B, ntot),
        in_specs=[
            pl.BlockSpec((1, TM, RW_N), lambda b, c: (b, c, 0)),
            pl.BlockSpec((1, SUBLANES, RW_N), lambda b, c: (b, jnp.maximum(c * hb - 1, 0), 0)),
            pl.BlockSpec((1, SUBLANES, RW_N),
                         lambda b, c: (b, jnp.minimum((c + 1) * hb, nrb - 1), 0)),
            full((2, RW_N)), full((1, 2 * RW_W)), full((2 * RW_LORA, 2 * RW_W)),
            full((1, 2 * RW_W)), full((2 * RW_LORA, 2 * RW_W)), full((256, RW_W)),
            full((1, RW_W)), full((1, RW_W)), full((1, RW_W)),
        ],
        out_specs=[
            pl.BlockSpec((1, RW_PAIRS, TM, 9 * LANES), lambda b, c: (b, 0, c, 0)),
            pl.BlockSpec((1, TM, 2 * RW_W), lambda b, c: (b, jnp.maximum(c - nct, 0), 0)),
        ],
        out_shape=[jax.ShapeDtypeStruct((B, RW_PAIRS, ntot * TM, 9 * LANES), F32),
                   jax.ShapeDtypeStruct((B, nlt * TM, 2 * RW_W), F32)],
        compiler_params=_cparams(2),
        name="rw_pre",
    )(u_rw, u_rw, u_rw, mu, w0, wup, a0, aup, gup, k_k, k_a, r_k)


def _rw_chunks(blks, states, lowers):
    n = len(blks)
    L = blks[0].shape[0]
    idx = range(n)
    hidx = [(i, h2) for i in idx for h2 in range(2)]
    col = lambda i, c: blks[i][:, c * LANES:(c + 1) * LANES]
    r, v, kk, lw, kd, bh = ([col(i, c) for i in idx] for c in range(6))
    tri = {lo: (_tri(L, lo, False, None), _tri(L, lo, True, None)) for lo in set(lowers)}
    incl = [tri[lo][0] for lo in lowers]
    strict = [tri[lo][1] for lo in lowers]
    logp = [_dot_sel(incl[i].astype(BF16), lw[i]) for i in idx]
    logp_end = [logp[i][L - 1:L, :] if lowers[i] else logp[i][0:1, :] for i in idx]
    p_inv = [jnp.exp(-logp[i]) for i in idx]
    a_t = [-kk[i] * jnp.exp(logp[i] - lw[i]) for i in idx]
    r_t = [r[i] * jnp.exp(logp[i]) for i in idx]
    bk = [jnp.concatenate([bh[i] * p_inv[i], kd[i] * p_inv[i]], axis=0) for i in idx]
    vv = [jnp.concatenate([v[i], v[i]], axis=0) for i in idx]
    lane = lax.broadcasted_iota(I32, (L, LANES), 1)
    in_head = (lane < RW_DH, lane >= RW_DH)
    eye = jnp.where(_tri(L, True, False, None) & _tri(L, False, False, None), 1.0, 0.0)
    row2 = lax.broadcasted_iota(I32, (L, 2 * L), 0)
    col2 = lax.broadcasted_iota(I32, (L, 2 * L), 1)
    src2 = jnp.where(col2 >= L, col2 - L, col2)
    ak_mask = {lo: ((src2 < row2) if lo else (src2 > row2)) & (col2 >= L) for lo in set(lowers)}
    r_mask = {lo: (src2 <= row2) if lo else (src2 >= row2) for lo in set(lowers)}

    a_s = [_dot1(a_t[i], states[i], NT) for i in idx]
    r_s = [_dot1(r_t[i], states[i], NT) for i in idx]
    g_a = [_dot1(jnp.where(in_head[h], a_t[i], 0.0), bk[i], NT) for i, h in hidx]
    g_r = [_dot1(jnp.where(in_head[h], r_t[i], 0.0), bk[i], NT) for i, h in hidx]
    n_mat = [jnp.where(strict[i], g_a[q][:, 0:L], 0.0) for q, (i, h) in enumerate(hidx)]
    w_rhs = [a_s[i] + _dot1(jnp.where(ak_mask[lowers[i]], g_a[q], 0.0), vv[i])
             for q, (i, h) in enumerate(hidx)]
    hq = range(len(hidx))
    rowb = lax.broadcasted_iota(I32, (L, L), 0)
    colb = lax.broadcasted_iota(I32, (L, L), 1)
    base = RW_INV_BASE
    n_d = [jnp.where(rowb // base == colb // base, n_mat[q], 0.0) for q in hq]
    inv = [eye + n_d[q] for q in hq]
    n_d2 = [_dot1(n_d[q], n_d[q]) for q in hq]
    inv = [inv[q] + _dot1(inv[q], n_d2[q]) for q in hq]
    blk = base
    while blk < L:
        link = (rowb // (2 * blk) == colb // (2 * blk)) & (rowb // blk != colb // blk)
        t_m = [_dot1(inv[q], jnp.where(link, n_mat[q], 0.0)) for q in hq]
        inv = [inv[q] + _dot1(t_m[q], inv[q]) for q in hq]
        blk *= 2
    u_h = [_dot1(inv[q], w_rhs[q]) for q in hq]
    u = [jnp.where(in_head[0], u_h[2 * i], u_h[2 * i + 1]) for i in idx]
    uv = [jnp.concatenate([u[i], v[i]], axis=0) for i in idx]
    y_h = [r_s[i] + _dot1(jnp.where(r_mask[lowers[i]], g_r[q], 0.0), uv[i])
           for q, (i, h) in enumerate(hidx)]
    y = [jnp.where(in_head[0], y_h[2 * i], y_h[2 * i + 1]) for i in idx]

    rr = lax.broadcasted_iota(I32, (LANES, LANES), 0) // RW_DH
    cc = lax.broadcasted_iota(I32, (LANES, LANES), 1) // RW_DH
    s_new = []
    for i in idx:
        to_end = jnp.exp(logp_end[i] - logp[i])
        bk_end = jnp.concatenate([bh[i] * to_end, kd[i] * to_end], axis=0)
        s_new.append(jnp.where(
            rr == cc, states[i] * jnp.exp(logp_end[i]) + _dot1(uv[i], bk_end, TN), 0.0))
    return y, s_new


def _rw_scan_kernel(f_ref, b_ref, yf_ref, yb_ref, s_ref):
    j = pl.program_id(2)

    @pl.when(j == 0)
    def _():
        s_ref[...] = jnp.zeros_like(s_ref)

    probs = [(bi, p) for bi in range(f_ref.shape[0]) for p in range(f_ref.shape[1])]
    blks, states, lowers = [], [], []
    for bi, p in probs:
        fb = f_ref[bi, p]
        bb = b_ref[bi, p]
        blks += [fb[:, 0:6 * LANES],
                 jnp.concatenate([bb[:, 0:3 * LANES], bb[:, 6 * LANES:9 * LANES]], axis=1)]
        states += [s_ref[bi, p, 0], s_ref[bi, p, 1]]
        lowers += [True, False]
    y, s_new = _rw_chunks(blks, states, lowers)
    for q, (bi, p) in enumerate(probs):
        yf_ref[bi, p] = y[2 * q]
        yb_ref[bi, p] = y[2 * q + 1]
        s_ref[bi, p, 0] = s_new[2 * q]
        s_ref[bi, p, 1] = s_new[2 * q + 1]


def _rw_scan_call(scan_in, B, t_ctx, t_lat):
    L = RW_CHUNK
    nct, nlt = t_ctx // L, t_lat // L
    ntot = nct + nlt
    fchunk = lambda j: j
    bchunk = lambda j: _seq_chunk(1, j, nct, ntot)
    pps = RW_PAIRS_PER_STEP
    bps = RW_BATCH_PER_STEP if B % RW_BATCH_PER_STEP == 0 else 1
    in_blk = (bps, pps, L, 9 * LANES)
    out_blk = (bps, pps, L, LANES)
    out_shape = jax.ShapeDtypeStruct((B, RW_PAIRS, t_lat, LANES), F32)
    return pl.pallas_call(
        _rw_scan_kernel,
        grid=(B // bps, RW_PAIRS // pps, ntot),
        in_specs=[pl.BlockSpec(in_blk, lambda b, p, j: (b, p, fchunk(j), 0)),
                  pl.BlockSpec(in_blk, lambda b, p, j: (b, p, bchunk(j), 0))],
        out_specs=[
            pl.BlockSpec(out_blk, lambda b, p, j: (b, p, jnp.maximum(fchunk(j) - nct, 0), 0)),
            pl.BlockSpec(out_blk,
                         lambda b, p, j: (b, p, jnp.where(j < nct, nlt - 1, bchunk(j) - nct), 0)),
        ],
        out_shape=[out_shape, out_shape],
        scratch_shapes=[pltpu.VMEM((bps, pps, 2, LANES, LANES), F32)],
        compiler_params=_cparams(3),
        name="rw_scan",
    )(scan_in, scan_in)


def _mix_out_kernel(x_ref, h_ref, zo_ref, yf_ref, yb_ref, gb_ref, g1_ref, sh2_ref, sc2_ref,
                    mlg_ref, lng_ref, lnb_ref, n2g_ref, wo_ref, rw_ref,
                    h1_ref, hn_ref, aff_ref):
    hm = h_ref[0, 0] + h_ref[1, 0]
    parts = []
    for h in range(ML_HEADS):
        hh = hm[:, h * ML_DH:(h + 1) * ML_DH]
        parts.append(hh * lax.rsqrt(jnp.mean(hh * hh, axis=-1, keepdims=True) + EPS))
    ml = jnp.concatenate(parts, axis=1) * mlg_ref[...] * jax.nn.sigmoid(zo_ref[0])

    nr = TM // GRID_W
    yf = yf_ref[0]
    yb = yb_ref[0]
    y = jnp.concatenate(
        [jnp.concatenate([yf[p, :, jr * LANES:(jr + 1) * LANES] + yb[p, :, jr * LANES:(jr + 1) * LANES]
                          for p in range(RW_PAIRS)], axis=1) for jr in range(nr)], axis=0)
    gbm = gb_ref[0]
    gbr = jnp.concatenate([gbm[:, jr * 2 * RW_W:(jr + 1) * 2 * RW_W] for jr in range(nr)], axis=0)
    hr = lax.broadcasted_iota(I32, (RW_W, RW_W), 0) // RW_DH
    hc = lax.broadcasted_iota(I32, (RW_W, RW_W), 1) // RW_DH
    head_ones = (hr == hc).astype(BF16)
    mean = _dot_rsel(y, head_ones) * (1.0 / RW_DH)
    dy = y - mean
    var = _dot_rsel(dy * dy, head_ones) * (1.0 / RW_DH)
    rw = dy * lax.rsqrt(var + RW_LN_EPS) * lng_ref[...] + lnb_ref[...]
    rw = (rw + gbr[:, RW_W:2 * RW_W]) * gbr[:, 0:RW_W]

    mix = _dot(jnp.concatenate([ml, rw], axis=1).astype(BF16), wo_ref[...])
    h1 = x_ref[0] + g1_ref[0] * mix
    h1_ref[0] = h1
    hn = _rmsnorm(h1, n2g_ref[...]) * (1.0 + sc2_ref[0]) + sh2_ref[0]
    hn_ref[0] = hn.astype(BF16)
    logits = _dot3(hn, rw_ref[...])
    lane = lax.broadcasted_iota(I32, logits.shape, 1)
    logits = jnp.where(lane < N_EXPERTS, logits, -jnp.inf)
    e = jnp.exp(logits - jnp.max(logits, axis=-1, keepdims=True))
    aff = e / jnp.sum(e, axis=-1, keepdims=True)
    aff_ref[0] = aff.T[0:N_EXPERTS, :]


def _mix_out_call(x, h_ml, u_ml, y_f, y_b, gb, mods3, ml_norm_g, ln_g, ln_b, norm2_g,
                  wo, router_pad, nct):
    B, T, _ = x.shape
    nlt = T // TM
    rows = T // GRID_W
    nr = TM // GRID_W
    yv = lambda y: y.reshape(B, RW_PAIRS, GRID_W, rows * LANES)
    gbv = gb.reshape(B, GRID_W, rows * 2 * RW_W)
    row1 = lambda shape: pl.BlockSpec(shape, lambda b, i: (0,) * len(shape))
    mod = lambda k: pl.BlockSpec((1, 1, D_MODEL), lambda b, i: (b, 0, k))
    return pl.pallas_call(
        _mix_out_kernel,
        grid=(B, nlt),
        in_specs=[
            pl.BlockSpec((1, TM, D_MODEL), lambda b, i: (b, i, 0)),
            pl.BlockSpec((2, 1, TM, ML_W), lambda b, i: (0, b, i, 0)),
            pl.BlockSpec((1, TM, ML_W), lambda b, i: (b, i + nct, 3)),
            pl.BlockSpec((1, RW_PAIRS, GRID_W, nr * LANES), lambda b, i: (b, 0, 0, i)),
            pl.BlockSpec((1, RW_PAIRS, GRID_W, nr * LANES), lambda b, i: (b, 0, 0, i)),
            pl.BlockSpec((1, GRID_W, nr * 2 * RW_W), lambda b, i: (b, 0, i)),
            mod(2), mod(3), mod(4),
            row1((1, ML_W)), row1((1, RW_W)), row1((1, RW_W)), row1((1, D_MODEL)),
            row1((D_MODEL, D_MODEL)), row1((D_MODEL, LANES)),
        ],
        out_specs=[
            pl.BlockSpec((1, TM, D_MODEL), lambda b, i: (b, i, 0)),
            pl.BlockSpec((1, TM, D_MODEL), lambda b, i: (b, i, 0)),
            pl.BlockSpec((1, N_EXPERTS, TM), lambda b, i: (b, 0, i)),
        ],
        out_shape=[jax.ShapeDtypeStruct((B, T, D_MODEL), F32),
                   jax.ShapeDtypeStruct((B, T, D_MODEL), BF16),
                   jax.ShapeDtypeStruct((B, N_EXPERTS, T), F32)],
        compiler_params=_cparams(2),
        name="mix_out",
    )(x, h_ml, u_ml, yv(y_f), yv(y_b), gbv, mods3, mods3, mods3,
      ml_norm_g, ln_g, ln_b, norm2_g, wo, router_pad)


def _route_kernel(aff_ref, pos_ref, st_ref, *, cap):
    a = aff_ref[0]
    T = a.shape[1]
    as_f32 = lambda bits: lax.bitcast_convert_type(bits, F32)

    def body(i, thr):
        cand = thr | jnp.left_shift(jnp.int32(1), 30 - i)
        cnt = jnp.sum(jnp.where(a >= as_f32(cand), 1.0, 0.0), axis=1, keepdims=True)
        return jnp.where(cnt >= cap, cand, thr)

    thr = lax.fori_loop(0, 31, body, jnp.zeros((N_EXPERTS, 1), I32))
    gt = a >= as_f32(thr + 1)
    eq = (a >= as_f32(thr)) & jnp.logical_not(gt)
    need = cap - jnp.sum(jnp.where(gt, 1.0, 0.0), axis=1, keepdims=True)
    tri = _tri(TM, False, False, BF16)

    def prefix_excl(mask):
        outs, carries = [], []
        carry = jnp.zeros((N_EXPERTS, 1), F32)
        for blk in range(T // TM):
            seg = jnp.where(mask[:, blk * TM:(blk + 1) * TM], 1.0, 0.0)
            inc = _dot(seg.astype(BF16), tri)
            outs.append(inc - seg + carry)
            carries.append(carry)
            carry = carry + jnp.sum(seg, axis=1, keepdims=True)
        return jnp.concatenate(outs, axis=1), carries + [carry]

    chosen = gt | (eq & (prefix_excl(eq)[0] < need))
    slot, block_starts = prefix_excl(chosen)
    pos_ref[0] = jnp.where(chosen, slot, -1.0).astype(I32)
    lane = lax.broadcasted_iota(I32, (N_EXPERTS, LANES), 1)
    st = jnp.zeros((N_EXPERTS, LANES), F32)
    for blk, start in enumerate(block_starts):
        st = jnp.where(lane == blk, start, st)
    st_ref[0] = st.astype(I32)


def _route_call(aff_t, cap):
    B, E, T = aff_t.shape
    assert T // TM < LANES
    return pl.pallas_call(
        functools.partial(_route_kernel, cap=cap),
        grid=(B,),
        in_specs=[pl.BlockSpec((1, E, T), lambda b: (b, 0, 0))],
        out_specs=[pl.BlockSpec((1, E, T), lambda b: (b, 0, 0)),
                   pl.BlockSpec((1, E, LANES), lambda b: (b, 0, 0))],
        out_shape=[jax.ShapeDtypeStruct((B, E, T), I32),
                   jax.ShapeDtypeStruct((B, E, LANES), I32)],
        compiler_params=_cparams(1),
        name="route",
    )(aff_t)


def _align_down(s, m):
    sh = m.bit_length() - 1
    return pl.multiple_of(lax.shift_left(lax.shift_right_logical(s, sh), sh), m)


def _gather_kernel(st_ref, pos_ref, hn_ref, o_ref, acc_ref, *, cap):
    b = pl.program_id(0)
    e = pl.program_id(1)
    nb = hn_ref.shape[1] // TM
    acc_ref[...] = jnp.zeros_like(acc_ref)
    win = lax.broadcasted_iota(I32, (GATHER_WIN, TM), 0)
    rows = []
    for k in range(nb):
        row0 = _align_down(st_ref[b, e, k], SUBLANES)
        onehot = jnp.where(pos_ref[0, :, k * TM:(k + 1) * TM] == row0 + win, 1.0, 0.0)
        acc_ref[pl.ds(row0, GATHER_WIN), :] += _dot(onehot.astype(BF16), hn_ref[0, k * TM:(k + 1) * TM, :])
        rows.append(row0)
    slot = lax.broadcasted_iota(I32, (cap, TM), 0)
    for k in range(nb):
        @pl.when(st_ref[b, e, k + 1] > rows[k] + GATHER_WIN)
        def _():
            rest = (pos_ref[0, :, k * TM:(k + 1) * TM] == slot) & (slot >= rows[k] + GATHER_WIN)
            acc_ref[0:cap, :] += _dot(jnp.where(rest, 1.0, 0.0).astype(BF16),
                                      hn_ref[0, k * TM:(k + 1) * TM, :])
    o_ref[0] = acc_ref[0:cap, :].astype(BF16)


def _gather_call(starts, pos, hn, cap):
    B, E, T = pos.shape
    grid_spec = pltpu.PrefetchScalarGridSpec(
        num_scalar_prefetch=1,
        grid=(B, E),
        in_specs=[pl.BlockSpec((1, 1, T), lambda b, e, st: (b * E + e, 0, 0)),
                  pl.BlockSpec((1, T, D_MODEL), lambda b, e, st: (b, 0, 0))],
        out_specs=pl.BlockSpec((1, cap, D_MODEL), lambda b, e, st: (e, b, 0)),
        scratch_shapes=[pltpu.VMEM((cap + GATHER_WIN, D_MODEL), F32)])
    return pl.pallas_call(
        functools.partial(_gather_kernel, cap=cap),
        grid_spec=grid_spec,
        out_shape=jax.ShapeDtypeStruct((E, B * cap, D_MODEL), BF16),
        compiler_params=_cparams(2),
        name="gather",
    )(starts, pos.reshape(B * E, 1, T), hn)


def _ffn_kernel(x_ref, w1_ref, w3_ref, w2_ref, o_ref, acc_ref):
    f = pl.program_id(2)

    @pl.when(f == 0)
    def _():
        acc_ref[...] = jnp.zeros_like(acc_ref)

    x = x_ref[0]
    h1 = _dot(x, w1_ref[0].astype(BF16))
    h3 = _dot(x, w3_ref[0].astype(BF16))
    hid = (_silu(h1) * h3).astype(BF16)
    acc_ref[...] += _dot(hid, w2_ref[0].astype(BF16))

    @pl.when(f == pl.num_programs(2) - 1)
    def _():
        o_ref[0] = acc_ref[...].astype(BF16)


def _ffn_call(xg, w1, w3, w2):
    E, M, _ = xg.shape
    tm = min(M, 2048)
    nf = D_EXPERT // FFN_TF
    return pl.pallas_call(
        _ffn_kernel,
        grid=(E, M // tm, nf),
        in_specs=[pl.BlockSpec((1, tm, D_MODEL), lambda e, m, f: (e, m, 0)),
                  pl.BlockSpec((1, D_MODEL, FFN_TF), lambda e, m, f: (e, 0, f)),
                  pl.BlockSpec((1, D_MODEL, FFN_TF), lambda e, m, f: (e, 0, f)),
                  pl.BlockSpec((1, FFN_TF, D_MODEL), lambda e, m, f: (e, f, 0))],
        out_specs=pl.BlockSpec((1, tm, D_MODEL), lambda e, m, f: (e, m, 0)),
        out_shape=jax.ShapeDtypeStruct((E, M, D_MODEL), BF16),
        scratch_shapes=[pltpu.VMEM((tm, D_MODEL), F32)],
        compiler_params=_cparams(3),
        name="ffn",
    )(xg, w1, w3, w2)


def _combine_kernel(st_ref, pos_ref, aff_ref, y_ref, h1_ref, g2_ref, fg_ref, o_ref, moe_ref, *, cap):
    b = pl.program_id(0)
    k = pl.program_id(1)
    pos = pos_ref[0]
    aff = aff_ref[0]
    win = lax.broadcasted_iota(I32, (COMBINE_WIN, TM), 0)
    ws, ys, rows = [], [], []
    for e in range(N_EXPERTS):
        row0 = jnp.minimum(_align_down(st_ref[b, e, k], BF16_ROWS), cap - COMBINE_WIN)
        row0 = pl.multiple_of(row0, BF16_ROWS)
        ws.append(jnp.where(pos[e:e + 1, :] == row0 + win, aff[e:e + 1, :], 0.0).astype(BF16))
        ys.append(y_ref[e, pl.ds(row0, COMBINE_WIN), :])
        rows.append(row0)
    moe_ref[...] = _dot(jnp.concatenate(ws, axis=0), jnp.concatenate(ys, axis=0), TN)
    slot = lax.broadcasted_iota(I32, (cap, TM), 0)
    for e in range(N_EXPERTS):
        @pl.when(st_ref[b, e, k + 1] > rows[e] + COMBINE_WIN)
        def _():
            rest = (pos[e:e + 1, :] == slot) & (slot >= rows[e] + COMBINE_WIN)
            moe_ref[...] += _dot(jnp.where(rest, aff[e:e + 1, :], 0.0).astype(BF16), y_ref[e], TN)
    h2 = h1_ref[0] + g2_ref[0] * moe_ref[...]
    o_ref[0] = _rmsnorm(h2, fg_ref[...])


def _combine_call(starts, pos, aff_t, ys, h1, mods3, final_g, cap):
    B, E, T = pos.shape
    assert cap >= COMBINE_WIN and (cap - COMBINE_WIN) % BF16_ROWS == 0
    grid_spec = pltpu.PrefetchScalarGridSpec(
        num_scalar_prefetch=1,
        grid=(B, T // TM),
        in_specs=[pl.BlockSpec((1, E, TM), lambda b, i, st: (b, 0, i)),
                  pl.BlockSpec((1, E, TM), lambda b, i, st: (b, 0, i)),
                  pl.BlockSpec((E, cap, D_MODEL), lambda b, i, st: (0, b, 0)),
                  pl.BlockSpec((1, TM, D_MODEL), lambda b, i, st: (b, i, 0)),
                  pl.BlockSpec((1, 1, D_MODEL), lambda b, i, st: (b, 0, 5)),
                  pl.BlockSpec((1, D_MODEL), lambda b, i, st: (0, 0))],
        out_specs=pl.BlockSpec((1, TM, D_MODEL), lambda b, i, st: (b, i, 0)),
        scratch_shapes=[pltpu.VMEM((TM, D_MODEL), F32)])
    return pl.pallas_call(
        functools.partial(_combine_kernel, cap=cap),
        grid_spec=grid_spec,
        out_shape=jax.ShapeDtypeStruct((B, T, D_MODEL), F32),
        compiler_params=_cparams(2),
        name="combine",
    )(starts, pos, aff_t, ys, h1, mods3, final_g)


def _pad_cols(w, n):
    return jnp.pad(w, ((0, 0), (0, n - w.shape[1])))


def _both_dirs(up):
    z = jnp.zeros_like(up[0])
    return jnp.concatenate([jnp.concatenate([up[0], z], axis=1),
                            jnp.concatenate([z, up[1]], axis=1)], axis=0)


def kernel(x, c, ctx, c_ctx, mod_w, mod_b, norm1_g, w_in, ml_conv_w, ml_conv_b, ml_gate_b, ml_norm_g,
           rw_mu, rw_w0, rw_w_up, rw_a0, rw_a_up, rw_g_up, rw_k_k, rw_k_a, rw_r_k, rw_ln_g, rw_ln_b,
           w_out, norm2_g, router_w, exp_w1, exp_w3, exp_w2, final_g):
    B, T, D = x.shape
    t_ctx = ctx.shape[1]
    assert D == D_MODEL and T % TM == 0 and t_ctx % TM == 0 and TM % (T // GRID_W) == 0
    assert mod_w.shape[0] == 1 and B < 16
    nct, nlt = t_ctx // TM, T // TM
    cap = EC_FACTOR * T // N_EXPERTS
    ml_cols = 4 * ML_W + ML_GATES

    cc = jnp.concatenate([c, c_ctx[None, :], jnp.zeros((16 - B - 1, D), F32)], axis=0)
    mods = _mod_call(cc, mod_w[0], mod_b)
    mods3 = mods.reshape(16, 1, 6 * D)
    g1n = norm1_g.reshape(1, D)

    w_ml = _pad_cols(w_in[0, :, :ml_cols], ML_N).astype(BF16)
    w_rw = _pad_cols(w_in[0, :, ml_cols:], RW_N).astype(BF16)
    u_ml = _proj_call("ml", x, ctx, mods3, g1n, w_ml)
    u_rw = _proj_call("rw", x, ctx, mods3, g1n, w_rw)

    h_ml = _mlstm_call(u_ml, ml_conv_w[0], ml_conv_b, _pad_cols(ml_gate_b, LANES), B, nct, nlt)

    gup = jnp.pad(rw_g_up[0], ((0, 256 - RW_LORA_G), (0, 0)))
    scan_in, gb = _rw_pre_call(
        u_rw, _pad_cols(rw_mu[0], RW_N), rw_w0[0].reshape(1, 2 * RW_W), _both_dirs(rw_w_up[0]),
        rw_a0[0].reshape(1, 2 * RW_W), _both_dirs(rw_a_up[0]), gup,
        rw_k_k, rw_k_a, rw_r_k[0].reshape(1, RW_W), B, nct, nlt)
    y_f, y_b = _rw_scan_call(scan_in, B, t_ctx, T)

    h1, hn, aff_t = _mix_out_call(
        x, h_ml, u_ml, y_f, y_b, gb, mods3, ml_norm_g, rw_ln_g, rw_ln_b, norm2_g,
        w_out[0].astype(BF16), _pad_cols(router_w[0], LANES), nct)

    pos, starts = _route_call(aff_t, cap)
    starts = starts[:, :, :nlt + 1]
    xg = _gather_call(starts, pos, hn, cap)
    ys = _ffn_call(xg, exp_w1[0], exp_w3[0], exp_w2[0])
    return _combine_call(starts, pos, aff_t, ys, h1, mods3, final_g.reshape(1, D), cap)
```

```python
import functools

import jax
import jax.numpy as jnp
from jax import lax
from jax.experimental import pallas as pl
from jax.experimental.pallas import tpu as pltpu

F32 = jnp.float32
BF16 = jnp.bfloat16
I32 = jnp.int32

D_MODEL = 1024
GRID_W = 64
EPS = 1e-6
ML_W = 512
ML_HEADS = 4
ML_DH = 128
ML_GATES = 16
ML_N = 4 * ML_W + 128
RW_W = 512
RW_HEADS = 8
RW_DH = 64
RW_LORA = 64
RW_LORA_G = 160
RW_LN_EPS = 64e-5
RW_COLS = 3 * RW_W + 2 * RW_LORA + 2 * RW_LORA + RW_LORA_G
RW_N = 2048
RW_PAIRS = RW_HEADS // 2
N_EXPERTS = 16
EC_FACTOR = 2
D_EXPERT = 2816

LANES = 128
SUBLANES = 8
TM = 256
RW_CHUNK = 64
RW_PAIRS_PER_STEP = 4
RW_BATCH_PER_STEP = 4
RW_INV_BASE = 4
FFN_TF = 256
BF16_ROWS = 16
GATHER_WIN = 64
COMBINE_WIN = 80
VMEM_LIMIT = 56 * 1024 * 1024

NN = (((1,), (0,)), ((), ()))
NT = (((1,), (1,)), ((), ()))
TN = (((0,), (0,)), ((), ()))


def _cparams(n_axes):
    return pltpu.CompilerParams(
        dimension_semantics=("arbitrary",) * n_axes, vmem_limit_bytes=VMEM_LIMIT)


def _dot(a, b, dn=NN):
    return lax.dot_general(a, b, dn, preferred_element_type=F32)


def _split2(a):
    hi = a.astype(BF16)
    lo = (a - hi.astype(F32)).astype(BF16)
    return hi, lo


def _split3(a):
    hi = a.astype(BF16)
    r1 = a - hi.astype(F32)
    mid = r1.astype(BF16)
    lo = (r1 - mid.astype(F32)).astype(BF16)
    return hi, mid, lo


def _dot1(a, b, dn=NN):
    return _dot(a.astype(BF16), b.astype(BF16), dn)


def _dot3(a, b, dn=NN):
    ah, al = _split2(a)
    bh, bl = _split2(b)
    return _dot(ah, bh, dn) + (_dot(ah, bl, dn) + _dot(al, bh, dn))


def _dot_sel(sel_bf16, b, dn=NN):
    bh, bm, bl = _split3(b)
    return _dot(sel_bf16, bh, dn) + (_dot(sel_bf16, bm, dn) + _dot(sel_bf16, bl, dn))


def _dot_rsel(a, sel_bf16, dn=NN):
    ah, am, al = _split3(a)
    return _dot(ah, sel_bf16, dn) + (_dot(am, sel_bf16, dn) + _dot(al, sel_bf16, dn))


def _silu(x):
    return x * jax.nn.sigmoid(x)


def _log_sigmoid(x):
    return jnp.minimum(x, 0.0) - jnp.log1p(jnp.exp(-jnp.abs(x)))


def _rmsnorm(x, g):
    return x * lax.rsqrt(jnp.mean(x * x, axis=-1, keepdims=True) + EPS) * g


def _tri(n, lower, strict, dtype):
    r = lax.broadcasted_iota(I32, (n, n), 0)
    c = lax.broadcasted_iota(I32, (n, n), 1)
    if lower:
        m = (c < r) if strict else (c <= r)
    else:
        m = (c > r) if strict else (c >= r)
    return m if dtype is None else m.astype(dtype)


def _shift_rows(z, prev_row, next_row):
    n = z.shape[0]
    row = lax.broadcasted_iota(I32, z.shape, 0)
    zm1 = jnp.where(row == 0, prev_row, pltpu.roll(z, 1, 0))
    zp1 = jnp.where(row == n - 1, next_row, pltpu.roll(z, n - 1, 0))
    return zm1, zp1


def _mod_kernel(c_ref, w_ref, b_ref, o_ref):
    o_ref[...] = _dot3(_silu(c_ref[...]), w_ref[...]) + b_ref[...]


def _mod_call(cc, mod_w, mod_b):
    n = mod_w.shape[1]
    tn = 1024
    return pl.pallas_call(
        _mod_kernel,
        grid=(n // tn,),
        in_specs=[pl.BlockSpec((16, D_MODEL), lambda j: (0, 0)),
                  pl.BlockSpec((D_MODEL, tn), lambda j: (0, j)),
                  pl.BlockSpec((1, tn), lambda j: (0, j))],
        out_specs=pl.BlockSpec((16, tn), lambda j: (0, j)),
        out_shape=jax.ShapeDtypeStruct((16, n), F32),
        compiler_params=_cparams(1),
        name="mod",
    )(cc, mod_w, mod_b)


def _proj_ml_kernel(ctx_ref, x_ref, sh_ref, sc_ref, g_ref, w_ref, o_ref, *, nct):
    i = pl.program_id(1)
    xin = jnp.where(i < nct, ctx_ref[0], x_ref[0])
    xn = _rmsnorm(xin, g_ref[...]) * (1.0 + sc_ref[0]) + sh_ref[0]
    o_ref[0] = _dot(xn.astype(BF16), w_ref[...])


def _proj_rw_kernel(ctx_ref, x_ref, sh_ref, sc_ref, g_ref, w_ref, o_ref, *, nct, rows):
    i = pl.program_id(1)
    cpt = TM // rows
    tpb = x_ref.shape[2] // cpt
    c0 = (jnp.maximum(i - nct, 0) % tpb) * cpt if tpb > 1 else 0
    xcm = jnp.concatenate([x_ref[0, :, c0 + c, :] for c in range(cpt)], axis=0)
    xin = jnp.where(i < nct, ctx_ref[0], xcm)
    xn = _rmsnorm(xin, g_ref[...]) * (1.0 + sc_ref[0]) + sh_ref[0]
    o_ref[0] = _dot(xn.astype(BF16), w_ref[...])


def _proj_call(kind, x, ctx, mods3, norm_g, w):
    B, T, _ = x.shape
    nct = ctx.shape[1] // TM
    nlt = T // TM
    n = w.shape[1]
    rows = T // GRID_W
    mrow = lambda b, i: jnp.where(i < nct, B, b)
    ctx_spec = pl.BlockSpec((1, TM, D_MODEL), lambda b, i: (b, jnp.minimum(i, nct - 1), 0))
    if kind == "ml":
        body = functools.partial(_proj_ml_kernel, nct=nct)
        x_in = x
        x_spec = pl.BlockSpec((1, TM, D_MODEL), lambda b, i: (b, jnp.maximum(i - nct, 0), 0))
    else:
        body = functools.partial(_proj_rw_kernel, nct=nct, rows=rows)
        x_in = x.reshape(B, rows, GRID_W, D_MODEL)
        cb = max(SUBLANES, TM // rows)
        tpb = cb * rows // TM
        x_spec = pl.BlockSpec((1, rows, cb, D_MODEL),
                              lambda b, i: (b, 0, jnp.maximum(i - nct, 0) // tpb, 0))
    return pl.pallas_call(
        body,
        grid=(B, nct + nlt),
        in_specs=[ctx_spec, x_spec,
                  pl.BlockSpec((1, 1, D_MODEL), lambda b, i: (mrow(b, i), 0, 0)),
                  pl.BlockSpec((1, 1, D_MODEL), lambda b, i: (mrow(b, i), 0, 1)),
                  pl.BlockSpec((1, D_MODEL), lambda b, i: (0, 0)),
                  pl.BlockSpec((D_MODEL, n), lambda b, i: (0, 0))],
        out_specs=pl.BlockSpec((1, TM, n), lambda b, i: (b, i, 0)),
        out_shape=jax.ShapeDtypeStruct((B, (nct + nlt) * TM, n), F32),
        compiler_params=_cparams(2),
        name="proj_" + kind,
    )(ctx, x_in, mods3, mods3, norm_g, w)


def _seq_chunk(d, j, nct, ntot):
    bwd = jnp.where(j < nct, nct - 1 - j, ntot - 1 - (j - nct))
    return jnp.where(d == 0, j, bwd)


def _mlstm_kernel(u_ref, hp_ref, hn_ref, cw_ref, cb_ref, gb_ref, o_ref, c_ref, n_ref, m_ref,
                  *, nct, ntot):
    d = pl.program_id(1)
    j = pl.program_id(2)
    c = _seq_chunk(d, j, nct, ntot)
    L = TM

    @pl.when(j == 0)
    def _():
        c_ref[...] = jnp.zeros_like(c_ref)
        n_ref[...] = jnp.zeros_like(n_ref)
        m_ref[...] = jnp.zeros_like(m_ref)

    seg_first = (c == 0) | (c == nct)
    seg_last = (c == nct - 1) | (c == ntot - 1)
    zqk = u_ref[0, :, 0:2 * ML_W]
    prev_row = jnp.where(seg_first, 0.0, hp_ref[0, SUBLANES - 1:SUBLANES, :])
    next_row = jnp.where(seg_last, 0.0, hn_ref[0, 0:1, :])
    zm1, zp1 = _shift_rows(zqk, prev_row, next_row)
    cw = cw_ref[...]
    qk = _silu(zm1 * cw[0:1] + zqk * cw[1:2] + zp1 * cw[2:3] + cb_ref[...])
    zv = u_ref[0, :, 2 * ML_W:3 * ML_W]

    g = u_ref[0, :, 4 * ML_W:4 * ML_W + LANES] + gb_ref[...]
    lane = lax.broadcasted_iota(I32, (L, LANES), 1)
    is_f = ((lane >= 4) & (lane < 8)) | ((lane >= 12) & (lane < 16))
    q_all = jnp.where(is_f, _log_sigmoid(g), g)
    causal = jnp.where(d == 0, _tri(L, True, False, F32), _tri(L, False, False, F32))
    b_all = _dot_sel(causal.astype(BF16), q_all)
    q_t = q_all.T
    b_t = b_all.T
    keep = causal > 0.5

    def lane_pick(a, idx):
        return jnp.sum(jnp.where(lane == idx, a, 0.0), axis=-1, keepdims=True)

    def row_pick(a_t, h):
        return jnp.where(d == 0, a_t[h:h + 1, :], a_t[8 + h:9 + h, :])

    hs = range(ML_HEADS)
    sl = [slice(h * ML_DH, (h + 1) * ML_DH) for h in hs]
    qf = [qk[:, sl[h]] for h in hs]
    q = [qf[h].astype(BF16) for h in hs]
    k = [qk[:, ML_W + h * ML_DH:ML_W + (h + 1) * ML_DH] * (ML_DH ** -0.5) for h in hs]
    kb = [k[h].astype(BF16) for h in hs]
    v = [zv[:, sl[h]] for h in hs]
    li_col = [lane_pick(q_all, h + 8 * d) for h in hs]
    b_col = [lane_pick(b_all, 4 + h + 8 * d) for h in hs]
    li_row = [row_pick(q_t, h) for h in hs]
    b_row = [row_pick(b_t, 4 + h) for h in hs]
    m_prev = [m_ref[h, 0:1, 0:1] for h in hs]
    C = [c_ref[h] for h in hs]
    n_row = [n_ref[h, 0:1, :] for h in hs]

    qk_s = [_dot(q[h], kb[h], NT) for h in hs]
    q_c = [_dot(q[h], C[h].astype(BF16), NT) for h in hs]
    dm = [jnp.where(keep, b_col[h] - b_row[h] + li_row[h], -jnp.inf) for h in hs]
    inter = [b_col[h] + m_prev[h] for h in hs]
    m_t = [jnp.maximum(inter[h], jnp.max(dm[h], axis=-1, keepdims=True)) for h in hs]
    wts = [jnp.exp(dm[h] - m_t[h]) * qk_s[h] for h in hs]
    w_inter = [jnp.exp(inter[h] - m_t[h]) for h in hs]
    num = [_dot(wts[h].astype(BF16), v[h].astype(BF16)) + w_inter[h] * q_c[h] for h in hs]
    for h in hs:
        den = (jnp.sum(wts[h], axis=-1, keepdims=True)
               + w_inter[h] * jnp.sum(qf[h] * n_row[h], axis=-1, keepdims=True))
        o_ref[0, 0, :, sl[h]] = num[h] / jnp.maximum(jnp.abs(den), jnp.exp(-m_t[h]))

    b_end = [jnp.where(d == 0, b_col[h][L - 1:L, :], b_col[h][0:1, :]) for h in hs]
    dec = [b_end[h] - b_col[h] + li_col[h] for h in hs]
    m_new = [jnp.maximum(b_end[h] + m_prev[h], jnp.max(dec[h], axis=0, keepdims=True)) for h in hs]
    wk = [jnp.exp(dec[h] - m_new[h]) for h in hs]
    s_old = [jnp.exp(b_end[h] + m_prev[h] - m_new[h]) for h in hs]
    c_upd = [_dot((v[h] * wk[h]).astype(BF16), kb[h], TN) for h in hs]
    for h in hs:
        c_ref[h] = s_old[h] * C[h] + c_upd[h]
        n_ref[h] = jnp.broadcast_to(
            s_old[h] * n_row[h] + jnp.sum(wk[h] * k[h], axis=0, keepdims=True), (SUBLANES, ML_DH))
        m_ref[h] = jnp.broadcast_to(m_new[h], (SUBLANES, LANES))


def _mlstm_call(u_ml, conv_w, conv_b, gate_b, B, nct, nlt):
    ntot = nct + nlt
    hb = TM // SUBLANES
    nrb = ntot * hb
    chunk = lambda d, j: _seq_chunk(d, j, nct, ntot)
    out_blk = lambda d, j: jnp.where(j < nct, d * (nlt - 1), chunk(d, j) - nct)
    return pl.pallas_call(
        functools.partial(_mlstm_kernel, nct=nct, ntot=ntot),
        grid=(B, 2, ntot),
        in_specs=[
            pl.BlockSpec((1, TM, ML_N), lambda b, d, j: (b, chunk(d, j), 0)),
            pl.BlockSpec((1, SUBLANES, 2 * ML_W),
                         lambda b, d, j: (b, jnp.maximum(chunk(d, j) * hb - 1, 0), 0)),
            pl.BlockSpec((1, SUBLANES, 2 * ML_W),
                         lambda b, d, j: (b, jnp.minimum((chunk(d, j) + 1) * hb, nrb - 1), 0)),
            pl.BlockSpec((3, 2 * ML_W), lambda b, d, j: (0, 0)),
            pl.BlockSpec((1, 2 * ML_W), lambda b, d, j: (0, 0)),
            pl.BlockSpec((1, LANES), lambda b, d, j: (0, 0)),
        ],
        out_specs=pl.BlockSpec((1, 1, TM, ML_W), lambda b, d, j: (d, b, out_blk(d, j), 0)),
        out_shape=jax.ShapeDtypeStruct((2, B, nlt * TM, ML_W), F32),
        scratch_shapes=[pltpu.VMEM((ML_HEADS, ML_DH, ML_DH), F32),
                        pltpu.VMEM((ML_HEADS, SUBLANES, ML_DH), F32),
                        pltpu.VMEM((ML_HEADS, SUBLANES, LANES), F32)],
        compiler_params=_cparams(3),
        name="mlstm",
    )(u_ml, u_ml, u_ml, conv_w, conv_b, gate_b)


def _rw_pre_kernel(u_ref, hp_ref, hn_ref, mu_ref, w0_ref, wup_ref, a0_ref, aup_ref, gup_ref,
                   kk_ref, ka_ref, rk_ref, scan_ref, gb_ref, *, nct, ntot):
    c = pl.program_id(1)
    seg_first = (c == 0) | (c == nct)
    seg_last = (c == nct - 1) | (c == ntot - 1)
    z = u_ref[0]
    prev_row = jnp.where(seg_first, 0.0, hp_ref[0, SUBLANES - 1:SUBLANES, :])
    next_row = jnp.where(seg_last, 0.0, hn_ref[0, 0:1, :])
    zm1, zp1 = _shift_rows(z, prev_row, next_row)
    mu = mu_ref[...]
    xs = z + mu[0:1] * (zm1 - z) + mu[1:2] * (zp1 - z)

    r = xs[:, 0:RW_W]
    k = xs[:, RW_W:2 * RW_W]
    v = xs[:, 2 * RW_W:3 * RW_W]
    wd = xs[:, 3 * RW_W:3 * RW_W + 2 * RW_LORA]
    ad = xs[:, 3 * RW_W + 2 * RW_LORA:3 * RW_W + 4 * RW_LORA]
    gd = xs[:, 3 * RW_W + 4 * RW_LORA:RW_N]

    w_pre = w0_ref[...] + _dot1(jnp.tanh(wd), wup_ref[...])
    log_w = -jax.nn.softplus(-w_pre) - 0.5
    lw = -jnp.exp(log_w)
    a = jax.nn.sigmoid(a0_ref[...] + _dot1(ad, aup_ref[...]))
    g = _dot1(jax.nn.sigmoid(gd), gup_ref[...])

    hr = lax.broadcasted_iota(I32, (RW_W, RW_W), 0) // RW_DH
    hc = lax.broadcasted_iota(I32, (RW_W, RW_W), 1) // RW_DH
    head_ones = (hr == hc).astype(BF16)
    kk = k * kk_ref[...]
    ss = _dot_rsel(kk * kk, head_ones)
    kk = kk / jnp.maximum(jnp.sqrt(ss), 1e-12)
    ka = ka_ref[...]
    k_f = k * (1.0 + (a[:, 0:RW_W] - 1.0) * ka)
    k_b = k * (1.0 + (a[:, RW_W:2 * RW_W] - 1.0) * ka)
    rk = rk_ref[...]
    bonus = _dot_rsel(r * (k_f + k_b) * rk, head_ones) * v

    cols = (r, v, kk, lw[:, 0:RW_W], k_f, kk * a[:, 0:RW_W],
            lw[:, RW_W:2 * RW_W], k_b, kk * a[:, RW_W:2 * RW_W])
    for p in range(RW_PAIRS):
        for qi, arr in enumerate(cols):
            scan_ref[0, p, :, qi * LANES:(qi + 1) * LANES] = arr[:, p * LANES:(p + 1) * LANES]
    gb_ref[0, :, 0:RW_W] = g
    gb_ref[0, :, RW_W:2 * RW_W] = bonus


def _rw_pre_call(u_rw, mu, w0, wup, a0, aup, gup, k_k, k_a, r_k, B, nct, nlt):
    ntot = nct + nlt
    hb = TM // SUBLANES
    nrb = ntot * hb
    full = lambda shape: pl.BlockSpec(shape, lambda b, c: (0,) * len(shape))
    return pl.pallas_call(
        functools.partial(_rw_pre_kernel, nct=nct, ntot=ntot),
        grid=(B, ntot),
        in_specs=[
            pl.BlockSpec((1, TM, RW_N), lambda b, c: (b, c, 0)),
            pl.BlockSpec((1, SUBLANES, RW_N), lambda b, c: (b, jnp.maximum(c * hb - 1, 0), 0)),
            pl.BlockSpec((1, SUBLANES, RW_N),
                         lambda b, c: (b, jnp.minimum((c + 1) * hb, nrb - 1), 0)),
            full((2, RW_N)), full((1, 2 * RW_W)), full((2 * RW_LORA, 2 * RW_W)),
            full((1, 2 * RW_W)), full((2 * RW_LORA, 2 * RW_W)), full((256, RW_W)),
            full((1, RW_W)), full((1, RW_W)), full((1, RW_W)),
        ],
        out_specs=[
            pl.BlockSpec((1, RW_PAIRS, TM, 9 * LANES), lambda b, c: (b, 0, c, 0)),
            pl.BlockSpec((1, TM, 2 * RW_W), lambda b, c: (b, jnp.maximum(c - nct, 0), 0)),
        ],
        out_shape=[jax.ShapeDtypeStruct((B, RW_PAIRS, ntot * TM, 9 * LANES), F32),
                   jax.ShapeDtypeStruct((B, nlt * TM, 2 * RW_W), F32)],
        compiler_params=_cparams(2),
        name="rw_pre",
    )(u_rw, u_rw, u_rw, mu, w0, wup, a0, aup, gup, k_k, k_a, r_k)


def _rw_chunks(blks, states, lowers):
    n = len(blks)
    L = blks[0].shape[0]
    assert L == RW_DH
    idx = range(n)
    col = lambda i, c: blks[i][:, c * LANES:(c + 1) * LANES]
    r, v, kk, lw, kd, bh = ([col(i, c) for i in idx] for c in range(6))
    incl = {lo: _tri(L, lo, False, None) for lo in set(lowers)}
    logp = [_dot_sel(incl[lowers[i]].astype(BF16), lw[i]) for i in idx]
    logp_end = [logp[i][L - 1:L, :] if lowers[i] else logp[i][0:1, :] for i in idx]
    p_inv = [jnp.exp(-logp[i]) for i in idx]
    a_t = [-kk[i] * jnp.exp(logp[i] - lw[i]) for i in idx]
    r_t = [r[i] * jnp.exp(logp[i]) for i in idx]

    row = lax.broadcasted_iota(I32, (L, LANES), 0)
    lane = lax.broadcasted_iota(I32, (L, LANES), 1)
    head0 = lane < RW_DH
    src = jnp.where(head0, lane, lane - RW_DH)

    def split(y_pair):
        return jnp.concatenate([jnp.where(head0, y_pair, 0.0), jnp.where(head0, 0.0, y_pair)],
                               axis=0).astype(BF16)

    def mm(x_cat, y_pair):
        return _dot(x_cat.astype(BF16), split(y_pair))

    strict = {lo: (src < row) if lo else (src > row) for lo in set(lowers)}
    incl_c = {lo: (src <= row) if lo else (src >= row) for lo in set(lowers)}
    eye = jnp.where(src == row, 1.0, 0.0)

    a_s = [_dot1(a_t[i], states[i], NT) for i in idx]
    r_s = [_dot1(r_t[i], states[i], NT) for i in idx]
    bk = [jnp.concatenate([split(bh[i] * p_inv[i]), split(kd[i] * p_inv[i])], axis=0) for i in idx]
    g_a = [_dot(a_t[i].astype(BF16), bk[i], NT) for i in idx]
    g_r = [_dot(r_t[i].astype(BF16), bk[i], NT) for i in idx]
    n_mat = [jnp.where(strict[lowers[i]], g_a[i][:, 0:LANES], 0.0) for i in idx]
    a_ak = [jnp.where(strict[lowers[i]], g_a[i][:, LANES:2 * LANES], 0.0) for i in idx]
    a_rb = [jnp.where(incl_c[lowers[i]], g_r[i][:, 0:LANES], 0.0) for i in idx]
    a_rk = [jnp.where(incl_c[lowers[i]], g_r[i][:, LANES:2 * LANES], 0.0) for i in idx]
    w_rhs = [a_s[i] + mm(a_ak[i], v[i]) for i in idx]
    base = RW_INV_BASE
    n_d = [jnp.where(row // base == src // base, n_mat[i], 0.0) for i in idx]
    inv = [eye + n_d[i] for i in idx]
    n_d2 = [mm(n_d[i], n_d[i]) for i in idx]
    inv = [inv[i] + mm(inv[i], n_d2[i]) for i in idx]
    blk = base
    while blk < L:
        link = (row // (2 * blk) == src // (2 * blk)) & (row // blk != src // blk)
        t_m = [mm(inv[i], jnp.where(link, n_mat[i], 0.0)) for i in idx]
        inv = [inv[i] + mm(t_m[i], inv[i]) for i in idx]
        blk *= 2
    u = [mm(inv[i], w_rhs[i]) for i in idx]
    uv = [jnp.concatenate([u[i], v[i]], axis=0) for i in idx]
    y = [r_s[i] + _dot(jnp.concatenate([a_rb[i], a_rk[i]], axis=1).astype(BF16),
                       jnp.concatenate([split(u[i]), split(v[i])], axis=0)) for i in idx]

    rr = lax.broadcasted_iota(I32, (LANES, LANES), 0) // RW_DH
    cc = lax.broadcasted_iota(I32, (LANES, LANES), 1) // RW_DH
    s_new = []
    for i in idx:
        to_end = jnp.exp(logp_end[i] - logp[i])
        bk_end = jnp.concatenate([bh[i] * to_end, kd[i] * to_end], axis=0)
        s_new.append(jnp.where(
            rr == cc, states[i] * jnp.exp(logp_end[i]) + _dot1(uv[i], bk_end, TN), 0.0))
    return y, s_new


def _rw_scan_kernel(f_ref, b_ref, yf_ref, yb_ref, s_ref):
    j = pl.program_id(2)

    @pl.when(j == 0)
    def _():
        s_ref[...] = jnp.zeros_like(s_ref)

    probs = [(bi, p) for bi in range(f_ref.shape[0]) for p in range(f_ref.shape[1])]
    blks, states, lowers = [], [], []
    for bi, p in probs:
        fb = f_ref[bi, p]
        bb = b_ref[bi, p]
        blks += [fb[:, 0:6 * LANES],
                 jnp.concatenate([bb[:, 0:3 * LANES], bb[:, 6 * LANES:9 * LANES]], axis=1)]
        states += [s_ref[bi, p, 0], s_ref[bi, p, 1]]
        lowers += [True, False]
    y, s_new = _rw_chunks(blks, states, lowers)
    for q, (bi, p) in enumerate(probs):
        yf_ref[bi, p] = y[2 * q]
        yb_ref[bi, p] = y[2 * q + 1]
        s_ref[bi, p, 0] = s_new[2 * q]
        s_ref[bi, p, 1] = s_new[2 * q + 1]


def _rw_scan_call(scan_in, B, t_ctx, t_lat):
    L = RW_CHUNK
    nct, nlt = t_ctx // L, t_lat // L
    ntot = nct + nlt
    fchunk = lambda j: j
    bchunk = lambda j: _seq_chunk(1, j, nct, ntot)
    pps = RW_PAIRS_PER_STEP
    bps = RW_BATCH_PER_STEP if B % RW_BATCH_PER_STEP == 0 else 1
    in_blk = (bps, pps, L, 9 * LANES)
    out_blk = (bps, pps, L, LANES)
    out_shape = jax.ShapeDtypeStruct((B, RW_PAIRS, t_lat, LANES), F32)
    return pl.pallas_call(
        _rw_scan_kernel,
        grid=(B // bps, RW_PAIRS // pps, ntot),
        in_specs=[pl.BlockSpec(in_blk, lambda b, p, j: (b, p, fchunk(j), 0)),
                  pl.BlockSpec(in_blk, lambda b, p, j: (b, p, bchunk(j), 0))],
        out_specs=[
            pl.BlockSpec(out_blk, lambda b, p, j: (b, p, jnp.maximum(fchunk(j) - nct, 0), 0)),
            pl.BlockSpec(out_blk,
                         lambda b, p, j: (b, p, jnp.where(j < nct, nlt - 1, bchunk(j) - nct), 0)),
        ],
        out_shape=[out_shape, out_shape],
        scratch_shapes=[pltpu.VMEM((bps, pps, 2, LANES, LANES), F32)],
        compiler_params=_cparams(3),
        name="rw_scan",
    )(scan_in, scan_in)


def _mix_out_kernel(x_ref, h_ref, zo_ref, yf_ref, yb_ref, gb_ref, g1_ref, sh2_ref, sc2_ref,
                    mlg_ref, lng_ref, lnb_ref, n2g_ref, wo_ref, rw_ref,
                    h1_ref, hn_ref, aff_ref):
    hm = h_ref[0, 0] + h_ref[1, 0]
    parts = []
    for h in range(ML_HEADS):
        hh = hm[:, h * ML_DH:(h + 1) * ML_DH]
        parts.append(hh * lax.rsqrt(jnp.mean(hh * hh, axis=-1, keepdims=True) + EPS))
    ml = jnp.concatenate(parts, axis=1) * mlg_ref[...] * jax.nn.sigmoid(zo_ref[0])

    nr = TM // GRID_W
    r0 = (pl.program_id(1) % (yf_ref.shape[3] // nr)) * nr
    y = jnp.concatenate(
        [jnp.concatenate([yf_ref[0, p, :, r0 + jr, :] + yb_ref[0, p, :, r0 + jr, :]
                          for p in range(RW_PAIRS)], axis=1) for jr in range(nr)], axis=0)
    gbr = jnp.concatenate([gb_ref[0, :, r0 + jr, :] for jr in range(nr)], axis=0)
    hr = lax.broadcasted_iota(I32, (RW_W, RW_W), 0) // RW_DH
    hc = lax.broadcasted_iota(I32, (RW_W, RW_W), 1) // RW_DH
    head_ones = (hr == hc).astype(BF16)
    mean = _dot_rsel(y, head_ones) * (1.0 / RW_DH)
    dy = y - mean
    var = _dot_rsel(dy * dy, head_ones) * (1.0 / RW_DH)
    rw = dy * lax.rsqrt(var + RW_LN_EPS) * lng_ref[...] + lnb_ref[...]
    rw = (rw + gbr[:, RW_W:2 * RW_W]) * gbr[:, 0:RW_W]

    mix = _dot(jnp.concatenate([ml, rw], axis=1).astype(BF16), wo_ref[...])
    h1 = x_ref[0] + g1_ref[0] * mix
    h1_ref[0] = h1
    hn = _rmsnorm(h1, n2g_ref[...]) * (1.0 + sc2_ref[0]) + sh2_ref[0]
    hn_ref[0] = hn.astype(BF16)
    logits = _dot3(hn, rw_ref[...])
    lane = lax.broadcasted_iota(I32, logits.shape, 1)
    logits = jnp.where(lane < N_EXPERTS, logits, -jnp.inf)
    e = jnp.exp(logits - jnp.max(logits, axis=-1, keepdims=True))
    aff = e / jnp.sum(e, axis=-1, keepdims=True)
    aff_ref[0] = aff.T[0:N_EXPERTS, :]


def _mix_out_call(x, h_ml, u_ml, y_f, y_b, gb, mods3, ml_norm_g, ln_g, ln_b, norm2_g,
                  wo, router_pad, nct):
    B, T, _ = x.shape
    nlt = T // TM
    rows = T // GRID_W
    nr = TM // GRID_W
    rb = SUBLANES
    assert rows % rb == 0 and rb % nr == 0
    yv = lambda y: y.reshape(B, RW_PAIRS, GRID_W, rows, LANES)
    gbv = gb.reshape(B, GRID_W, rows, 2 * RW_W)
    row1 = lambda shape: pl.BlockSpec(shape, lambda b, i: (0,) * len(shape))
    mod = lambda k: pl.BlockSpec((1, 1, D_MODEL), lambda b, i: (b, 0, k))
    return pl.pallas_call(
        _mix_out_kernel,
        grid=(B, nlt),
        in_specs=[
            pl.BlockSpec((1, TM, D_MODEL), lambda b, i: (b, i, 0)),
            pl.BlockSpec((2, 1, TM, ML_W), lambda b, i: (0, b, i, 0)),
            pl.BlockSpec((1, TM, ML_W), lambda b, i: (b, i + nct, 3)),
            pl.BlockSpec((1, RW_PAIRS, GRID_W, rb, LANES), lambda b, i: (b, 0, 0, i * nr // rb, 0)),
            pl.BlockSpec((1, RW_PAIRS, GRID_W, rb, LANES), lambda b, i: (b, 0, 0, i * nr // rb, 0)),
            pl.BlockSpec((1, GRID_W, rb, 2 * RW_W), lambda b, i: (b, 0, i * nr // rb, 0)),
            mod(2), mod(3), mod(4),
            row1((1, ML_W)), row1((1, RW_W)), row1((1, RW_W)), row1((1, D_MODEL)),
            row1((D_MODEL, D_MODEL)), row1((D_MODEL, LANES)),
        ],
        out_specs=[
            pl.BlockSpec((1, TM, D_MODEL), lambda b, i: (b, i, 0)),
            pl.BlockSpec((1, TM, D_MODEL), lambda b, i: (b, i, 0)),
            pl.BlockSpec((1, N_EXPERTS, TM), lambda b, i: (b, 0, i)),
        ],
        out_shape=[jax.ShapeDtypeStruct((B, T, D_MODEL), F32),
                   jax.ShapeDtypeStruct((B, T, D_MODEL), BF16),
                   jax.ShapeDtypeStruct((B, N_EXPERTS, T), F32)],
        compiler_params=_cparams(2),
        name="mix_out",
    )(x, h_ml, u_ml, yv(y_f), yv(y_b), gbv, mods3, mods3, mods3,
      ml_norm_g, ln_g, ln_b, norm2_g, wo, router_pad)


def _route_kernel(aff_ref, pos_ref, st_ref, *, cap):
    a = aff_ref[0]
    T = a.shape[1]
    as_f32 = lambda bits: lax.bitcast_convert_type(bits, F32)

    def body(i, thr):
        cand = thr | jnp.left_shift(jnp.int32(1), 30 - i)
        cnt = jnp.sum(jnp.where(a >= as_f32(cand), 1.0, 0.0), axis=1, keepdims=True)
        return jnp.where(cnt >= cap, cand, thr)

    thr = lax.fori_loop(0, 31, body, jnp.zeros((N_EXPERTS, 1), I32))
    gt = a >= as_f32(thr + 1)
    eq = (a >= as_f32(thr)) & jnp.logical_not(gt)
    need = cap - jnp.sum(jnp.where(gt, 1.0, 0.0), axis=1, keepdims=True)
    tri = _tri(TM, False, False, BF16)

    def prefix_excl(mask):
        outs, carries = [], []
        carry = jnp.zeros((N_EXPERTS, 1), F32)
        for blk in range(T // TM):
            seg = jnp.where(mask[:, blk * TM:(blk + 1) * TM], 1.0, 0.0)
            inc = _dot(seg.astype(BF16), tri)
            outs.append(inc - seg + carry)
            carries.append(carry)
            carry = carry + jnp.sum(seg, axis=1, keepdims=True)
        return jnp.concatenate(outs, axis=1), carries + [carry]

    chosen = gt | (eq & (prefix_excl(eq)[0] < need))
    slot, block_starts = prefix_excl(chosen)
    pos_ref[0] = jnp.where(chosen, slot, -1.0).astype(I32)
    lane = lax.broadcasted_iota(I32, (N_EXPERTS, LANES), 1)
    st = jnp.zeros((N_EXPERTS, LANES), F32)
    for blk, start in enumerate(block_starts):
        st = jnp.where(lane == blk, start, st)
    st_ref[0] = st.astype(I32)


def _route_call(aff_t, cap):
    B, E, T = aff_t.shape
    assert T // TM < LANES
    return pl.pallas_call(
        functools.partial(_route_kernel, cap=cap),
        grid=(B,),
        in_specs=[pl.BlockSpec((1, E, T), lambda b: (b, 0, 0))],
        out_specs=[pl.BlockSpec((1, E, T), lambda b: (b, 0, 0)),
                   pl.BlockSpec((1, E, LANES), lambda b: (b, 0, 0))],
        out_shape=[jax.ShapeDtypeStruct((B, E, T), I32),
                   jax.ShapeDtypeStruct((B, E, LANES), I32)],
        compiler_params=_cparams(1),
        name="route",
    )(aff_t)


def _align_down(s, m):
    sh = m.bit_length() - 1
    return pl.multiple_of(lax.shift_left(lax.shift_right_logical(s, sh), sh), m)


def _gather_kernel(st_ref, pos_ref, hn_ref, o_ref, acc_ref, *, cap):
    b = pl.program_id(0)
    e = pl.program_id(1)
    nb = hn_ref.shape[1] // TM
    acc_ref[...] = jnp.zeros_like(acc_ref)
    win = lax.broadcasted_iota(I32, (GATHER_WIN, TM), 0)
    rows = []
    for k in range(nb):
        row0 = _align_down(st_ref[b, e, k], SUBLANES)
        onehot = jnp.where(pos_ref[0, :, k * TM:(k + 1) * TM] == row0 + win, 1.0, 0.0)
        acc_ref[pl.ds(row0, GATHER_WIN), :] += _dot(onehot.astype(BF16), hn_ref[0, k * TM:(k + 1) * TM, :])
        rows.append(row0)
    slot = lax.broadcasted_iota(I32, (cap, TM), 0)
    for k in range(nb):
        @pl.when(st_ref[b, e, k + 1] > rows[k] + GATHER_WIN)
        def _():
            rest = (pos_ref[0, :, k * TM:(k + 1) * TM] == slot) & (slot >= rows[k] + GATHER_WIN)
            acc_ref[0:cap, :] += _dot(jnp.where(rest, 1.0, 0.0).astype(BF16),
                                      hn_ref[0, k * TM:(k + 1) * TM, :])
    o_ref[0] = acc_ref[0:cap, :].astype(BF16)


def _gather_call(starts, pos, hn, cap):
    B, E, T = pos.shape
    grid_spec = pltpu.PrefetchScalarGridSpec(
        num_scalar_prefetch=1,
        grid=(B, E),
        in_specs=[pl.BlockSpec((1, 1, T), lambda b, e, st: (b * E + e, 0, 0)),
                  pl.BlockSpec((1, T, D_MODEL), lambda b, e, st: (b, 0, 0))],
        out_specs=pl.BlockSpec((1, cap, D_MODEL), lambda b, e, st: (e, b, 0)),
        scratch_shapes=[pltpu.VMEM((cap + GATHER_WIN, D_MODEL), F32)])
    return pl.pallas_call(
        functools.partial(_gather_kernel, cap=cap),
        grid_spec=grid_spec,
        out_shape=jax.ShapeDtypeStruct((E, B * cap, D_MODEL), BF16),
        compiler_params=_cparams(2),
        name="gather",
    )(starts, pos.reshape(B * E, 1, T), hn)


def _ffn_kernel(x_ref, w1_ref, w3_ref, w2_ref, o_ref, acc_ref):
    f = pl.program_id(2)

    @pl.when(f == 0)
    def _():
        acc_ref[...] = jnp.zeros_like(acc_ref)

    x = x_ref[0]
    h1 = _dot(x, w1_ref[0].astype(BF16))
    h3 = _dot(x, w3_ref[0].astype(BF16))
    hid = (_silu(h1) * h3).astype(BF16)
    acc_ref[...] += _dot(hid, w2_ref[0].astype(BF16))

    @pl.when(f == pl.num_programs(2) - 1)
    def _():
        o_ref[0] = acc_ref[...].astype(BF16)


def _ffn_call(xg, w1, w3, w2):
    E, M, _ = xg.shape
    tm = min(M, 2048)
    nf = D_EXPERT // FFN_TF
    return pl.pallas_call(
        _ffn_kernel,
        grid=(E, M // tm, nf),
        in_specs=[pl.BlockSpec((1, tm, D_MODEL), lambda e, m, f: (e, m, 0)),
                  pl.BlockSpec((1, D_MODEL, FFN_TF), lambda e, m, f: (e, 0, f)),
                  pl.BlockSpec((1, D_MODEL, FFN_TF), lambda e, m, f: (e, 0, f)),
                  pl.BlockSpec((1, FFN_TF, D_MODEL), lambda e, m, f: (e, f, 0))],
        out_specs=pl.BlockSpec((1, tm, D_MODEL), lambda e, m, f: (e, m, 0)),
        out_shape=jax.ShapeDtypeStruct((E, M, D_MODEL), BF16),
        scratch_shapes=[pltpu.VMEM((tm, D_MODEL), F32)],
        compiler_params=_cparams(3),
        name="ffn",
    )(xg, w1, w3, w2)


def _combine_kernel(st_ref, pos_ref, aff_ref, y_ref, h1_ref, g2_ref, fg_ref, o_ref, moe_ref, *, cap):
    b = pl.program_id(0)
    k = pl.program_id(1)
    pos = pos_ref[0]
    aff = aff_ref[0]
    win = lax.broadcasted_iota(I32, (COMBINE_WIN, TM), 0)
    ws, ys, rows = [], [], []
    for e in range(N_EXPERTS):
        row0 = jnp.minimum(_align_down(st_ref[b, e, k], BF16_ROWS), cap - COMBINE_WIN)
        row0 = pl.multiple_of(row0, BF16_ROWS)
        ws.append(jnp.where(pos[e:e + 1, :] == row0 + win, aff[e:e + 1, :], 0.0).astype(BF16))
        ys.append(y_ref[e, pl.ds(row0, COMBINE_WIN), :])
        rows.append(row0)
    moe_ref[...] = _dot(jnp.concatenate(ws, axis=0), jnp.concatenate(ys, axis=0), TN)
    slot = lax.broadcasted_iota(I32, (cap, TM), 0)
    for e in range(N_EXPERTS):
        @pl.when(st_ref[b, e, k + 1] > rows[e] + COMBINE_WIN)
        def _():
            rest = (pos[e:e + 1, :] == slot) & (slot >= rows[e] + COMBINE_WIN)
            moe_ref[...] += _dot(jnp.where(rest, aff[e:e + 1, :], 0.0).astype(BF16), y_ref[e], TN)
    h2 = h1_ref[0] + g2_ref[0] * moe_ref[...]
    o_ref[0] = _rmsnorm(h2, fg_ref[...])


def _combine_call(starts, pos, aff_t, ys, h1, mods3, final_g, cap):
    B, E, T = pos.shape
    assert cap >= COMBINE_WIN and (cap - COMBINE_WIN) % BF16_ROWS == 0
    grid_spec = pltpu.PrefetchScalarGridSpec(
        num_scalar_prefetch=1,
        grid=(B, T // TM),
        in_specs=[pl.BlockSpec((1, E, TM), lambda b, i, st: (b, 0, i)),
                  pl.BlockSpec((1, E, TM), lambda b, i, st: (b, 0, i)),
                  pl.BlockSpec((E, cap, D_MODEL), lambda b, i, st: (0, b, 0)),
                  pl.BlockSpec((1, TM, D_MODEL), lambda b, i, st: (b, i, 0)),
                  pl.BlockSpec((1, 1, D_MODEL), lambda b, i, st: (b, 0, 5)),
                  pl.BlockSpec((1, D_MODEL), lambda b, i, st: (0, 0))],
        out_specs=pl.BlockSpec((1, TM, D_MODEL), lambda b, i, st: (b, i, 0)),
        scratch_shapes=[pltpu.VMEM((TM, D_MODEL), F32)])
    return pl.pallas_call(
        functools.partial(_combine_kernel, cap=cap),
        grid_spec=grid_spec,
        out_shape=jax.ShapeDtypeStruct((B, T, D_MODEL), F32),
        compiler_params=_cparams(2),
        name="combine",
    )(starts, pos, aff_t, ys, h1, mods3, final_g)


def _pad_cols(w, n):
    return jnp.pad(w, ((0, 0), (0, n - w.shape[1])))


def _both_dirs(up):
    z = jnp.zeros_like(up[0])
    return jnp.concatenate([jnp.concatenate([up[0], z], axis=1),
                            jnp.concatenate([z, up[1]], axis=1)], axis=0)


def kernel(x, c, ctx, c_ctx, mod_w, mod_b, norm1_g, w_in, ml_conv_w, ml_conv_b, ml_gate_b, ml_norm_g,
           rw_mu, rw_w0, rw_w_up, rw_a0, rw_a_up, rw_g_up, rw_k_k, rw_k_a, rw_r_k, rw_ln_g, rw_ln_b,
           w_out, norm2_g, router_w, exp_w1, exp_w3, exp_w2, final_g):
    B, T, D = x.shape
    t_ctx = ctx.shape[1]
    assert D == D_MODEL and T % TM == 0 and t_ctx % TM == 0 and TM % (T // GRID_W) == 0
    assert mod_w.shape[0] == 1 and B < 16
    nct, nlt = t_ctx // TM, T // TM
    cap = EC_FACTOR * T // N_EXPERTS
    ml_cols = 4 * ML_W + ML_GATES

    cc = jnp.concatenate([c, c_ctx[None, :], jnp.zeros((16 - B - 1, D), F32)], axis=0)
    mods = _mod_call(cc, mod_w[0], mod_b)
    mods3 = mods.reshape(16, 1, 6 * D)
    g1n = norm1_g.reshape(1, D)

    w_ml = _pad_cols(w_in[0, :, :ml_cols], ML_N).astype(BF16)
    w_rw = _pad_cols(w_in[0, :, ml_cols:], RW_N).astype(BF16)
    u_ml = _proj_call("ml", x, ctx, mods3, g1n, w_ml)
    u_rw = _proj_call("rw", x, ctx, mods3, g1n, w_rw)

    h_ml = _mlstm_call(u_ml, ml_conv_w[0], ml_conv_b, _pad_cols(ml_gate_b, LANES), B, nct, nlt)

    gup = jnp.pad(rw_g_up[0], ((0, 256 - RW_LORA_G), (0, 0)))
    scan_in, gb = _rw_pre_call(
        u_rw, _pad_cols(rw_mu[0], RW_N), rw_w0[0].reshape(1, 2 * RW_W), _both_dirs(rw_w_up[0]),
        rw_a0[0].reshape(1, 2 * RW_W), _both_dirs(rw_a_up[0]), gup,
        rw_k_k, rw_k_a, rw_r_k[0].reshape(1, RW_W), B, nct, nlt)
    y_f, y_b = _rw_scan_call(scan_in, B, t_ctx, T)

    h1, hn, aff_t = _mix_out_call(
        x, h_ml, u_ml, y_f, y_b, gb, mods3, ml_norm_g, rw_ln_g, rw_ln_b, norm2_g,
        w_out[0].astype(BF16), _pad_cols(router_w[0], LANES), nct)

    pos, starts = _route_call(aff_t, cap)
    starts = starts[:, :, :nlt + 1]
    xg = _gather_call(starts, pos, hn, cap)
    ys = _ffn_call(xg, exp_w1[0], exp_w3[0], exp_w2[0])
    return _combine_call(starts, pos, aff_t, ys, h1, mods3, final_g.reshape(1, D), cap)
```

```python
import functools

import jax
import jax.numpy as jnp
from jax import lax
from jax.experimental import pallas as pl
from jax.experimental.pallas import tpu as pltpu

F32 = jnp.float32
BF16 = jnp.bfloat16
I32 = jnp.int32

D_MODEL = 1024
GRID_W = 64
EPS = 1e-6
ML_W = 512
ML_HEADS = 4
ML_DH = 128
ML_GATES = 16
ML_N = 4 * ML_W + 128
RW_W = 512
RW_HEADS = 8
RW_DH = 64
RW_LORA = 64
RW_LORA_G = 160
RW_LN_EPS = 64e-5
RW_COLS = 3 * RW_W + 2 * RW_LORA + 2 * RW_LORA + RW_LORA_G
RW_N = 2048
RW_PAIRS = RW_HEADS // 2
N_EXPERTS = 16
EC_FACTOR = 2
D_EXPERT = 2816

LANES = 128
SUBLANES = 8
HALO = SUBLANES
TM = 256
RW_CHUNK = 64
RW_PAIRS_PER_STEP = 4
RW_BATCH_PER_STEP = 4
RW_INV_BASE = 4
FFN_TF = 256
BF16_ROWS = 16
GATHER_WIN = 64
COMBINE_WIN = 80
VMEM_LIMIT = 56 * 1024 * 1024

NN = (((1,), (0,)), ((), ()))
NT = (((1,), (1,)), ((), ()))
TN = (((0,), (0,)), ((), ()))


def _cparams(n_axes):
    return pltpu.CompilerParams(
        dimension_semantics=("arbitrary",) * n_axes, vmem_limit_bytes=VMEM_LIMIT)


def _dot(a, b, dn=NN):
    return lax.dot_general(a, b, dn, preferred_element_type=F32)


def _split2(a):
    hi = a.astype(BF16)
    lo = (a - hi.astype(F32)).astype(BF16)
    return hi, lo


def _split3(a):
    hi = a.astype(BF16)
    r1 = a - hi.astype(F32)
    mid = r1.astype(BF16)
    lo = (r1 - mid.astype(F32)).astype(BF16)
    return hi, mid, lo


def _dot1(a, b, dn=NN):
    return _dot(a.astype(BF16), b.astype(BF16), dn)


def _dot3(a, b, dn=NN):
    ah, al = _split2(a)
    bh, bl = _split2(b)
    return _dot(ah, bh, dn) + (_dot(ah, bl, dn) + _dot(al, bh, dn))


def _dot_sel(sel_bf16, b, dn=NN):
    bh, bm, bl = _split3(b)
    return _dot(sel_bf16, bh, dn) + (_dot(sel_bf16, bm, dn) + _dot(sel_bf16, bl, dn))


def _dot_rsel(a, sel_bf16, dn=NN):
    ah, al = _split2(a)
    return _dot(ah, sel_bf16, dn) + _dot(al, sel_bf16, dn)


def _silu(x):
    return x * jax.nn.sigmoid(x)


def _log_sigmoid(x):
    return jnp.minimum(x, 0.0) - jnp.log1p(jnp.exp(-jnp.abs(x)))


def _rmsnorm(x, g):
    return x * lax.rsqrt(jnp.mean(x * x, axis=-1, keepdims=True) + EPS) * g


def _tri(n, lower, strict, dtype):
    r = lax.broadcasted_iota(I32, (n, n), 0)
    c = lax.broadcasted_iota(I32, (n, n), 1)
    if lower:
        m = (c < r) if strict else (c <= r)
    else:
        m = (c > r) if strict else (c >= r)
    return m if dtype is None else m.astype(dtype)


def _shift_rows(z, prev_row, next_row):
    n = z.shape[0]
    row = lax.broadcasted_iota(I32, z.shape, 0)
    zm1 = jnp.where(row == 0, prev_row, pltpu.roll(z, 1, 0))
    zp1 = jnp.where(row == n - 1, next_row, pltpu.roll(z, n - 1, 0))
    return zm1, zp1


def _mod_kernel(c_ref, w_ref, b_ref, o_ref):
    o_ref[...] = _dot3(_silu(c_ref[...]), w_ref[...]) + b_ref[...]


def _mod_call(cc, mod_w, mod_b):
    n = mod_w.shape[1]
    tn = 1024
    return pl.pallas_call(
        _mod_kernel,
        grid=(n // tn,),
        in_specs=[pl.BlockSpec((16, D_MODEL), lambda j: (0, 0)),
                  pl.BlockSpec((D_MODEL, tn), lambda j: (0, j)),
                  pl.BlockSpec((1, tn), lambda j: (0, j))],
        out_specs=pl.BlockSpec((16, tn), lambda j: (0, j)),
        out_shape=jax.ShapeDtypeStruct((16, n), F32),
        compiler_params=_cparams(1),
        name="mod",
    )(cc, mod_w, mod_b)


def _project_tile(xin, prev_rows, next_rows, sh, sc, g, w_ref):
    xe = jnp.concatenate([prev_rows, xin, next_rows], axis=0)
    xn = _rmsnorm(xe, g) * (1.0 + sc) + sh
    return _dot(xn.astype(BF16), w_ref[...])


def _tile_and_edges(u, seg_first, seg_last):
    z = u[HALO:HALO + TM]
    prev_row = jnp.where(seg_first, 0.0, u[HALO - 1:HALO])
    next_row = jnp.where(seg_last, 0.0, u[HALO + TM:HALO + TM + 1])
    return z, prev_row, next_row


def _ml_in_kernel(ctx_ref, x_ref, xp_ref, xn_ref, sh_ref, sc_ref, g_ref, w_ref, cw_ref, cb_ref, gb_ref,
                  qkv_ref, gate_ref, zo_ref, *, nct, ntot):
    i = pl.program_id(1)
    xin = jnp.where(i < nct, ctx_ref[0], x_ref[0])
    u = _project_tile(xin, xp_ref[0], xn_ref[0], sh_ref[0], sc_ref[0], g_ref[...], w_ref)
    z, prev_row, next_row = _tile_and_edges(
        u, (i == 0) | (i == nct), (i == nct - 1) | (i == ntot - 1))
    zqk = z[:, 0:2 * ML_W]
    zm1, zp1 = _shift_rows(zqk, prev_row[:, 0:2 * ML_W], next_row[:, 0:2 * ML_W])
    cw = cw_ref[...]
    qk = _silu(zm1 * cw[0:1] + zqk * cw[1:2] + zp1 * cw[2:3] + cb_ref[...])
    qkv_ref[0, :, 0:ML_W] = qk[:, 0:ML_W].astype(BF16)
    qkv_ref[0, :, ML_W:2 * ML_W] = (qk[:, ML_W:2 * ML_W] * (ML_DH ** -0.5)).astype(BF16)
    qkv_ref[0, :, 2 * ML_W:3 * ML_W] = z[:, 2 * ML_W:3 * ML_W].astype(BF16)
    zo_ref[0] = z[:, 3 * ML_W:4 * ML_W]
    g = z[:, 4 * ML_W:4 * ML_W + LANES] + gb_ref[...]
    lane = lax.broadcasted_iota(I32, (TM, LANES), 1)
    is_f = ((lane >= 4) & (lane < 8)) | ((lane >= 12) & (lane < 16))
    gate_ref[0] = jnp.where(is_f, _log_sigmoid(g), g)


def _ml_in_call(x, ctx, mods3, norm_g, w, conv_w, conv_b, gate_b):
    B, T, _ = x.shape
    assert ctx.shape[1] == TM
    nct, nlt = 1, T // TM
    ntot = nct + nlt
    hb = TM // HALO
    lat = lambda i: jnp.maximum(i - nct, 0)
    mrow = lambda b, i: jnp.where(i < nct, B, b)
    full = lambda shape: pl.BlockSpec(shape, lambda b, i: (0,) * len(shape))
    return pl.pallas_call(
        functools.partial(_ml_in_kernel, nct=nct, ntot=ntot),
        grid=(B, ntot),
        in_specs=[
            pl.BlockSpec((1, TM, D_MODEL), lambda b, i: (b, 0, 0)),
            pl.BlockSpec((1, TM, D_MODEL), lambda b, i: (b, lat(i), 0)),
            pl.BlockSpec((1, HALO, D_MODEL), lambda b, i: (b, jnp.maximum(lat(i) * hb - 1, 0), 0)),
            pl.BlockSpec((1, HALO, D_MODEL),
                         lambda b, i: (b, jnp.minimum((lat(i) + 1) * hb, nlt * hb - 1), 0)),
            pl.BlockSpec((1, 1, D_MODEL), lambda b, i: (mrow(b, i), 0, 0)),
            pl.BlockSpec((1, 1, D_MODEL), lambda b, i: (mrow(b, i), 0, 1)),
            full((1, D_MODEL)), full((D_MODEL, ML_N)),
            full((3, 2 * ML_W)), full((1, 2 * ML_W)), full((1, LANES)),
        ],
        out_specs=[
            pl.BlockSpec((1, TM, 3 * ML_W), lambda b, i: (b, i, 0)),
            pl.BlockSpec((1, TM, LANES), lambda b, i: (b, i, 0)),
            pl.BlockSpec((1, TM, ML_W), lambda b, i: (b, lat(i), 0)),
        ],
        out_shape=[jax.ShapeDtypeStruct((B, ntot * TM, 3 * ML_W), BF16),
                   jax.ShapeDtypeStruct((B, ntot * TM, LANES), F32),
                   jax.ShapeDtypeStruct((B, nlt * TM, ML_W), F32)],
        compiler_params=_cparams(2),
        name="ml_in",
    )(ctx, x, x, x, mods3, mods3, norm_g, w, conv_w, conv_b, gate_b)


def _seq_chunk(d, j, nct, ntot):
    bwd = jnp.where(j < nct, nct - 1 - j, ntot - 1 - (j - nct))
    return jnp.where(d == 0, j, bwd)


def _mlstm_kernel(qkv_ref, gate_ref, o_ref, c_ref, n_ref, m_ref):
    d = pl.program_id(1)
    j = pl.program_id(2)
    L = TM

    @pl.when(j == 0)
    def _():
        c_ref[...] = jnp.zeros_like(c_ref)
        n_ref[...] = jnp.zeros_like(n_ref)
        m_ref[...] = jnp.zeros_like(m_ref)

    q_all = gate_ref[0]
    lane = lax.broadcasted_iota(I32, (L, LANES), 1)
    causal = jnp.where(d == 0, _tri(L, True, False, F32), _tri(L, False, False, F32))
    b_all = _dot_sel(causal.astype(BF16), q_all)
    q_t = q_all.T
    b_t = b_all.T
    keep = causal > 0.5

    def lane_pick(a, idx):
        return jnp.sum(jnp.where(lane == idx, a, 0.0), axis=-1, keepdims=True)

    def row_pick(a_t, h):
        return jnp.where(d == 0, a_t[h:h + 1, :], a_t[8 + h:9 + h, :])

    hs = range(ML_HEADS)
    sl = [slice(h * ML_DH, (h + 1) * ML_DH) for h in hs]
    q = [qkv_ref[0, :, sl[h]] for h in hs]
    kb = [qkv_ref[0, :, ML_W + h * ML_DH:ML_W + (h + 1) * ML_DH] for h in hs]
    vb = [qkv_ref[0, :, 2 * ML_W + h * ML_DH:2 * ML_W + (h + 1) * ML_DH] for h in hs]
    qf = [q[h].astype(F32) for h in hs]
    k = [kb[h].astype(F32) for h in hs]
    v = [vb[h].astype(F32) for h in hs]
    li_col = [lane_pick(q_all, h + 8 * d) for h in hs]
    b_col = [lane_pick(b_all, 4 + h + 8 * d) for h in hs]
    li_row = [row_pick(q_t, h) for h in hs]
    b_row = [row_pick(b_t, 4 + h) for h in hs]
    m_prev = [m_ref[h, 0:1, 0:1] for h in hs]
    C = [c_ref[h] for h in hs]
    n_row = [n_ref[h, 0:1, :] for h in hs]

    qk_s = [_dot(q[h], kb[h], NT) for h in hs]
    q_c = [_dot(q[h], C[h].astype(BF16), NT) for h in hs]
    dm = [jnp.where(keep, b_col[h] - b_row[h] + li_row[h], -jnp.inf) for h in hs]
    inter = [b_col[h] + m_prev[h] for h in hs]
    m_t = [jnp.maximum(inter[h], jnp.max(dm[h], axis=-1, keepdims=True)) for h in hs]
    wts = [jnp.exp(dm[h] - m_t[h]) * qk_s[h] for h in hs]
    w_inter = [jnp.exp(inter[h] - m_t[h]) for h in hs]
    num = [_dot(wts[h].astype(BF16), vb[h]) + w_inter[h] * q_c[h] for h in hs]
    for h in hs:
        den = (jnp.sum(wts[h], axis=-1, keepdims=True)
               + w_inter[h] * jnp.sum(qf[h] * n_row[h], axis=-1, keepdims=True))
        o_ref[0, 0, :, sl[h]] = num[h] / jnp.maximum(jnp.abs(den), jnp.exp(-m_t[h]))

    b_end = [jnp.where(d == 0, b_col[h][L - 1:L, :], b_col[h][0:1, :]) for h in hs]
    dec = [b_end[h] - b_col[h] + li_col[h] for h in hs]
    m_new = [jnp.maximum(b_end[h] + m_prev[h], jnp.max(dec[h], axis=0, keepdims=True)) for h in hs]
    wk = [jnp.exp(dec[h] - m_new[h]) for h in hs]
    s_old = [jnp.exp(b_end[h] + m_prev[h] - m_new[h]) for h in hs]
    c_upd = [_dot((v[h] * wk[h]).astype(BF16), kb[h], TN) for h in hs]
    for h in hs:
        c_ref[h] = s_old[h] * C[h] + c_upd[h]
        n_ref[h] = jnp.broadcast_to(
            s_old[h] * n_row[h] + jnp.sum(wk[h] * k[h], axis=0, keepdims=True), (SUBLANES, ML_DH))
        m_ref[h] = jnp.broadcast_to(m_new[h], (SUBLANES, LANES))


def _mlstm_call(qkv, gates, B, nct, nlt):
    ntot = nct + nlt
    chunk = lambda d, j: _seq_chunk(d, j, nct, ntot)
    out_blk = lambda d, j: jnp.where(j < nct, d * (nlt - 1), chunk(d, j) - nct)
    return pl.pallas_call(
        _mlstm_kernel,
        grid=(B, 2, ntot),
        in_specs=[
            pl.BlockSpec((1, TM, 3 * ML_W), lambda b, d, j: (b, chunk(d, j), 0)),
            pl.BlockSpec((1, TM, LANES), lambda b, d, j: (b, chunk(d, j), 0)),
        ],
        out_specs=pl.BlockSpec((1, 1, TM, ML_W), lambda b, d, j: (d, b, out_blk(d, j), 0)),
        out_shape=jax.ShapeDtypeStruct((2, B, nlt * TM, ML_W), F32),
        scratch_shapes=[pltpu.VMEM((ML_HEADS, ML_DH, ML_DH), F32),
                        pltpu.VMEM((ML_HEADS, SUBLANES, ML_DH), F32),
                        pltpu.VMEM((ML_HEADS, SUBLANES, LANES), F32)],
        compiler_params=_cparams(3),
        name="mlstm",
    )(qkv, gates)


def _rw_in_kernel(ctx_ref, x_ref, xp_ref, xn_ref, sh_ref, sc_ref, g_ref, w_ref,
                  mu_ref, w0_ref, wup_ref, a0_ref, aup_ref, gup_ref, kk_ref, ka_ref, rk_ref,
                  scan_ref, gb_ref, *, nct, ntot, rows):
    i = pl.program_id(1)
    jt = jnp.maximum(i - nct, 0)
    cpt = TM // rows
    tpb = x_ref.shape[2] // cpt
    c0 = (jt % tpb) * cpt if tpb > 1 else 0
    xcm = jnp.concatenate([x_ref[0, :, c0 + c, :] for c in range(cpt)], axis=0)
    xin = jnp.where(i < nct, ctx_ref[0], xcm)
    prev1 = xp_ref[0, SUBLANES - 1:SUBLANES, jnp.maximum(jt * cpt - 1, 0) % SUBLANES, :]
    next1 = xn_ref[0, 0:1, jnp.minimum((jt + 1) * cpt, GRID_W - 1) % SUBLANES, :]
    u = _project_tile(xin, jnp.broadcast_to(prev1, (HALO, D_MODEL)),
                      jnp.broadcast_to(next1, (HALO, D_MODEL)),
                      sh_ref[0], sc_ref[0], g_ref[...], w_ref)
    z, prev_row, next_row = _tile_and_edges(
        u, (i == 0) | (i == nct), (i == nct - 1) | (i == ntot - 1))
    zm1, zp1 = _shift_rows(z, prev_row, next_row)
    mu = mu_ref[...]
    xs = z + mu[0:1] * (zm1 - z) + mu[1:2] * (zp1 - z)

    r = xs[:, 0:RW_W]
    k = xs[:, RW_W:2 * RW_W]
    v = xs[:, 2 * RW_W:3 * RW_W]
    wd = xs[:, 3 * RW_W:3 * RW_W + 2 * RW_LORA]
    ad = xs[:, 3 * RW_W + 2 * RW_LORA:3 * RW_W + 4 * RW_LORA]
    gd = xs[:, 3 * RW_W + 4 * RW_LORA:RW_N]

    w_pre = w0_ref[...] + _dot1(jnp.tanh(wd), wup_ref[...])
    log_w = -jax.nn.softplus(-w_pre) - 0.5
    lw = -jnp.exp(log_w)
    a = jax.nn.sigmoid(a0_ref[...] + _dot1(ad, aup_ref[...]))
    g = _dot1(jax.nn.sigmoid(gd), gup_ref[...])

    hr = lax.broadcasted_iota(I32, (RW_W, RW_W), 0) // RW_DH
    hc = lax.broadcasted_iota(I32, (RW_W, RW_W), 1) // RW_DH
    head_ones = (hr == hc).astype(BF16)
    kk = k * kk_ref[...]
    ss = _dot_rsel(kk * kk, head_ones)
    kk = kk / jnp.maximum(jnp.sqrt(ss), 1e-12)
    ka = ka_ref[...]
    k_f = k * (1.0 + (a[:, 0:RW_W] - 1.0) * ka)
    k_b = k * (1.0 + (a[:, RW_W:2 * RW_W] - 1.0) * ka)
    rk = rk_ref[...]
    bonus = _dot_rsel(r * (k_f + k_b) * rk, head_ones) * v

    cols = (r, v, kk, lw[:, 0:RW_W], k_f, kk * a[:, 0:RW_W],
            lw[:, RW_W:2 * RW_W], k_b, kk * a[:, RW_W:2 * RW_W])
    for p in range(RW_PAIRS):
        for qi, arr in enumerate(cols):
            scan_ref[0, p, :, qi * LANES:(qi + 1) * LANES] = arr[:, p * LANES:(p + 1) * LANES]
    gb_ref[0, :, 0:RW_W] = g
    gb_ref[0, :, RW_W:2 * RW_W] = bonus


def _rw_in_call(x, ctx, mods3, norm_g, w, mu, w0, wup, a0, aup, gup, k_k, k_a, r_k):
    B, T, _ = x.shape
    assert ctx.shape[1] == TM
    nct, nlt = 1, T // TM
    ntot = nct + nlt
    rows = T // GRID_W
    cpt = TM // rows
    cb = max(SUBLANES, cpt)
    tpb = cb // cpt
    assert rows % SUBLANES == 0 and GRID_W % cb == 0 and cb % cpt == 0
    lat = lambda c: jnp.maximum(c - nct, 0)
    mrow = lambda b, c: jnp.where(c < nct, B, b)
    full = lambda shape: pl.BlockSpec(shape, lambda b, c: (0,) * len(shape))
    x4 = x.reshape(B, rows, GRID_W, D_MODEL)
    return pl.pallas_call(
        functools.partial(_rw_in_kernel, nct=nct, ntot=ntot, rows=rows),
        grid=(B, ntot),
        in_specs=[
            pl.BlockSpec((1, TM, D_MODEL), lambda b, c: (b, 0, 0)),
            pl.BlockSpec((1, rows, cb, D_MODEL), lambda b, c: (b, 0, lat(c) // tpb, 0)),
            pl.BlockSpec((1, SUBLANES, SUBLANES, D_MODEL),
                         lambda b, c: (b, rows // SUBLANES - 1,
                                       jnp.maximum(lat(c) * cpt - 1, 0) // SUBLANES, 0)),
            pl.BlockSpec((1, SUBLANES, SUBLANES, D_MODEL),
                         lambda b, c: (b, 0, jnp.minimum((lat(c) + 1) * cpt, GRID_W - 1) // SUBLANES, 0)),
            pl.BlockSpec((1, 1, D_MODEL), lambda b, c: (mrow(b, c), 0, 0)),
            pl.BlockSpec((1, 1, D_MODEL), lambda b, c: (mrow(b, c), 0, 1)),
            full((1, D_MODEL)), full((D_MODEL, RW_N)),
            full((2, RW_N)), full((1, 2 * RW_W)), full((2 * RW_LORA, 2 * RW_W)),
            full((1, 2 * RW_W)), full((2 * RW_LORA, 2 * RW_W)), full((256, RW_W)),
            full((1, RW_W)), full((1, RW_W)), full((1, RW_W)),
        ],
        out_specs=[
            pl.BlockSpec((1, RW_PAIRS, TM, 9 * LANES), lambda b, c: (b, 0, c, 0)),
            pl.BlockSpec((1, TM, 2 * RW_W), lambda b, c: (b, jnp.maximum(c - nct, 0), 0)),
        ],
        out_shape=[jax.ShapeDtypeStruct((B, RW_PAIRS, ntot * TM, 9 * LANES), F32),
                   jax.ShapeDtypeStruct((B, nlt * TM, 2 * RW_W), F32)],
        compiler_params=_cparams(2),
        name="rw_in",
    )(ctx, x4, x4, x4, mods3, mods3, norm_g, w, mu, w0, wup, a0, aup, gup, k_k, k_a, r_k)


def _rw_chunks(blks, states, lowers):
    n = len(blks)
    L = blks[0].shape[0]
    assert L == RW_DH
    idx = range(n)
    col = lambda i, c: blks[i][:, c * LANES:(c + 1) * LANES]
    r, v, kk, lw, kd, bh = ([col(i, c) for i in idx] for c in range(6))
    incl = {lo: _tri(L, lo, False, None) for lo in set(lowers)}
    logp = [_dot_sel(incl[lowers[i]].astype(BF16), lw[i]) for i in idx]
    logp_end = [logp[i][L - 1:L, :] if lowers[i] else logp[i][0:1, :] for i in idx]
    p_inv = [jnp.exp(-logp[i]) for i in idx]
    a_t = [-kk[i] * jnp.exp(logp[i] - lw[i]) for i in idx]
    r_t = [r[i] * jnp.exp(logp[i]) for i in idx]

    row = lax.broadcasted_iota(I32, (L, LANES), 0)
    lane = lax.broadcasted_iota(I32, (L, LANES), 1)
    head0 = lane < RW_DH
    src = jnp.where(head0, lane, lane - RW_DH)

    def split(y_pair):
        return jnp.concatenate([jnp.where(head0, y_pair, 0.0), jnp.where(head0, 0.0, y_pair)],
                               axis=0).astype(BF16)

    def mm(x_cat, y_pair):
        return _dot(x_cat.astype(BF16), split(y_pair))

    strict = {lo: (src < row) if lo else (src > row) for lo in set(lowers)}
    incl_c = {lo: (src <= row) if lo else (src >= row) for lo in set(lowers)}
    eye = jnp.where(src == row, 1.0, 0.0)

    a_s = [_dot1(a_t[i], states[i], NT) for i in idx]
    r_s = [_dot1(r_t[i], states[i], NT) for i in idx]
    bk = [jnp.concatenate([split(bh[i] * p_inv[i]), split(kd[i] * p_inv[i])], axis=0) for i in idx]
    g_a = [_dot(a_t[i].astype(BF16), bk[i], NT) for i in idx]
    g_r = [_dot(r_t[i].astype(BF16), bk[i], NT) for i in idx]
    n_mat = [jnp.where(strict[lowers[i]], g_a[i][:, 0:LANES], 0.0) for i in idx]
    a_ak = [jnp.where(strict[lowers[i]], g_a[i][:, LANES:2 * LANES], 0.0) for i in idx]
    a_rb = [jnp.where(incl_c[lowers[i]], g_r[i][:, 0:LANES], 0.0) for i in idx]
    a_rk = [jnp.where(incl_c[lowers[i]], g_r[i][:, LANES:2 * LANES], 0.0) for i in idx]
    w_rhs = [a_s[i] + mm(a_ak[i], v[i]) for i in idx]
    base = RW_INV_BASE
    n_d = [jnp.where(row // base == src // base, n_mat[i], 0.0) for i in idx]
    inv = [eye + n_d[i] for i in idx]
    n_d2 = [mm(n_d[i], n_d[i]) for i in idx]
    inv = [inv[i] + mm(inv[i], n_d2[i]) for i in idx]
    blk = base
    while blk < L:
        link = (row // (2 * blk) == src // (2 * blk)) & (row // blk != src // blk)
        t_m = [mm(inv[i], jnp.where(link, n_mat[i], 0.0)) for i in idx]
        inv = [inv[i] + mm(t_m[i], inv[i]) for i in idx]
        blk *= 2
    u = [mm(inv[i], w_rhs[i]) for i in idx]
    uv = [jnp.concatenate([u[i], v[i]], axis=0) for i in idx]
    y = [r_s[i] + _dot(jnp.concatenate([a_rb[i], a_rk[i]], axis=1).astype(BF16),
                       jnp.concatenate([split(u[i]), split(v[i])], axis=0)) for i in idx]

    rr = lax.broadcasted_iota(I32, (LANES, LANES), 0) // RW_DH
    cc = lax.broadcasted_iota(I32, (LANES, LANES), 1) // RW_DH
    s_new = []
    for i in idx:
        to_end = jnp.exp(logp_end[i] - logp[i])
        bk_end = jnp.concatenate([bh[i] * to_end, kd[i] * to_end], axis=0)
        s_new.append(jnp.where(
            rr == cc, states[i] * jnp.exp(logp_end[i]) + _dot1(uv[i], bk_end, TN), 0.0))
    return y, s_new


def _rw_scan_kernel(f_ref, b_ref, yf_ref, yb_ref, s_ref):
    j = pl.program_id(2)

    @pl.when(j == 0)
    def _():
        s_ref[...] = jnp.zeros_like(s_ref)

    probs = [(bi, p) for bi in range(f_ref.shape[0]) for p in range(f_ref.shape[1])]
    blks, states, lowers = [], [], []
    for bi, p in probs:
        fb = f_ref[bi, p]
        bb = b_ref[bi, p]
        blks += [fb[:, 0:6 * LANES],
                 jnp.concatenate([bb[:, 0:3 * LANES], bb[:, 6 * LANES:9 * LANES]], axis=1)]
        states += [s_ref[bi, p, 0], s_ref[bi, p, 1]]
        lowers += [True, False]
    y, s_new = _rw_chunks(blks, states, lowers)
    for q, (bi, p) in enumerate(probs):
        yf_ref[bi, p] = y[2 * q]
        yb_ref[bi, p] = y[2 * q + 1]
        s_ref[bi, p, 0] = s_new[2 * q]
        s_ref[bi, p, 1] = s_new[2 * q + 1]


def _rw_scan_call(scan_in, B, t_ctx, t_lat):
    L = RW_CHUNK
    nct, nlt = t_ctx // L, t_lat // L
    ntot = nct + nlt
    fchunk = lambda j: j
    bchunk = lambda j: _seq_chunk(1, j, nct, ntot)
    pps = RW_PAIRS_PER_STEP
    bps = RW_BATCH_PER_STEP if B % RW_BATCH_PER_STEP == 0 else 1
    in_blk = (bps, pps, L, 9 * LANES)
    out_blk = (bps, pps, L, LANES)
    out_shape = jax.ShapeDtypeStruct((B, RW_PAIRS, t_lat, LANES), F32)
    return pl.pallas_call(
        _rw_scan_kernel,
        grid=(B // bps, RW_PAIRS // pps, ntot),
        in_specs=[pl.BlockSpec(in_blk, lambda b, p, j: (b, p, fchunk(j), 0)),
                  pl.BlockSpec(in_blk, lambda b, p, j: (b, p, bchunk(j), 0))],
        out_specs=[
            pl.BlockSpec(out_blk, lambda b, p, j: (b, p, jnp.maximum(fchunk(j) - nct, 0), 0)),
            pl.BlockSpec(out_blk,
                         lambda b, p, j: (b, p, jnp.where(j < nct, nlt - 1, bchunk(j) - nct), 0)),
        ],
        out_shape=[out_shape, out_shape],
        scratch_shapes=[pltpu.VMEM((bps, pps, 2, LANES, LANES), F32)],
        compiler_params=_cparams(3),
        name="rw_scan",
    )(scan_in, scan_in)


def _mix_out_kernel(x_ref, h_ref, zo_ref, yf_ref, yb_ref, gb_ref, g1_ref, sh2_ref, sc2_ref,
                    mlg_ref, lng_ref, lnb_ref, n2g_ref, wo_ref, rw_ref,
                    h1_ref, hn_ref, aff_ref):
    hm = h_ref[0, 0] + h_ref[1, 0]
    parts = []
    for h in range(ML_HEADS):
        hh = hm[:, h * ML_DH:(h + 1) * ML_DH]
        parts.append(hh * lax.rsqrt(jnp.mean(hh * hh, axis=-1, keepdims=True) + EPS))
    ml = jnp.concatenate(parts, axis=1) * mlg_ref[...] * jax.nn.sigmoid(zo_ref[0])

    nr = TM // GRID_W
    r0 = (pl.program_id(1) % (yf_ref.shape[3] // nr)) * nr
    y = jnp.concatenate(
        [jnp.concatenate([yf_ref[0, p, :, r0 + jr, :] + yb_ref[0, p, :, r0 + jr, :]
                          for p in range(RW_PAIRS)], axis=1) for jr in range(nr)], axis=0)
    gbr = jnp.concatenate([gb_ref[0, :, r0 + jr, :] for jr in range(nr)], axis=0)
    hr = lax.broadcasted_iota(I32, (RW_W, RW_W), 0) // RW_DH
    hc = lax.broadcasted_iota(I32, (RW_W, RW_W), 1) // RW_DH
    head_ones = (hr == hc).astype(BF16)
    mean = _dot_rsel(y, head_ones) * (1.0 / RW_DH)
    dy = y - mean
    var = _dot_rsel(dy * dy, head_ones) * (1.0 / RW_DH)
    rw = dy * lax.rsqrt(var + RW_LN_EPS) * lng_ref[...] + lnb_ref[...]
    rw = (rw + gbr[:, RW_W:2 * RW_W]) * gbr[:, 0:RW_W]

    mix = _dot(jnp.concatenate([ml, rw], axis=1).astype(BF16), wo_ref[...])
    h1 = x_ref[0] + g1_ref[0] * mix
    h1_ref[0] = h1
    hn = _rmsnorm(h1, n2g_ref[...]) * (1.0 + sc2_ref[0]) + sh2_ref[0]
    hn_ref[0] = hn.astype(BF16)
    logits = _dot3(hn, rw_ref[...])
    lane = lax.broadcasted_iota(I32, logits.shape, 1)
    logits = jnp.where(lane < N_EXPERTS, logits, -jnp.inf)
    e = jnp.exp(logits - jnp.max(logits, axis=-1, keepdims=True))
    aff = e / jnp.sum(e, axis=-1, keepdims=True)
    aff_ref[0] = aff.T[0:N_EXPERTS, :]


def _mix_out_call(x, h_ml, zo, y_f, y_b, gb, mods3, ml_norm_g, ln_g, ln_b, norm2_g,
                  wo, router_pad):
    B, T, _ = x.shape
    nlt = T // TM
    rows = T // GRID_W
    nr = TM // GRID_W
    rb = SUBLANES
    assert rows % rb == 0 and rb % nr == 0
    yv = lambda y: y.reshape(B, RW_PAIRS, GRID_W, rows, LANES)
    gbv = gb.reshape(B, GRID_W, rows, 2 * RW_W)
    row1 = lambda shape: pl.BlockSpec(shape, lambda b, i: (0,) * len(shape))
    mod = lambda k: pl.BlockSpec((1, 1, D_MODEL), lambda b, i: (b, 0, k))
    return pl.pallas_call(
        _mix_out_kernel,
        grid=(B, nlt),
        in_specs=[
            pl.BlockSpec((1, TM, D_MODEL), lambda b, i: (b, i, 0)),
            pl.BlockSpec((2, 1, TM, ML_W), lambda b, i: (0, b, i, 0)),
            pl.BlockSpec((1, TM, ML_W), lambda b, i: (b, i, 0)),
            pl.BlockSpec((1, RW_PAIRS, GRID_W, rb, LANES), lambda b, i: (b, 0, 0, i * nr // rb, 0)),
            pl.BlockSpec((1, RW_PAIRS, GRID_W, rb, LANES), lambda b, i: (b, 0, 0, i * nr // rb, 0)),
            pl.BlockSpec((1, GRID_W, rb, 2 * RW_W), lambda b, i: (b, 0, i * nr // rb, 0)),
            mod(2), mod(3), mod(4),
            row1((1, ML_W)), row1((1, RW_W)), row1((1, RW_W)), row1((1, D_MODEL)),
            row1((D_MODEL, D_MODEL)), row1((D_MODEL, LANES)),
        ],
        out_specs=[
            pl.BlockSpec((1, TM, D_MODEL), lambda b, i: (b, i, 0)),
            pl.BlockSpec((1, TM, D_MODEL), lambda b, i: (b, i, 0)),
            pl.BlockSpec((1, N_EXPERTS, TM), lambda b, i: (b, 0, i)),
        ],
        out_shape=[jax.ShapeDtypeStruct((B, T, D_MODEL), F32),
                   jax.ShapeDtypeStruct((B, T, D_MODEL), BF16),
                   jax.ShapeDtypeStruct((B, N_EXPERTS, T), F32)],
        compiler_params=_cparams(2),
        name="mix_out",
    )(x, h_ml, zo, yv(y_f), yv(y_b), gbv, mods3, mods3, mods3,
      ml_norm_g, ln_g, ln_b, norm2_g, wo, router_pad)


def _route_kernel(aff_ref, pos_ref, st_ref, *, cap):
    a = aff_ref[0]
    T = a.shape[1]
    as_f32 = lambda bits: lax.bitcast_convert_type(bits, F32)

    def body(i, thr):
        cand = thr | jnp.left_shift(jnp.int32(1), 30 - i)
        cnt = jnp.sum(jnp.where(a >= as_f32(cand), 1.0, 0.0), axis=1, keepdims=True)
        return jnp.where(cnt >= cap, cand, thr)

    thr = lax.fori_loop(0, 31, body, jnp.zeros((N_EXPERTS, 1), I32))
    gt = a >= as_f32(thr + 1)
    eq = (a >= as_f32(thr)) & jnp.logical_not(gt)
    need = cap - jnp.sum(jnp.where(gt, 1.0, 0.0), axis=1, keepdims=True)
    tri = _tri(TM, False, False, BF16)

    def prefix_excl(mask):
        outs, carries = [], []
        carry = jnp.zeros((N_EXPERTS, 1), F32)
        for blk in range(T // TM):
            seg = jnp.where(mask[:, blk * TM:(blk + 1) * TM], 1.0, 0.0)
            inc = _dot(seg.astype(BF16), tri)
            outs.append(inc - seg + carry)
            carries.append(carry)
            carry = carry + jnp.sum(seg, axis=1, keepdims=True)
        return jnp.concatenate(outs, axis=1), carries + [carry]

    chosen = gt | (eq & (prefix_excl(eq)[0] < need))
    slot, block_starts = prefix_excl(chosen)
    pos_ref[0] = jnp.where(chosen, slot, -1.0).astype(I32)
    lane = lax.broadcasted_iota(I32, (N_EXPERTS, LANES), 1)
    st = jnp.zeros((N_EXPERTS, LANES), F32)
    for blk, start in enumerate(block_starts):
        st = jnp.where(lane == blk, start, st)
    st_ref[0] = st.astype(I32)


def _route_call(aff_t, cap):
    B, E, T = aff_t.shape
    assert T // TM < LANES
    return pl.pallas_call(
        functools.partial(_route_kernel, cap=cap),
        grid=(B,),
        in_specs=[pl.BlockSpec((1, E, T), lambda b: (b, 0, 0))],
        out_specs=[pl.BlockSpec((1, E, T), lambda b: (b, 0, 0)),
                   pl.BlockSpec((1, E, LANES), lambda b: (b, 0, 0))],
        out_shape=[jax.ShapeDtypeStruct((B, E, T), I32),
                   jax.ShapeDtypeStruct((B, E, LANES), I32)],
        compiler_params=_cparams(1),
        name="route",
    )(aff_t)


def _align_down(s, m):
    sh = m.bit_length() - 1
    return pl.multiple_of(lax.shift_left(lax.shift_right_logical(s, sh), sh), m)


def _gather_kernel(st_ref, pos_ref, hn_ref, o_ref, acc_ref, *, cap):
    b = pl.program_id(0)
    e = pl.program_id(1)
    nb = hn_ref.shape[1] // TM
    acc_ref[...] = jnp.zeros_like(acc_ref)
    win = lax.broadcasted_iota(I32, (GATHER_WIN, TM), 0)
    rows = []
    for k in range(nb):
        row0 = _align_down(st_ref[b, e, k], SUBLANES)
        onehot = jnp.where(pos_ref[0, :, k * TM:(k + 1) * TM] == row0 + win, 1.0, 0.0)
        acc_ref[pl.ds(row0, GATHER_WIN), :] += _dot(onehot.astype(BF16), hn_ref[0, k * TM:(k + 1) * TM, :])
        rows.append(row0)
    slot = lax.broadcasted_iota(I32, (cap, TM), 0)
    for k in range(nb):
        @pl.when(st_ref[b, e, k + 1] > rows[k] + GATHER_WIN)
        def _():
            rest = (pos_ref[0, :, k * TM:(k + 1) * TM] == slot) & (slot >= rows[k] + GATHER_WIN)
            acc_ref[0:cap, :] += _dot(jnp.where(rest, 1.0, 0.0).astype(BF16),
                                      hn_ref[0, k * TM:(k + 1) * TM, :])
    o_ref[0] = acc_ref[0:cap, :].astype(BF16)


def _gather_call(starts, pos, hn, cap):
    B, E, T = pos.shape
    grid_spec = pltpu.PrefetchScalarGridSpec(
        num_scalar_prefetch=1,
        grid=(B, E),
        in_specs=[pl.BlockSpec((1, 1, T), lambda b, e, st: (b * E + e, 0, 0)),
                  pl.BlockSpec((1, T, D_MODEL), lambda b, e, st: (b, 0, 0))],
        out_specs=pl.BlockSpec((1, cap, D_MODEL), lambda b, e, st: (e, b, 0)),
        scratch_shapes=[pltpu.VMEM((cap + GATHER_WIN, D_MODEL), F32)])
    return pl.pallas_call(
        functools.partial(_gather_kernel, cap=cap),
        grid_spec=grid_spec,
        out_shape=jax.ShapeDtypeStruct((E, B * cap, D_MODEL), BF16),
        compiler_params=_cparams(2),
        name="gather",
    )(starts, pos.reshape(B * E, 1, T), hn)


def _ffn_kernel(x_ref, w1_ref, w3_ref, w2_ref, o_ref, acc_ref):
    f = pl.program_id(2)

    @pl.when(f == 0)
    def _():
        acc_ref[...] = jnp.zeros_like(acc_ref)

    x = x_ref[0]
    h1 = _dot(x, w1_ref[0].astype(BF16))
    h3 = _dot(x, w3_ref[0].astype(BF16))
    hid = (_silu(h1) * h3).astype(BF16)
    acc_ref[...] += _dot(hid, w2_ref[0].astype(BF16))

    @pl.when(f == pl.num_programs(2) - 1)
    def _():
        o_ref[0] = acc_ref[...].astype(BF16)


def _ffn_call(xg, w1, w3, w2):
    E, M, _ = xg.shape
    tm = min(M, 2048)
    nf = D_EXPERT // FFN_TF
    return pl.pallas_call(
        _ffn_kernel,
        grid=(E, M // tm, nf),
        in_specs=[pl.BlockSpec((1, tm, D_MODEL), lambda e, m, f: (e, m, 0)),
                  pl.BlockSpec((1, D_MODEL, FFN_TF), lambda e, m, f: (e, 0, f)),
                  pl.BlockSpec((1, D_MODEL, FFN_TF), lambda e, m, f: (e, 0, f)),
                  pl.BlockSpec((1, FFN_TF, D_MODEL), lambda e, m, f: (e, f, 0))],
        out_specs=pl.BlockSpec((1, tm, D_MODEL), lambda e, m, f: (e, m, 0)),
        out_shape=jax.ShapeDtypeStruct((E, M, D_MODEL), BF16),
        scratch_shapes=[pltpu.VMEM((tm, D_MODEL), F32)],
        compiler_params=_cparams(3),
        name="ffn",
    )(xg, w1, w3, w2)


def _combine_kernel(st_ref, pos_ref, aff_ref, y_ref, h1_ref, g2_ref, fg_ref, o_ref, moe_ref, *, cap):
    b = pl.program_id(0)
    k = pl.program_id(1)
    pos = pos_ref[0]
    aff = aff_ref[0]
    win = lax.broadcasted_iota(I32, (COMBINE_WIN, TM), 0)
    ws, ys, rows = [], [], []
    for e in range(N_EXPERTS):
        row0 = jnp.minimum(_align_down(st_ref[b, e, k], BF16_ROWS), cap - COMBINE_WIN)
        row0 = pl.multiple_of(row0, BF16_ROWS)
        ws.append(jnp.where(pos[e:e + 1, :] == row0 + win, aff[e:e + 1, :], 0.0).astype(BF16))
        ys.append(y_ref[e, pl.ds(row0, COMBINE_WIN), :])
        rows.append(row0)
    moe_ref[...] = _dot(jnp.concatenate(ws, axis=0), jnp.concatenate(ys, axis=0), TN)
    slot = lax.broadcasted_iota(I32, (cap, TM), 0)
    for e in range(N_EXPERTS):
        @pl.when(st_ref[b, e, k + 1] > rows[e] + COMBINE_WIN)
        def _():
            rest = (pos[e:e + 1, :] == slot) & (slot >= rows[e] + COMBINE_WIN)
            moe_ref[...] += _dot(jnp.where(rest, aff[e:e + 1, :], 0.0).astype(BF16), y_ref[e], TN)
    h2 = h1_ref[0] + g2_ref[0] * moe_ref[...]
    o_ref[0] = _rmsnorm(h2, fg_ref[...])


def _combine_call(starts, pos, aff_t, ys, h1, mods3, final_g, cap):
    B, E, T = pos.shape
    assert cap >= COMBINE_WIN and (cap - COMBINE_WIN) % BF16_ROWS == 0
    grid_spec = pltpu.PrefetchScalarGridSpec(
        num_scalar_prefetch=1,
        grid=(B, T // TM),
        in_specs=[pl.BlockSpec((1, E, TM), lambda b, i, st: (b, 0, i)),
                  pl.BlockSpec((1, E, TM), lambda b, i, st: (b, 0, i)),
                  pl.BlockSpec((E, cap, D_MODEL), lambda b, i, st: (0, b, 0)),
                  pl.BlockSpec((1, TM, D_MODEL), lambda b, i, st: (b, i, 0)),
                  pl.BlockSpec((1, 1, D_MODEL), lambda b, i, st: (b, 0, 5)),
                  pl.BlockSpec((1, D_MODEL), lambda b, i, st: (0, 0))],
        out_specs=pl.BlockSpec((1, TM, D_MODEL), lambda b, i, st: (b, i, 0)),
        scratch_shapes=[pltpu.VMEM((TM, D_MODEL), F32)])
    return pl.pallas_call(
        functools.partial(_combine_kernel, cap=cap),
        grid_spec=grid_spec,
        out_shape=jax.ShapeDtypeStruct((B, T, D_MODEL), F32),
        compiler_params=_cparams(2),
        name="combine",
    )(starts, pos, aff_t, ys, h1, mods3, final_g)


def _pad_cols(w, n):
    return jnp.pad(w, ((0, 0), (0, n - w.shape[1])))


def _both_dirs(up):
    z = jnp.zeros_like(up[0])
    return jnp.concatenate([jnp.concatenate([up[0], z], axis=1),
                            jnp.concatenate([z, up[1]], axis=1)], axis=0)


def kernel(x, c, ctx, c_ctx, mod_w, mod_b, norm1_g, w_in, ml_conv_w, ml_conv_b, ml_gate_b, ml_norm_g,
           rw_mu, rw_w0, rw_w_up, rw_a0, rw_a_up, rw_g_up, rw_k_k, rw_k_a, rw_r_k, rw_ln_g, rw_ln_b,
           w_out, norm2_g, router_w, exp_w1, exp_w3, exp_w2, final_g):
    B, T, D = x.shape
    t_ctx = ctx.shape[1]
    assert D == D_MODEL and T % TM == 0 and t_ctx % TM == 0 and TM % (T // GRID_W) == 0
    assert mod_w.shape[0] == 1 and B < 16
    nct, nlt = t_ctx // TM, T // TM
    cap = EC_FACTOR * T // N_EXPERTS
    ml_cols = 4 * ML_W + ML_GATES

    cc = jnp.concatenate([c, c_ctx[None, :], jnp.zeros((16 - B - 1, D), F32)], axis=0)
    mods = _mod_call(cc, mod_w[0], mod_b)
    mods3 = mods.reshape(16, 1, 6 * D)
    g1n = norm1_g.reshape(1, D)

    w_ml = _pad_cols(w_in[0, :, :ml_cols], ML_N).astype(BF16)
    w_rw = _pad_cols(w_in[0, :, ml_cols:], RW_N).astype(BF16)
    qkv, gates, zo = _ml_in_call(x, ctx, mods3, g1n, w_ml, ml_conv_w[0], ml_conv_b,
                                 _pad_cols(ml_gate_b, LANES))
    h_ml = _mlstm_call(qkv, gates, B, nct, nlt)

    gup = jnp.pad(rw_g_up[0], ((0, 256 - RW_LORA_G), (0, 0)))
    scan_in, gb = _rw_in_call(
        x, ctx, mods3, g1n, w_rw,
        _pad_cols(rw_mu[0], RW_N), rw_w0[0].reshape(1, 2 * RW_W), _both_dirs(rw_w_up[0]),
        rw_a0[0].reshape(1, 2 * RW_W), _both_dirs(rw_a_up[0]), gup,
        rw_k_k, rw_k_a, rw_r_k[0].reshape(1, RW_W))
    y_f, y_b = _rw_scan_call(scan_in, B, t_ctx, T)

    h1, hn, aff_t = _mix_out_call(
        x, h_ml, zo, y_f, y_b, gb, mods3, ml_norm_g, rw_ln_g, rw_ln_b, norm2_g,
        w_out[0].astype(BF16), _pad_cols(router_w[0], LANES))

    pos, starts = _route_call(aff_t, cap)
    starts = starts[:, :, :nlt + 1]
    xg = _gather_call(starts, pos, hn, cap)
    ys = _ffn_call(xg, exp_w1[0], exp_w3[0], exp_w2[0])
    return _combine_call(starts, pos, aff_t, ys, h1, mods3, final_g.reshape(1, D), cap)
```

```python
import functools

import jax
import jax.numpy as jnp
from jax import lax
from jax.experimental import pallas as pl
from jax.experimental.pallas import tpu as pltpu

F32 = jnp.float32
BF16 = jnp.bfloat16
I32 = jnp.int32

D_MODEL = 1024
GRID_W = 64
EPS = 1e-6
ML_W = 512
ML_HEADS = 4
ML_DH = 128
ML_GATES = 16
ML_N = 4 * ML_W + 128
RW_W = 512
RW_HEADS = 8
RW_DH = 64
RW_LORA = 64
RW_LORA_G = 160
RW_LN_EPS = 64e-5
RW_COLS = 3 * RW_W + 2 * RW_LORA + 2 * RW_LORA + RW_LORA_G
RW_N = 2048
RW_PAIRS = RW_HEADS // 2
N_EXPERTS = 16
EC_FACTOR = 2
D_EXPERT = 2816

LANES = 128
SUBLANES = 8
HALO = SUBLANES
TM = 256
RW_CHUNK = 64
RW_PAIRS_PER_STEP = 4
RW_BATCH_PER_STEP = 4
RW_INV_BASE = 4
FFN_TF = 256
BF16_ROWS = 16
GATHER_WIN = 64
COMBINE_WIN = 80
VMEM_LIMIT = 56 * 1024 * 1024

NN = (((1,), (0,)), ((), ()))
NT = (((1,), (1,)), ((), ()))
TN = (((0,), (0,)), ((), ()))


def _cparams(n_axes):
    return pltpu.CompilerParams(
        dimension_semantics=("arbitrary",) * n_axes, vmem_limit_bytes=VMEM_LIMIT)


def _dot(a, b, dn=NN):
    return lax.dot_general(a, b, dn, preferred_element_type=F32)


def _split2(a):
    hi = a.astype(BF16)
    lo = (a - hi.astype(F32)).astype(BF16)
    return hi, lo


def _split3(a):
    hi = a.astype(BF16)
    r1 = a - hi.astype(F32)
    mid = r1.astype(BF16)
    lo = (r1 - mid.astype(F32)).astype(BF16)
    return hi, mid, lo


def _dot1(a, b, dn=NN):
    return _dot(a.astype(BF16), b.astype(BF16), dn)


def _dot3(a, b, dn=NN):
    ah, al = _split2(a)
    bh, bl = _split2(b)
    return _dot(ah, bh, dn) + (_dot(ah, bl, dn) + _dot(al, bh, dn))


def _dot_sel(sel_bf16, b, dn=NN):
    bh, bm, bl = _split3(b)
    return _dot(sel_bf16, bh, dn) + (_dot(sel_bf16, bm, dn) + _dot(sel_bf16, bl, dn))


def _dot_rsel(a, sel_bf16, dn=NN):
    ah, al = _split2(a)
    return _dot(ah, sel_bf16, dn) + _dot(al, sel_bf16, dn)


def _silu(x):
    return x * jax.nn.sigmoid(x)


def _log_sigmoid(x):
    return jnp.minimum(x, 0.0) - jnp.log1p(jnp.exp(-jnp.abs(x)))


def _rmsnorm(x, g):
    return x * lax.rsqrt(jnp.mean(x * x, axis=-1, keepdims=True) + EPS) * g


def _tri(n, lower, strict, dtype):
    r = lax.broadcasted_iota(I32, (n, n), 0)
    c = lax.broadcasted_iota(I32, (n, n), 1)
    if lower:
        m = (c < r) if strict else (c <= r)
    else:
        m = (c > r) if strict else (c >= r)
    return m if dtype is None else m.astype(dtype)


def _cumsum_rows(x, forward):
    n = x.shape[0]
    row = lax.broadcasted_iota(I32, x.shape, 0)
    s = 1
    while s < n:
        if forward:
            x = x + jnp.where(row >= s, pltpu.roll(x, s, 0), 0.0)
        else:
            x = x + jnp.where(row < n - s, pltpu.roll(x, n - s, 0), 0.0)
        s *= 2
    return x


def _shift_rows(z, prev_row, next_row):
    n = z.shape[0]
    row = lax.broadcasted_iota(I32, z.shape, 0)
    zm1 = jnp.where(row == 0, prev_row, pltpu.roll(z, 1, 0))
    zp1 = jnp.where(row == n - 1, next_row, pltpu.roll(z, n - 1, 0))
    return zm1, zp1


def _mod_kernel(c_ref, w_ref, b_ref, o_ref):
    o_ref[...] = _dot3(_silu(c_ref[...]), w_ref[...]) + b_ref[...]


def _mod_call(cc, mod_w, mod_b):
    n = mod_w.shape[1]
    tn = 1024
    return pl.pallas_call(
        _mod_kernel,
        grid=(n // tn,),
        in_specs=[pl.BlockSpec((16, D_MODEL), lambda j: (0, 0)),
                  pl.BlockSpec((D_MODEL, tn), lambda j: (0, j)),
                  pl.BlockSpec((1, tn), lambda j: (0, j))],
        out_specs=pl.BlockSpec((16, tn), lambda j: (0, j)),
        out_shape=jax.ShapeDtypeStruct((16, n), F32),
        compiler_params=_cparams(1),
        name="mod",
    )(cc, mod_w, mod_b)


def _project_tile(xin, prev_rows, next_rows, sh, sc, g, w_ref):
    xe = jnp.concatenate([prev_rows, xin, next_rows], axis=0)
    xn = _rmsnorm(xe, g) * (1.0 + sc) + sh
    return _dot(xn.astype(BF16), w_ref[...])


def _tile_and_edges(u, seg_first, seg_last):
    z = u[HALO:HALO + TM]
    prev_row = jnp.where(seg_first, 0.0, u[HALO - 1:HALO])
    next_row = jnp.where(seg_last, 0.0, u[HALO + TM:HALO + TM + 1])
    return z, prev_row, next_row


def _ml_in_kernel(ctx_ref, x_ref, xp_ref, xn_ref, sh_ref, sc_ref, g_ref, w_ref, cw_ref, cb_ref, gb_ref,
                  qkv_ref, gate_ref, zo_ref, *, nct, ntot):
    i = pl.program_id(1)
    xin = jnp.where(i < nct, ctx_ref[0], x_ref[0])
    u = _project_tile(xin, xp_ref[0], xn_ref[0], sh_ref[0], sc_ref[0], g_ref[...], w_ref)
    z, prev_row, next_row = _tile_and_edges(
        u, (i == 0) | (i == nct), (i == nct - 1) | (i == ntot - 1))
    zqk = z[:, 0:2 * ML_W]
    zm1, zp1 = _shift_rows(zqk, prev_row[:, 0:2 * ML_W], next_row[:, 0:2 * ML_W])
    cw = cw_ref[...]
    qk = _silu(zm1 * cw[0:1] + zqk * cw[1:2] + zp1 * cw[2:3] + cb_ref[...])
    qkv_ref[0, :, 0:ML_W] = qk[:, 0:ML_W].astype(BF16)
    qkv_ref[0, :, ML_W:2 * ML_W] = (qk[:, ML_W:2 * ML_W] * (ML_DH ** -0.5)).astype(BF16)
    qkv_ref[0, :, 2 * ML_W:3 * ML_W] = z[:, 2 * ML_W:3 * ML_W].astype(BF16)
    zo_ref[0] = z[:, 3 * ML_W:4 * ML_W]
    g = z[:, 4 * ML_W:4 * ML_W + LANES] + gb_ref[...]
    lane = lax.broadcasted_iota(I32, (TM, LANES), 1)
    is_f = ((lane >= 4) & (lane < 8)) | ((lane >= 12) & (lane < 16))
    gate_ref[0] = jnp.where(is_f, _log_sigmoid(g), g)


def _ml_in_call(x, ctx, mods3, norm_g, w, conv_w, conv_b, gate_b):
    B, T, _ = x.shape
    assert ctx.shape[1] == TM
    nct, nlt = 1, T // TM
    ntot = nct + nlt
    hb = TM // HALO
    lat = lambda i: jnp.maximum(i - nct, 0)
    mrow = lambda b, i: jnp.where(i < nct, B, b)
    full = lambda shape: pl.BlockSpec(shape, lambda b, i: (0,) * len(shape))
    return pl.pallas_call(
        functools.partial(_ml_in_kernel, nct=nct, ntot=ntot),
        grid=(B, ntot),
        in_specs=[
            pl.BlockSpec((1, TM, D_MODEL), lambda b, i: (b, 0, 0)),
            pl.BlockSpec((1, TM, D_MODEL), lambda b, i: (b, lat(i), 0)),
            pl.BlockSpec((1, HALO, D_MODEL), lambda b, i: (b, jnp.maximum(lat(i) * hb - 1, 0), 0)),
            pl.BlockSpec((1, HALO, D_MODEL),
                         lambda b, i: (b, jnp.minimum((lat(i) + 1) * hb, nlt * hb - 1), 0)),
            pl.BlockSpec((1, 1, D_MODEL), lambda b, i: (mrow(b, i), 0, 0)),
            pl.BlockSpec((1, 1, D_MODEL), lambda b, i: (mrow(b, i), 0, 1)),
            full((1, D_MODEL)), full((D_MODEL, ML_N)),
            full((3, 2 * ML_W)), full((1, 2 * ML_W)), full((1, LANES)),
        ],
        out_specs=[
            pl.BlockSpec((1, TM, 3 * ML_W), lambda b, i: (b, i, 0)),
            pl.BlockSpec((1, TM, LANES), lambda b, i: (b, i, 0)),
            pl.BlockSpec((1, TM, ML_W), lambda b, i: (b, lat(i), 0)),
        ],
        out_shape=[jax.ShapeDtypeStruct((B, ntot * TM, 3 * ML_W), BF16),
                   jax.ShapeDtypeStruct((B, ntot * TM, LANES), F32),
                   jax.ShapeDtypeStruct((B, nlt * TM, ML_W), F32)],
        compiler_params=_cparams(2),
        name="ml_in",
    )(ctx, x, x, x, mods3, mods3, norm_g, w, conv_w, conv_b, gate_b)


def _seq_chunk(d, j, nct, ntot):
    bwd = jnp.where(j < nct, nct - 1 - j, ntot - 1 - (j - nct))
    return jnp.where(d == 0, j, bwd)


def _mlstm_kernel(qkv_ref, gate_ref, o_ref, c_ref, n_ref, m_ref):
    d = pl.program_id(1)
    j = pl.program_id(2)
    L = TM

    @pl.when(j == 0)
    def _():
        c_ref[...] = jnp.zeros_like(c_ref)
        n_ref[...] = jnp.zeros_like(n_ref)
        m_ref[...] = jnp.zeros_like(m_ref)

    q_all = gate_ref[0]
    lane = lax.broadcasted_iota(I32, (L, LANES), 1)
    causal = jnp.where(d == 0, _tri(L, True, False, F32), _tri(L, False, False, F32))
    b_all = _dot_sel(causal.astype(BF16), q_all)
    q_t = q_all.T
    b_t = b_all.T
    keep = causal > 0.5

    def lane_pick(a, idx):
        return jnp.sum(jnp.where(lane == idx, a, 0.0), axis=-1, keepdims=True)

    def row_pick(a_t, h):
        return jnp.where(d == 0, a_t[h:h + 1, :], a_t[8 + h:9 + h, :])

    hs = range(ML_HEADS)
    sl = [slice(h * ML_DH, (h + 1) * ML_DH) for h in hs]
    q = [qkv_ref[0, :, sl[h]] for h in hs]
    kb = [qkv_ref[0, :, ML_W + h * ML_DH:ML_W + (h + 1) * ML_DH] for h in hs]
    vb = [qkv_ref[0, :, 2 * ML_W + h * ML_DH:2 * ML_W + (h + 1) * ML_DH] for h in hs]
    qf = [q[h].astype(F32) for h in hs]
    k = [kb[h].astype(F32) for h in hs]
    v = [vb[h].astype(F32) for h in hs]
    li_col = [lane_pick(q_all, h + 8 * d) for h in hs]
    b_col = [lane_pick(b_all, 4 + h + 8 * d) for h in hs]
    li_row = [row_pick(q_t, h) for h in hs]
    b_row = [row_pick(b_t, 4 + h) for h in hs]
    m_prev = [m_ref[h, 0:1, 0:1] for h in hs]
    C = [c_ref[h] for h in hs]
    n_row = [n_ref[h, 0:1, :] for h in hs]

    qk_s = [_dot(q[h], kb[h], NT) for h in hs]
    q_c = [_dot(q[h], C[h].astype(BF16), NT) for h in hs]
    dm = [jnp.where(keep, b_col[h] - b_row[h] + li_row[h], -jnp.inf) for h in hs]
    inter = [b_col[h] + m_prev[h] for h in hs]
    m_t = [jnp.maximum(inter[h], jnp.max(dm[h], axis=-1, keepdims=True)) for h in hs]
    wts = [jnp.exp(dm[h] - m_t[h]) * qk_s[h] for h in hs]
    w_inter = [jnp.exp(inter[h] - m_t[h]) for h in hs]
    num = [_dot(wts[h].astype(BF16), vb[h]) + w_inter[h] * q_c[h] for h in hs]
    for h in hs:
        den = (jnp.sum(wts[h], axis=-1, keepdims=True)
               + w_inter[h] * jnp.sum(qf[h] * n_row[h], axis=-1, keepdims=True))
        o_ref[0, 0, :, sl[h]] = num[h] / jnp.maximum(jnp.abs(den), jnp.exp(-m_t[h]))

    b_end = [jnp.where(d == 0, b_col[h][L - 1:L, :], b_col[h][0:1, :]) for h in hs]
    dec = [b_end[h] - b_col[h] + li_col[h] for h in hs]
    m_new = [jnp.maximum(b_end[h] + m_prev[h], jnp.max(dec[h], axis=0, keepdims=True)) for h in hs]
    wk = [jnp.exp(dec[h] - m_new[h]) for h in hs]
    s_old = [jnp.exp(b_end[h] + m_prev[h] - m_new[h]) for h in hs]
    c_upd = [_dot((v[h] * wk[h]).astype(BF16), kb[h], TN) for h in hs]
    for h in hs:
        c_ref[h] = s_old[h] * C[h] + c_upd[h]
        n_ref[h] = jnp.broadcast_to(
            s_old[h] * n_row[h] + jnp.sum(wk[h] * k[h], axis=0, keepdims=True), (SUBLANES, ML_DH))
        m_ref[h] = jnp.broadcast_to(m_new[h], (SUBLANES, LANES))


def _mlstm_call(qkv, gates, B, nct, nlt):
    ntot = nct + nlt
    chunk = lambda d, j: _seq_chunk(d, j, nct, ntot)
    out_blk = lambda d, j: jnp.where(j < nct, d * (nlt - 1), chunk(d, j) - nct)
    return pl.pallas_call(
        _mlstm_kernel,
        grid=(B, 2, ntot),
        in_specs=[
            pl.BlockSpec((1, TM, 3 * ML_W), lambda b, d, j: (b, chunk(d, j), 0)),
            pl.BlockSpec((1, TM, LANES), lambda b, d, j: (b, chunk(d, j), 0)),
        ],
        out_specs=pl.BlockSpec((1, 1, TM, ML_W), lambda b, d, j: (d, b, out_blk(d, j), 0)),
        out_shape=jax.ShapeDtypeStruct((2, B, nlt * TM, ML_W), F32),
        scratch_shapes=[pltpu.VMEM((ML_HEADS, ML_DH, ML_DH), F32),
                        pltpu.VMEM((ML_HEADS, SUBLANES, ML_DH), F32),
                        pltpu.VMEM((ML_HEADS, SUBLANES, LANES), F32)],
        compiler_params=_cparams(3),
        name="mlstm",
    )(qkv, gates)


def _rw_in_kernel(ctx_ref, x_ref, xp_ref, xn_ref, sh_ref, sc_ref, g_ref, w_ref,
                  mu_ref, w0_ref, wup_ref, a0_ref, aup_ref, gup_ref, kk_ref, ka_ref, rk_ref,
                  scan_ref, gb_ref, *, nct, ntot, rows):
    i = pl.program_id(1)
    jt = jnp.maximum(i - nct, 0)
    cpt = TM // rows
    tpb = x_ref.shape[2] // cpt
    c0 = (jt % tpb) * cpt if tpb > 1 else 0
    xcm = jnp.concatenate([x_ref[0, :, c0 + c, :] for c in range(cpt)], axis=0)
    xin = jnp.where(i < nct, ctx_ref[0], xcm)
    prev1 = xp_ref[0, SUBLANES - 1:SUBLANES, jnp.maximum(jt * cpt - 1, 0) % SUBLANES, :]
    next1 = xn_ref[0, 0:1, jnp.minimum((jt + 1) * cpt, GRID_W - 1) % SUBLANES, :]
    u = _project_tile(xin, jnp.broadcast_to(prev1, (HALO, D_MODEL)),
                      jnp.broadcast_to(next1, (HALO, D_MODEL)),
                      sh_ref[0], sc_ref[0], g_ref[...], w_ref)
    z, prev_row, next_row = _tile_and_edges(
        u, (i == 0) | (i == nct), (i == nct - 1) | (i == ntot - 1))
    zm1, zp1 = _shift_rows(z, prev_row, next_row)
    mu = mu_ref[...]
    xs = z * (1.0 - mu[0:1] - mu[1:2]) + mu[0:1] * zm1 + mu[1:2] * zp1

    r = xs[:, 0:RW_W]
    k = xs[:, RW_W:2 * RW_W]
    v = xs[:, 2 * RW_W:3 * RW_W]
    wd = xs[:, 3 * RW_W:3 * RW_W + 2 * RW_LORA]
    ad = xs[:, 3 * RW_W + 2 * RW_LORA:3 * RW_W + 4 * RW_LORA]
    gd = xs[:, 3 * RW_W + 4 * RW_LORA:RW_N]

    w_pre = w0_ref[...] + _dot1(jnp.tanh(wd), wup_ref[...])
    log_w = -jax.nn.softplus(-w_pre) - 0.5
    lw = -jnp.exp(log_w)
    a = jax.nn.sigmoid(a0_ref[...] + _dot1(ad, aup_ref[...]))
    g = _dot1(jax.nn.sigmoid(gd), gup_ref[...])

    hr = lax.broadcasted_iota(I32, (RW_W, RW_W), 0) // RW_DH
    hc = lax.broadcasted_iota(I32, (RW_W, RW_W), 1) // RW_DH
    head_ones = (hr == hc).astype(BF16)
    kk = k * kk_ref[...]
    ss = _dot_rsel(kk * kk, head_ones)
    kk = kk / jnp.maximum(jnp.sqrt(ss), 1e-12)
    ka = ka_ref[...]
    k_f = k * (1.0 + (a[:, 0:RW_W] - 1.0) * ka)
    k_b = k * (1.0 + (a[:, RW_W:2 * RW_W] - 1.0) * ka)
    rk = rk_ref[...]
    bonus = _dot_rsel(r * (k_f + k_b) * rk, head_ones) * v

    cols = (r, v, kk, lw[:, 0:RW_W], k_f, kk * a[:, 0:RW_W],
            lw[:, RW_W:2 * RW_W], k_b, kk * a[:, RW_W:2 * RW_W])
    for p in range(RW_PAIRS):
        for qi, arr in enumerate(cols):
            scan_ref[0, p, :, qi * LANES:(qi + 1) * LANES] = arr[:, p * LANES:(p + 1) * LANES]
    gb_ref[0, :, 0:RW_W] = g
    gb_ref[0, :, RW_W:2 * RW_W] = bonus


def _rw_in_call(x, ctx, mods3, norm_g, w, mu, w0, wup, a0, aup, gup, k_k, k_a, r_k):
    B, T, _ = x.shape
    assert ctx.shape[1] == TM
    nct, nlt = 1, T // TM
    ntot = nct + nlt
    rows = T // GRID_W
    cpt = TM // rows
    cb = max(SUBLANES, cpt)
    tpb = cb // cpt
    assert rows % SUBLANES == 0 and GRID_W % cb == 0 and cb % cpt == 0
    lat = lambda c: jnp.maximum(c - nct, 0)
    mrow = lambda b, c: jnp.where(c < nct, B, b)
    full = lambda shape: pl.BlockSpec(shape, lambda b, c: (0,) * len(shape))
    x4 = x.reshape(B, rows, GRID_W, D_MODEL)
    return pl.pallas_call(
        functools.partial(_rw_in_kernel, nct=nct, ntot=ntot, rows=rows),
        grid=(B, ntot),
        in_specs=[
            pl.BlockSpec((1, TM, D_MODEL), lambda b, c: (b, 0, 0)),
            pl.BlockSpec((1, rows, cb, D_MODEL), lambda b, c: (b, 0, lat(c) // tpb, 0)),
            pl.BlockSpec((1, SUBLANES, SUBLANES, D_MODEL),
                         lambda b, c: (b, rows // SUBLANES - 1,
                                       jnp.maximum(lat(c) * cpt - 1, 0) // SUBLANES, 0)),
            pl.BlockSpec((1, SUBLANES, SUBLANES, D_MODEL),
                         lambda b, c: (b, 0, jnp.minimum((lat(c) + 1) * cpt, GRID_W - 1) // SUBLANES, 0)),
            pl.BlockSpec((1, 1, D_MODEL), lambda b, c: (mrow(b, c), 0, 0)),
            pl.BlockSpec((1, 1, D_MODEL), lambda b, c: (mrow(b, c), 0, 1)),
            full((1, D_MODEL)), full((D_MODEL, RW_N)),
            full((2, RW_N)), full((1, 2 * RW_W)), full((2 * RW_LORA, 2 * RW_W)),
            full((1, 2 * RW_W)), full((2 * RW_LORA, 2 * RW_W)), full((256, RW_W)),
            full((1, RW_W)), full((1, RW_W)), full((1, RW_W)),
        ],
        out_specs=[
            pl.BlockSpec((1, RW_PAIRS, TM, 9 * LANES), lambda b, c: (b, 0, c, 0)),
            pl.BlockSpec((1, TM, 2 * RW_W), lambda b, c: (b, jnp.maximum(c - nct, 0), 0)),
        ],
        out_shape=[jax.ShapeDtypeStruct((B, RW_PAIRS, ntot * TM, 9 * LANES), F32),
                   jax.ShapeDtypeStruct((B, nlt * TM, 2 * RW_W), F32)],
        compiler_params=_cparams(2),
        name="rw_in",
    )(ctx, x4, x4, x4, mods3, mods3, norm_g, w, mu, w0, wup, a0, aup, gup, k_k, k_a, r_k)


def _rw_chunks(blks, states, lowers):
    n = len(blks)
    L = blks[0].shape[0]
    assert L == RW_DH
    idx = range(n)
    col = lambda i, c: blks[i][:, c * LANES:(c + 1) * LANES]
    r, v, kk, lw, kd, bh = ([col(i, c) for i in idx] for c in range(6))
    logp = [_cumsum_rows(lw[i], lowers[i]) for i in idx]
    logp_end = [logp[i][L - 1:L, :] if lowers[i] else logp[i][0:1, :] for i in idx]
    p_inv = [jnp.exp(-logp[i]) for i in idx]
    a_t = [-kk[i] * jnp.exp(logp[i] - lw[i]) for i in idx]
    r_t = [r[i] * jnp.exp(logp[i]) for i in idx]

    row = lax.broadcasted_iota(I32, (L, LANES), 0)
    lane = lax.broadcasted_iota(I32, (L, LANES), 1)
    head0 = lane < RW_DH
    src = jnp.where(head0, lane, lane - RW_DH)

    def split(y_pair):
        return jnp.concatenate([jnp.where(head0, y_pair, 0.0), jnp.where(head0, 0.0, y_pair)],
                               axis=0).astype(BF16)

    def mm(x_cat, y_pair):
        return _dot(x_cat.astype(BF16), split(y_pair))

    strict = {lo: (src < row) if lo else (src > row) for lo in set(lowers)}
    incl_c = {lo: (src <= row) if lo else (src >= row) for lo in set(lowers)}
    eye = jnp.where(src == row, 1.0, 0.0)

    a_s = [_dot1(a_t[i], states[i], NT) for i in idx]
    r_s = [_dot1(r_t[i], states[i], NT) for i in idx]
    bk = [jnp.concatenate([split(bh[i] * p_inv[i]), split(kd[i] * p_inv[i])], axis=0) for i in idx]
    g_ar = [_dot(jnp.concatenate([a_t[i], r_t[i]], axis=0).astype(BF16), bk[i], NT) for i in idx]
    g_a = [g_ar[i][0:L] for i in idx]
    g_r = [g_ar[i][L:2 * L] for i in idx]
    n_mat = [jnp.where(strict[lowers[i]], g_a[i][:, 0:LANES], 0.0) for i in idx]
    a_ak = [jnp.where(strict[lowers[i]], g_a[i][:, LANES:2 * LANES], 0.0) for i in idx]
    a_rb = [jnp.where(incl_c[lowers[i]], g_r[i][:, 0:LANES], 0.0) for i in idx]
    a_rk = [jnp.where(incl_c[lowers[i]], g_r[i][:, LANES:2 * LANES], 0.0) for i in idx]
    w_rhs = [a_s[i] + mm(a_ak[i], v[i]) for i in idx]
    base = RW_INV_BASE
    n_d = [jnp.where(row // base == src // base, n_mat[i], 0.0) for i in idx]
    inv = [eye + n_d[i] for i in idx]
    n_d2 = [mm(n_d[i], n_d[i]) for i in idx]
    inv = [inv[i] + mm(inv[i], n_d2[i]) for i in idx]
    blk = base
    while blk < L:
        link = (row // (2 * blk) == src // (2 * blk)) & (row // blk != src // blk)
        t_m = [mm(inv[i], jnp.where(link, n_mat[i], 0.0)) for i in idx]
        inv = [inv[i] + mm(t_m[i], inv[i]) for i in idx]
        blk *= 2
    u = [mm(inv[i], w_rhs[i]) for i in idx]
    uv = [jnp.concatenate([u[i], v[i]], axis=0) for i in idx]
    y = [r_s[i] + _dot(jnp.concatenate([a_rb[i], a_rk[i]], axis=1).astype(BF16),
                       jnp.concatenate([split(u[i]), split(v[i])], axis=0)) for i in idx]

    rr = lax.broadcasted_iota(I32, (LANES, LANES), 0) // RW_DH
    cc = lax.broadcasted_iota(I32, (LANES, LANES), 1) // RW_DH
    s_new = []
    for i in idx:
        to_end = jnp.exp(logp_end[i] - logp[i])
        bk_end = jnp.concatenate([bh[i] * to_end, kd[i] * to_end], axis=0)
        s_new.append(jnp.where(
            rr == cc, states[i] * jnp.exp(logp_end[i]) + _dot1(uv[i], bk_end, TN), 0.0))
    return y, s_new


def _rw_scan_kernel(f_ref, b_ref, yf_ref, yb_ref, s_ref):
    j = pl.program_id(2)

    @pl.when(j == 0)
    def _():
        s_ref[...] = jnp.zeros_like(s_ref)

    probs = [(bi, p) for bi in range(f_ref.shape[0]) for p in range(f_ref.shape[1])]
    blks, states, lowers = [], [], []
    for bi, p in probs:
        fb = f_ref[bi, p]
        bb = b_ref[bi, p]
        blks += [fb[:, 0:6 * LANES],
                 jnp.concatenate([bb[:, 0:3 * LANES], bb[:, 6 * LANES:9 * LANES]], axis=1)]
        states += [s_ref[bi, p, 0], s_ref[bi, p, 1]]
        lowers += [True, False]
    y, s_new = _rw_chunks(blks, states, lowers)
    for q, (bi, p) in enumerate(probs):
        yf_ref[bi, p] = y[2 * q]
        yb_ref[bi, p] = y[2 * q + 1]
        s_ref[bi, p, 0] = s_new[2 * q]
        s_ref[bi, p, 1] = s_new[2 * q + 1]


def _rw_scan_call(scan_in, B, t_ctx, t_lat):
    L = RW_CHUNK
    nct, nlt = t_ctx // L, t_lat // L
    ntot = nct + nlt
    fchunk = lambda j: j
    bchunk = lambda j: _seq_chunk(1, j, nct, ntot)
    pps = RW_PAIRS_PER_STEP
    bps = RW_BATCH_PER_STEP if B % RW_BATCH_PER_STEP == 0 else 1
    in_blk = (bps, pps, L, 9 * LANES)
    out_blk = (bps, pps, L, LANES)
    out_shape = jax.ShapeDtypeStruct((B, RW_PAIRS, t_lat, LANES), F32)
    return pl.pallas_call(
        _rw_scan_kernel,
        grid=(B // bps, RW_PAIRS // pps, ntot),
        in_specs=[pl.BlockSpec(in_blk, lambda b, p, j: (b, p, fchunk(j), 0)),
                  pl.BlockSpec(in_blk, lambda b, p, j: (b, p, bchunk(j), 0))],
        out_specs=[
            pl.BlockSpec(out_blk, lambda b, p, j: (b, p, jnp.maximum(fchunk(j) - nct, 0), 0)),
            pl.BlockSpec(out_blk,
                         lambda b, p, j: (b, p, jnp.where(j < nct, nlt - 1, bchunk(j) - nct), 0)),
        ],
        out_shape=[out_shape, out_shape],
        scratch_shapes=[pltpu.VMEM((bps, pps, 2, LANES, LANES), F32)],
        compiler_params=_cparams(3),
        name="rw_scan",
    )(scan_in, scan_in)


def _mix_out_kernel(x_ref, h_ref, zo_ref, yf_ref, yb_ref, gb_ref, g1_ref, sh2_ref, sc2_ref,
                    mlg_ref, lng_ref, lnb_ref, n2g_ref, wo_ref, rw_ref,
                    h1_ref, hn_ref, aff_ref):
    hm = h_ref[0, 0] + h_ref[1, 0]
    parts = []
    for h in range(ML_HEADS):
        hh = hm[:, h * ML_DH:(h + 1) * ML_DH]
        parts.append(hh * lax.rsqrt(jnp.mean(hh * hh, axis=-1, keepdims=True) + EPS))
    ml = jnp.concatenate(parts, axis=1) * mlg_ref[...] * jax.nn.sigmoid(zo_ref[0])

    nr = yf_ref.shape[3]
    to_raster = lambda a: jnp.swapaxes(a, 0, 1).reshape(nr * GRID_W, a.shape[-1])
    y = jnp.concatenate([to_raster(yf_ref[0, p] + yb_ref[0, p]) for p in range(RW_PAIRS)], axis=1)
    gbr = to_raster(gb_ref[0])
    hr = lax.broadcasted_iota(I32, (RW_W, RW_W), 0) // RW_DH
    hc = lax.broadcasted_iota(I32, (RW_W, RW_W), 1) // RW_DH
    head_ones = (hr == hc).astype(BF16)
    mean = _dot_rsel(y, head_ones) * (1.0 / RW_DH)
    dy = y - mean
    var = _dot_rsel(dy * dy, head_ones) * (1.0 / RW_DH)
    rw = dy * lax.rsqrt(var + RW_LN_EPS) * lng_ref[...] + lnb_ref[...]
    rw = (rw + gbr[:, RW_W:2 * RW_W]) * gbr[:, 0:RW_W]

    mix = _dot(jnp.concatenate([ml, rw], axis=1).astype(BF16), wo_ref[...])
    h1 = x_ref[0] + g1_ref[0] * mix
    h1_ref[0] = h1
    hn = _rmsnorm(h1, n2g_ref[...]) * (1.0 + sc2_ref[0]) + sh2_ref[0]
    hn_ref[0] = hn.astype(BF16)
    logits = _dot3(hn, rw_ref[...])
    lane = lax.broadcasted_iota(I32, logits.shape, 1)
    logits = jnp.where(lane < N_EXPERTS, logits, -jnp.inf)
    e = jnp.exp(logits - jnp.max(logits, axis=-1, keepdims=True))
    aff = e / jnp.sum(e, axis=-1, keepdims=True)
    aff_ref[0] = aff.T[0:N_EXPERTS, :]


def _mix_out_call(x, h_ml, zo, y_f, y_b, gb, mods3, ml_norm_g, ln_g, ln_b, norm2_g,
                  wo, router_pad):
    B, T, _ = x.shape
    rows = T // GRID_W
    rb = SUBLANES
    tmo = rb * GRID_W
    assert rows % rb == 0
    yv = lambda y: y.reshape(B, RW_PAIRS, GRID_W, rows, LANES)
    gbv = gb.reshape(B, GRID_W, rows, 2 * RW_W)
    row1 = lambda shape: pl.BlockSpec(shape, lambda b, i: (0,) * len(shape))
    mod = lambda k: pl.BlockSpec((1, 1, D_MODEL), lambda b, i: (b, 0, k))
    return pl.pallas_call(
        _mix_out_kernel,
        grid=(B, T // tmo),
        in_specs=[
            pl.BlockSpec((1, tmo, D_MODEL), lambda b, i: (b, i, 0)),
            pl.BlockSpec((2, 1, tmo, ML_W), lambda b, i: (0, b, i, 0)),
            pl.BlockSpec((1, tmo, ML_W), lambda b, i: (b, i, 0)),
            pl.BlockSpec((1, RW_PAIRS, GRID_W, rb, LANES), lambda b, i: (b, 0, 0, i, 0)),
            pl.BlockSpec((1, RW_PAIRS, GRID_W, rb, LANES), lambda b, i: (b, 0, 0, i, 0)),
            pl.BlockSpec((1, GRID_W, rb, 2 * RW_W), lambda b, i: (b, 0, i, 0)),
            mod(2), mod(3), mod(4),
            row1((1, ML_W)), row1((1, RW_W)), row1((1, RW_W)), row1((1, D_MODEL)),
            row1((D_MODEL, D_MODEL)), row1((D_MODEL, LANES)),
        ],
        out_specs=[
            pl.BlockSpec((1, tmo, D_MODEL), lambda b, i: (b, i, 0)),
            pl.BlockSpec((1, tmo, D_MODEL), lambda b, i: (b, i, 0)),
            pl.BlockSpec((1, N_EXPERTS, tmo), lambda b, i: (b, 0, i)),
        ],
        out_shape=[jax.ShapeDtypeStruct((B, T, D_MODEL), F32),
                   jax.ShapeDtypeStruct((B, T, D_MODEL), BF16),
                   jax.ShapeDtypeStruct((B, N_EXPERTS, T), F32)],
        compiler_params=_cparams(2),
        name="mix_out",
    )(x, h_ml, zo, yv(y_f), yv(y_b), gbv, mods3, mods3, mods3,
      ml_norm_g, ln_g, ln_b, norm2_g, wo, router_pad)


def _route_kernel(aff_ref, pos_ref, st_ref, *, cap):
    a = aff_ref[0]
    T = a.shape[1]
    as_f32 = lambda bits: lax.bitcast_convert_type(bits, F32)

    def body(i, thr):
        cand = thr | jnp.left_shift(jnp.int32(1), 30 - i)
        cnt = jnp.sum(jnp.where(a >= as_f32(cand), 1.0, 0.0), axis=1, keepdims=True)
        return jnp.where(cnt >= cap, cand, thr)

    thr = lax.fori_loop(0, 31, body, jnp.zeros((N_EXPERTS, 1), I32))
    gt = a >= as_f32(thr + 1)
    eq = (a >= as_f32(thr)) & jnp.logical_not(gt)
    need = cap - jnp.sum(jnp.where(gt, 1.0, 0.0), axis=1, keepdims=True)
    tri = _tri(TM, False, False, BF16)

    def prefix_excl(mask):
        outs, carries = [], []
        carry = jnp.zeros((N_EXPERTS, 1), F32)
        for blk in range(T // TM):
            seg = jnp.where(mask[:, blk * TM:(blk + 1) * TM], 1.0, 0.0)
            inc = _dot(seg.astype(BF16), tri)
            outs.append(inc - seg + carry)
            carries.append(carry)
            carry = carry + jnp.sum(seg, axis=1, keepdims=True)
        return jnp.concatenate(outs, axis=1), carries + [carry]

    chosen = gt | (eq & (prefix_excl(eq)[0] < need))
    slot, block_starts = prefix_excl(chosen)
    pos_ref[0] = jnp.where(chosen, slot, -1.0).astype(I32)
    lane = lax.broadcasted_iota(I32, (N_EXPERTS, LANES), 1)
    st = jnp.zeros((N_EXPERTS, LANES), F32)
    for blk, start in enumerate(block_starts):
        st = jnp.where(lane == blk, start, st)
    st_ref[0] = st.astype(I32)


def _route_call(aff_t, cap):
    B, E, T = aff_t.shape
    assert T // TM < LANES
    return pl.pallas_call(
        functools.partial(_route_kernel, cap=cap),
        grid=(B,),
        in_specs=[pl.BlockSpec((1, E, T), lambda b: (b, 0, 0))],
        out_specs=[pl.BlockSpec((1, E, T), lambda b: (b, 0, 0)),
                   pl.BlockSpec((1, E, LANES), lambda b: (b, 0, 0))],
        out_shape=[jax.ShapeDtypeStruct((B, E, T), I32),
                   jax.ShapeDtypeStruct((B, E, LANES), I32)],
        compiler_params=_cparams(1),
        name="route",
    )(aff_t)


def _align_down(s, m):
    sh = m.bit_length() - 1
    return pl.multiple_of(lax.shift_left(lax.shift_right_logical(s, sh), sh), m)


def _gather_kernel(st_ref, pos_ref, hn_ref, o_ref, acc_ref, *, cap):
    b = pl.program_id(0)
    e = pl.program_id(1)
    nb = hn_ref.shape[1] // TM
    acc_ref[...] = jnp.zeros_like(acc_ref)
    win = lax.broadcasted_iota(I32, (GATHER_WIN, TM), 0)
    rows = []
    for k in range(nb):
        row0 = _align_down(st_ref[b, e, k], SUBLANES)
        onehot = jnp.where(pos_ref[0, :, k * TM:(k + 1) * TM] == row0 + win, 1.0, 0.0)
        acc_ref[pl.ds(row0, GATHER_WIN), :] += _dot(onehot.astype(BF16), hn_ref[0, k * TM:(k + 1) * TM, :])
        rows.append(row0)
    slot = lax.broadcasted_iota(I32, (cap, TM), 0)
    for k in range(nb):
        @pl.when(st_ref[b, e, k + 1] > rows[k] + GATHER_WIN)
        def _():
            rest = (pos_ref[0, :, k * TM:(k + 1) * TM] == slot) & (slot >= rows[k] + GATHER_WIN)
            acc_ref[0:cap, :] += _dot(jnp.where(rest, 1.0, 0.0).astype(BF16),
                                      hn_ref[0, k * TM:(k + 1) * TM, :])
    o_ref[0] = acc_ref[0:cap, :].astype(BF16)


def _gather_call(starts, pos, hn, cap):
    B, E, T = pos.shape
    grid_spec = pltpu.PrefetchScalarGridSpec(
        num_scalar_prefetch=1,
        grid=(B, E),
        in_specs=[pl.BlockSpec((1, 1, T), lambda b, e, st: (b * E + e, 0, 0)),
                  pl.BlockSpec((1, T, D_MODEL), lambda b, e, st: (b, 0, 0))],
        out_specs=pl.BlockSpec((1, cap, D_MODEL), lambda b, e, st: (e, b, 0)),
        scratch_shapes=[pltpu.VMEM((cap + GATHER_WIN, D_MODEL), F32)])
    return pl.pallas_call(
        functools.partial(_gather_kernel, cap=cap),
        grid_spec=grid_spec,
        out_shape=jax.ShapeDtypeStruct((E, B * cap, D_MODEL), BF16),
        compiler_params=_cparams(2),
        name="gather",
    )(starts, pos.reshape(B * E, 1, T), hn)


def _ffn_kernel(x_ref, w1_ref, w3_ref, w2_ref, o_ref, acc_ref):
    f = pl.program_id(2)

    @pl.when(f == 0)
    def _():
        acc_ref[...] = jnp.zeros_like(acc_ref)

    x = x_ref[0]
    h1 = _dot(x, w1_ref[0].astype(BF16))
    h3 = _dot(x, w3_ref[0].astype(BF16))
    hid = (_silu(h1) * h3).astype(BF16)
    acc_ref[...] += _dot(hid, w2_ref[0].astype(BF16))

    @pl.when(f == pl.num_programs(2) - 1)
    def _():
        o_ref[0] = acc_ref[...].astype(BF16)


def _ffn_call(xg, w1, w3, w2):
    E, M, _ = xg.shape
    tm = min(M, 2048)
    nf = D_EXPERT // FFN_TF
    return pl.pallas_call(
        _ffn_kernel,
        grid=(E, M // tm, nf),
        in_specs=[pl.BlockSpec((1, tm, D_MODEL), lambda e, m, f: (e, m, 0)),
                  pl.BlockSpec((1, D_MODEL, FFN_TF), lambda e, m, f: (e, 0, f)),
                  pl.BlockSpec((1, D_MODEL, FFN_TF), lambda e, m, f: (e, 0, f)),
                  pl.BlockSpec((1, FFN_TF, D_MODEL), lambda e, m, f: (e, f, 0))],
        out_specs=pl.BlockSpec((1, tm, D_MODEL), lambda e, m, f: (e, m, 0)),
        out_shape=jax.ShapeDtypeStruct((E, M, D_MODEL), BF16),
        scratch_shapes=[pltpu.VMEM((tm, D_MODEL), F32)],
        compiler_params=_cparams(3),
        name="ffn",
    )(xg, w1, w3, w2)


def _combine_kernel(st_ref, pos_ref, aff_ref, y_ref, h1_ref, g2_ref, fg_ref, o_ref, moe_ref, *, cap):
    b = pl.program_id(0)
    k = pl.program_id(1)
    pos = pos_ref[0]
    aff = aff_ref[0]
    win = lax.broadcasted_iota(I32, (COMBINE_WIN, TM), 0)
    ws, ys, rows = [], [], []
    for e in range(N_EXPERTS):
        row0 = jnp.minimum(_align_down(st_ref[b, e, k], BF16_ROWS), cap - COMBINE_WIN)
        row0 = pl.multiple_of(row0, BF16_ROWS)
        ws.append(jnp.where(pos[e:e + 1, :] == row0 + win, aff[e:e + 1, :], 0.0).astype(BF16))
        ys.append(y_ref[e, pl.ds(row0, COMBINE_WIN), :])
        rows.append(row0)
    moe_ref[...] = _dot(jnp.concatenate(ws, axis=0), jnp.concatenate(ys, axis=0), TN)
    slot = lax.broadcasted_iota(I32, (cap, TM), 0)
    for e in range(N_EXPERTS):
        @pl.when(st_ref[b, e, k + 1] > rows[e] + COMBINE_WIN)
        def _():
            rest = (pos[e:e + 1, :] == slot) & (slot >= rows[e] + COMBINE_WIN)
            moe_ref[...] += _dot(jnp.where(rest, aff[e:e + 1, :], 0.0).astype(BF16), y_ref[e], TN)
    h2 = h1_ref[0] + g2_ref[0] * moe_ref[...]
    o_ref[0] = _rmsnorm(h2, fg_ref[...])


def _combine_call(starts, pos, aff_t, ys, h1, mods3, final_g, cap):
    B, E, T = pos.shape
    assert cap >= COMBINE_WIN and (cap - COMBINE_WIN) % BF16_ROWS == 0
    grid_spec = pltpu.PrefetchScalarGridSpec(
        num_scalar_prefetch=1,
        grid=(B, T // TM),
        in_specs=[pl.BlockSpec((1, E, TM), lambda b, i, st: (b, 0, i)),
                  pl.BlockSpec((1, E, TM), lambda b, i, st: (b, 0, i)),
                  pl.BlockSpec((E, cap, D_MODEL), lambda b, i, st: (0, b, 0)),
                  pl.BlockSpec((1, TM, D_MODEL), lambda b, i, st: (b, i, 0)),
                  pl.BlockSpec((1, 1, D_MODEL), lambda b, i, st: (b, 0, 5)),
                  pl.BlockSpec((1, D_MODEL), lambda b, i, st: (0, 0))],
        out_specs=pl.BlockSpec((1, TM, D_MODEL), lambda b, i, st: (b, i, 0)),
        scratch_shapes=[pltpu.VMEM((TM, D_MODEL), F32)])
    return pl.pallas_call(
        functools.partial(_combine_kernel, cap=cap),
        grid_spec=grid_spec,
        out_shape=jax.ShapeDtypeStruct((B, T, D_MODEL), F32),
        compiler_params=_cparams(2),
        name="combine",
    )(starts, pos, aff_t, ys, h1, mods3, final_g)


def _pad_cols(w, n):
    return jnp.pad(w, ((0, 0), (0, n - w.shape[1])))


def _both_dirs(up):
    z = jnp.zeros_like(up[0])
    return jnp.concatenate([jnp.concatenate([up[0], z], axis=1),
                            jnp.concatenate([z, up[1]], axis=1)], axis=0)


def kernel(x, c, ctx, c_ctx, mod_w, mod_b, norm1_g, w_in, ml_conv_w, ml_conv_b, ml_gate_b, ml_norm_g,
           rw_mu, rw_w0, rw_w_up, rw_a0, rw_a_up, rw_g_up, rw_k_k, rw_k_a, rw_r_k, rw_ln_g, rw_ln_b,
           w_out, norm2_g, router_w, exp_w1, exp_w3, exp_w2, final_g):
    B, T, D = x.shape
    t_ctx = ctx.shape[1]
    assert D == D_MODEL and T % TM == 0 and t_ctx % TM == 0 and TM % (T // GRID_W) == 0
    assert mod_w.shape[0] == 1 and B < 16
    nct, nlt = t_ctx // TM, T // TM
    cap = EC_FACTOR * T // N_EXPERTS
    ml_cols = 4 * ML_W + ML_GATES

    cc = jnp.concatenate([c, c_ctx[None, :], jnp.zeros((16 - B - 1, D), F32)], axis=0)
    mods = _mod_call(cc, mod_w[0], mod_b)
    mods3 = mods.reshape(16, 1, 6 * D)
    g1n = norm1_g.reshape(1, D)

    w_ml = _pad_cols(w_in[0, :, :ml_cols], ML_N).astype(BF16)
    w_rw = _pad_cols(w_in[0, :, ml_cols:], RW_N).astype(BF16)
    qkv, gates, zo = _ml_in_call(x, ctx, mods3, g1n, w_ml, ml_conv_w[0], ml_conv_b,
                                 _pad_cols(ml_gate_b, LANES))
    h_ml = _mlstm_call(qkv, gates, B, nct, nlt)

    gup = jnp.pad(rw_g_up[0], ((0, 256 - RW_LORA_G), (0, 0)))
    scan_in, gb = _rw_in_call(
        x, ctx, mods3, g1n, w_rw,
        _pad_cols(rw_mu[0], RW_N), rw_w0[0].reshape(1, 2 * RW_W), _both_dirs(rw_w_up[0]),
        rw_a0[0].reshape(1, 2 * RW_W), _both_dirs(rw_a_up[0]), gup,
        rw_k_k, rw_k_a, rw_r_k[0].reshape(1, RW_W))
    y_f, y_b = _rw_scan_call(scan_in, B, t_ctx, T)

    h1, hn, aff_t = _mix_out_call(
        x, h_ml, zo, y_f, y_b, gb, mods3, ml_norm_g, rw_ln_g, rw_ln_b, norm2_g,
        w_out[0].astype(BF16), _pad_cols(router_w[0], LANES))

    pos, starts = _route_call(aff_t, cap)
    starts = starts[:, :, :nlt + 1]
    xg = _gather_call(starts, pos, hn, cap)
    ys = _ffn_call(xg, exp_w1[0], exp_w3[0], exp_w2[0])
    return _combine_call(starts, pos, aff_t, ys, h1, mods3, final_g.reshape(1, D), cap)
```

```python
import functools

import jax
import jax.numpy as jnp
from jax import lax
from jax.experimental import pallas as pl
from jax.experimental.pallas import tpu as pltpu

F32 = jnp.float32
BF16 = jnp.bfloat16
I32 = jnp.int32

D_MODEL = 1024
GRID_W = 64
EPS = 1e-6
ML_W = 512
ML_HEADS = 4
ML_DH = 128
ML_GATES = 16
ML_N = 4 * ML_W + 128
RW_W = 512
RW_HEADS = 8
RW_DH = 64
RW_LORA = 64
RW_LORA_G = 160
RW_LN_EPS = 64e-5
RW_COLS = 3 * RW_W + 2 * RW_LORA + 2 * RW_LORA + RW_LORA_G
RW_N = 2048
RW_PAIRS = RW_HEADS // 2
N_EXPERTS = 16
EC_FACTOR = 2
D_EXPERT = 2816

LANES = 128
SUBLANES = 8
HALO = SUBLANES
TM = 256
RW_CHUNK = 64
RW_PAIRS_PER_STEP = 4
RW_BATCH_PER_STEP = 4
RW_INV_BASE = 4
FFN_TF = 256
BF16_ROWS = 16
GATHER_WIN = 64
COMBINE_WIN = 80
VMEM_LIMIT = 56 * 1024 * 1024

NN = (((1,), (0,)), ((), ()))
NT = (((1,), (1,)), ((), ()))
TN = (((0,), (0,)), ((), ()))


def _cparams(n_axes):
    return pltpu.CompilerParams(
        dimension_semantics=("arbitrary",) * n_axes, vmem_limit_bytes=VMEM_LIMIT)


def _dot(a, b, dn=NN):
    return lax.dot_general(a, b, dn, preferred_element_type=F32)


def _split2(a):
    hi = a.astype(BF16)
    lo = (a - hi.astype(F32)).astype(BF16)
    return hi, lo


def _split3(a):
    hi = a.astype(BF16)
    r1 = a - hi.astype(F32)
    mid = r1.astype(BF16)
    lo = (r1 - mid.astype(F32)).astype(BF16)
    return hi, mid, lo


def _dot1(a, b, dn=NN):
    return _dot(a.astype(BF16), b.astype(BF16), dn)


def _dot3(a, b, dn=NN):
    ah, al = _split2(a)
    bh, bl = _split2(b)
    return _dot(ah, bh, dn) + (_dot(ah, bl, dn) + _dot(al, bh, dn))


def _dot_sel(sel_bf16, b, dn=NN):
    bh, bm, bl = _split3(b)
    return _dot(sel_bf16, bh, dn) + (_dot(sel_bf16, bm, dn) + _dot(sel_bf16, bl, dn))


def _dot_rsel(a, sel_bf16, dn=NN):
    ah, al = _split2(a)
    return _dot(ah, sel_bf16, dn) + _dot(al, sel_bf16, dn)


def _silu(x):
    return x * jax.nn.sigmoid(x)


def _log_sigmoid(x):
    return jnp.minimum(x, 0.0) - jnp.log1p(jnp.exp(-jnp.abs(x)))


def _rmsnorm(x, g):
    return x * lax.rsqrt(jnp.mean(x * x, axis=-1, keepdims=True) + EPS) * g


def _tri(n, lower, strict, dtype):
    r = lax.broadcasted_iota(I32, (n, n), 0)
    c = lax.broadcasted_iota(I32, (n, n), 1)
    if lower:
        m = (c < r) if strict else (c <= r)
    else:
        m = (c > r) if strict else (c >= r)
    return m if dtype is None else m.astype(dtype)


def _cumsum_rows(x, forward):
    n = x.shape[0]
    row = lax.broadcasted_iota(I32, x.shape, 0)
    s = 1
    while s < n:
        if forward:
            x = x + jnp.where(row >= s, pltpu.roll(x, s, 0), 0.0)
        else:
            x = x + jnp.where(row < n - s, pltpu.roll(x, n - s, 0), 0.0)
        s *= 2
    return x


def _shift_rows(z, prev_row, next_row):
    n = z.shape[0]
    row = lax.broadcasted_iota(I32, z.shape, 0)
    zm1 = jnp.where(row == 0, prev_row, pltpu.roll(z, 1, 0))
    zp1 = jnp.where(row == n - 1, next_row, pltpu.roll(z, n - 1, 0))
    return zm1, zp1


def _mod_kernel(c_ref, w_ref, b_ref, o_ref):
    o_ref[...] = _dot3(_silu(c_ref[...]), w_ref[...]) + b_ref[...]


def _mod_call(cc, mod_w, mod_b):
    n = mod_w.shape[1]
    tn = 1024
    return pl.pallas_call(
        _mod_kernel,
        grid=(n // tn,),
        in_specs=[pl.BlockSpec((16, D_MODEL), lambda j: (0, 0)),
                  pl.BlockSpec((D_MODEL, tn), lambda j: (0, j)),
                  pl.BlockSpec((1, tn), lambda j: (0, j))],
        out_specs=pl.BlockSpec((16, tn), lambda j: (0, j)),
        out_shape=jax.ShapeDtypeStruct((16, n), F32),
        compiler_params=_cparams(1),
        name="mod",
    )(cc, mod_w, mod_b)


def _project_tile(xin, prev_rows, next_rows, sh, sc, g, w_ref):
    xe = jnp.concatenate([prev_rows, xin, next_rows], axis=0)
    xn = _rmsnorm(xe, g) * (1.0 + sc) + sh
    return _dot(xn.astype(BF16), w_ref[...])


def _tile_and_edges(u, seg_first, seg_last):
    z = u[HALO:HALO + TM]
    prev_row = jnp.where(seg_first, 0.0, u[HALO - 1:HALO])
    next_row = jnp.where(seg_last, 0.0, u[HALO + TM:HALO + TM + 1])
    return z, prev_row, next_row


def _ml_in_kernel(ctx_ref, x_ref, xp_ref, xn_ref, sh_ref, sc_ref, g_ref, w_ref, cw_ref, cb_ref, gb_ref,
                  qkv_ref, gate_ref, zo_ref, *, nct, ntot):
    i = pl.program_id(1)
    xin = jnp.where(i < nct, ctx_ref[0], x_ref[0])
    u = _project_tile(xin, xp_ref[0], xn_ref[0], sh_ref[0], sc_ref[0], g_ref[...], w_ref)
    z, prev_row, next_row = _tile_and_edges(
        u, (i == 0) | (i == nct), (i == nct - 1) | (i == ntot - 1))
    zqk = z[:, 0:2 * ML_W]
    zm1, zp1 = _shift_rows(zqk, prev_row[:, 0:2 * ML_W], next_row[:, 0:2 * ML_W])
    cw = cw_ref[...]
    qk = _silu(zm1 * cw[0:1] + zqk * cw[1:2] + zp1 * cw[2:3] + cb_ref[...])
    qkv_ref[0, :, 0:ML_W] = qk[:, 0:ML_W].astype(BF16)
    qkv_ref[0, :, ML_W:2 * ML_W] = (qk[:, ML_W:2 * ML_W] * (ML_DH ** -0.5)).astype(BF16)
    qkv_ref[0, :, 2 * ML_W:3 * ML_W] = z[:, 2 * ML_W:3 * ML_W].astype(BF16)
    zo_ref[0] = z[:, 3 * ML_W:4 * ML_W]
    g = z[:, 4 * ML_W:4 * ML_W + LANES] + gb_ref[...]
    lane = lax.broadcasted_iota(I32, (TM, LANES), 1)
    is_f = ((lane >= 4) & (lane < 8)) | ((lane >= 12) & (lane < 16))
    gate_ref[0] = jnp.where(is_f, _log_sigmoid(g), g)


def _ml_in_call(x, ctx, mods3, norm_g, w, conv_w, conv_b, gate_b):
    B, T, _ = x.shape
    assert ctx.shape[1] == TM
    nct, nlt = 1, T // TM
    ntot = nct + nlt
    hb = TM // HALO
    lat = lambda i: jnp.maximum(i - nct, 0)
    mrow = lambda b, i: jnp.where(i < nct, B, b)
    full = lambda shape: pl.BlockSpec(shape, lambda b, i: (0,) * len(shape))
    return pl.pallas_call(
        functools.partial(_ml_in_kernel, nct=nct, ntot=ntot),
        grid=(B, ntot),
        in_specs=[
            pl.BlockSpec((1, TM, D_MODEL), lambda b, i: (b, 0, 0)),
            pl.BlockSpec((1, TM, D_MODEL), lambda b, i: (b, lat(i), 0)),
            pl.BlockSpec((1, HALO, D_MODEL), lambda b, i: (b, jnp.maximum(lat(i) * hb - 1, 0), 0)),
            pl.BlockSpec((1, HALO, D_MODEL),
                         lambda b, i: (b, jnp.minimum((lat(i) + 1) * hb, nlt * hb - 1), 0)),
            pl.BlockSpec((1, 1, D_MODEL), lambda b, i: (mrow(b, i), 0, 0)),
            pl.BlockSpec((1, 1, D_MODEL), lambda b, i: (mrow(b, i), 0, 1)),
            full((1, D_MODEL)), full((D_MODEL, ML_N)),
            full((3, 2 * ML_W)), full((1, 2 * ML_W)), full((1, LANES)),
        ],
        out_specs=[
            pl.BlockSpec((1, TM, 3 * ML_W), lambda b, i: (b, i, 0)),
            pl.BlockSpec((1, TM, LANES), lambda b, i: (b, i, 0)),
            pl.BlockSpec((1, TM, ML_W), lambda b, i: (b, lat(i), 0)),
        ],
        out_shape=[jax.ShapeDtypeStruct((B, ntot * TM, 3 * ML_W), BF16),
                   jax.ShapeDtypeStruct((B, ntot * TM, LANES), F32),
                   jax.ShapeDtypeStruct((B, nlt * TM, ML_W), F32)],
        compiler_params=_cparams(2),
        name="ml_in",
    )(ctx, x, x, x, mods3, mods3, norm_g, w, conv_w, conv_b, gate_b)


def _seq_chunk(d, j, nct, ntot):
    bwd = jnp.where(j < nct, nct - 1 - j, ntot - 1 - (j - nct))
    return jnp.where(d == 0, j, bwd)


def _mlstm_kernel(qf_ref, gf_ref, qb_ref, gb_ref, of_ref, ob_ref, c_ref, n_ref, m_ref):
    j = pl.program_id(1)
    L = TM

    @pl.when(j == 0)
    def _():
        c_ref[...] = jnp.zeros_like(c_ref)
        n_ref[...] = jnp.zeros_like(n_ref)
        m_ref[...] = jnp.zeros_like(m_ref)

    dirs = ((qf_ref, gf_ref, of_ref, True, 0), (qb_ref, gb_ref, ob_ref, False, 2 * ML_HEADS))
    keep_d = [_tri(L, lower, False, None) for _, _, _, lower, _ in dirs]
    q_all = [g_ref[0] for _, g_ref, _, _, _ in dirs]
    b_all = [_dot_sel(keep_d[di].astype(BF16), q_all[di]) for di in range(2)]
    q_t = [q_all[di].T for di in range(2)]
    b_t = [b_all[di].T for di in range(2)]

    chains = [(di, h) for di in range(2) for h in range(ML_HEADS)]
    cs = range(len(chains))
    sl = [slice(h * ML_DH, (h + 1) * ML_DH) for h in range(ML_HEADS)]
    qkv = [dirs[di][0] for di, _ in chains]
    q = [qkv[c][0, :, sl[h]] for c, (di, h) in enumerate(chains)]
    kb = [qkv[c][0, :, ML_W + h * ML_DH:ML_W + (h + 1) * ML_DH] for c, (di, h) in enumerate(chains)]
    vb = [qkv[c][0, :, 2 * ML_W + h * ML_DH:2 * ML_W + (h + 1) * ML_DH] for c, (di, h) in enumerate(chains)]
    v = [vb[c].astype(F32) for c in cs]
    icol = [dirs[di][4] + h for di, h in chains]
    fcol = [icol[c] + ML_HEADS for c in cs]
    b_col = [b_all[di][:, fcol[c]:fcol[c] + 1] for c, (di, h) in enumerate(chains)]
    li_row = [q_t[di][icol[c]:icol[c] + 1, :] for c, (di, h) in enumerate(chains)]
    b_row = [b_t[di][fcol[c]:fcol[c] + 1, :] for c, (di, h) in enumerate(chains)]
    keep = [keep_d[di] for di, _ in chains]
    m_prev = [m_ref[di, h, 0:1, 0:1] for di, h in chains]
    C = [c_ref[di, h] for di, h in chains]
    n_row = [n_ref[di, h, 0:1, :] for di, h in chains]

    col0 = lax.broadcasted_iota(I32, (L, ML_DH), 1) == 0
    row0 = lax.broadcasted_iota(I32, (ML_DH, ML_DH), 0) == 0
    v_ext = [jnp.concatenate([vb[c], jnp.where(col0, 1.0, 0.0).astype(BF16)], axis=1) for c in cs]
    c_ext = [jnp.concatenate([C[c], jnp.where(row0, n_row[c], 0.0)], axis=0).astype(BF16) for c in cs]
    qk_s = [_dot(q[c], kb[c], NT) for c in cs]
    q_cn = [_dot(q[c], c_ext[c], NT) for c in cs]
    rel = [jnp.where(keep[c], li_row[c] - b_row[c], -jnp.inf) for c in cs]
    inter = [b_col[c] + m_prev[c] for c in cs]
    m_t = [jnp.maximum(inter[c], b_col[c] + jnp.max(rel[c], axis=-1, keepdims=True)) for c in cs]
    wts = [jnp.exp(rel[c] + (b_col[c] - m_t[c])) * qk_s[c] for c in cs]
    w_inter = [jnp.exp(inter[c] - m_t[c]) for c in cs]
    nd = [_dot(wts[c].astype(BF16), v_ext[c]) + w_inter[c] * q_cn[c] for c in cs]
    for c, (di, h) in enumerate(chains):
        den = nd[c][:, ML_DH:ML_DH + 1]
        dirs[di][2][0, :, sl[h]] = nd[c][:, 0:ML_DH] / jnp.maximum(jnp.abs(den), jnp.exp(-m_t[c]))

    b_end = [b_col[c][L - 1:L, :] if dirs[di][3] else b_col[c][0:1, :] for c, (di, h) in enumerate(chains)]
    dec = [b_end[c] - b_row[c] + li_row[c] for c in cs]
    m_new = [jnp.maximum(b_end[c] + m_prev[c], jnp.max(dec[c], axis=-1, keepdims=True)) for c in cs]
    wk = [jnp.exp(dec[c] - m_new[c]) for c in cs]
    s_old = [jnp.exp(b_end[c] + m_prev[c] - m_new[c]) for c in cs]
    c_upd = [_dot((v[c].T * wk[c]).astype(BF16), kb[c]) for c in cs]
    n_upd = [_dot(jnp.broadcast_to(wk[c], (SUBLANES, L)).astype(BF16), kb[c]) for c in cs]
    for c, (di, h) in enumerate(chains):
        c_ref[di, h] = s_old[c] * C[c] + c_upd[c]
        n_ref[di, h] = s_old[c] * n_ref[di, h] + n_upd[c]
        m_ref[di, h] = jnp.broadcast_to(m_new[c], (SUBLANES, LANES))


def _mlstm_call(qkv, gates, B, nct, nlt):
    ntot = nct + nlt
    mirror = lambda j: _seq_chunk(1, j, nct, ntot)
    out_shape = jax.ShapeDtypeStruct((B, nlt * TM, ML_W), F32)
    return pl.pallas_call(
        _mlstm_kernel,
        grid=(B, ntot),
        in_specs=[
            pl.BlockSpec((1, TM, 3 * ML_W), lambda b, j: (b, j, 0)),
            pl.BlockSpec((1, TM, LANES), lambda b, j: (b, j, 0)),
            pl.BlockSpec((1, TM, 3 * ML_W), lambda b, j: (b, mirror(j), 0)),
            pl.BlockSpec((1, TM, LANES), lambda b, j: (b, mirror(j), 0)),
        ],
        out_specs=[
            pl.BlockSpec((1, TM, ML_W), lambda b, j: (b, jnp.maximum(j - nct, 0), 0)),
            pl.BlockSpec((1, TM, ML_W), lambda b, j: (b, jnp.where(j < nct, nlt - 1, mirror(j) - nct), 0)),
        ],
        out_shape=[out_shape, out_shape],
        scratch_shapes=[pltpu.VMEM((2, ML_HEADS, ML_DH, ML_DH), F32),
                        pltpu.VMEM((2, ML_HEADS, SUBLANES, ML_DH), F32),
                        pltpu.VMEM((2, ML_HEADS, SUBLANES, LANES), F32)],
        compiler_params=_cparams(2),
        name="mlstm",
    )(qkv, gates, qkv, gates)


def _rw_in_kernel(ctx_ref, x_ref, xp_ref, xn_ref, sh_ref, sc_ref, g_ref, w_ref,
                  mu_ref, w0_ref, wup_ref, a0_ref, aup_ref, gup_ref, kk_ref, ka_ref, rk_ref,
                  scan_ref, gb_ref, *, nct, ntot, rows):
    i = pl.program_id(1)
    jt = jnp.maximum(i - nct, 0)
    cpt = TM // rows
    tpb = x_ref.shape[2] // cpt
    c0 = (jt % tpb) * cpt if tpb > 1 else 0
    xcm = jnp.concatenate([x_ref[0, :, c0 + c, :] for c in range(cpt)], axis=0)
    xin = jnp.where(i < nct, ctx_ref[0], xcm)
    prev1 = xp_ref[0, SUBLANES - 1:SUBLANES, jnp.maximum(jt * cpt - 1, 0) % SUBLANES, :]
    next1 = xn_ref[0, 0:1, jnp.minimum((jt + 1) * cpt, GRID_W - 1) % SUBLANES, :]
    u = _project_tile(xin, jnp.broadcast_to(prev1, (HALO, D_MODEL)),
                      jnp.broadcast_to(next1, (HALO, D_MODEL)),
                      sh_ref[0], sc_ref[0], g_ref[...], w_ref)
    z, prev_row, next_row = _tile_and_edges(
        u, (i == 0) | (i == nct), (i == nct - 1) | (i == ntot - 1))
    zm1, zp1 = _shift_rows(z, prev_row, next_row)
    mu = mu_ref[...]
    xs = z * (1.0 - mu[0:1] - mu[1:2]) + mu[0:1] * zm1 + mu[1:2] * zp1

    r = xs[:, 0:RW_W]
    k = xs[:, RW_W:2 * RW_W]
    v = xs[:, 2 * RW_W:3 * RW_W]
    wd = xs[:, 3 * RW_W:3 * RW_W + 2 * RW_LORA]
    ad = xs[:, 3 * RW_W + 2 * RW_LORA:3 * RW_W + 4 * RW_LORA]
    gd = xs[:, 3 * RW_W + 4 * RW_LORA:RW_N]

    w_pre = w0_ref[...] + _dot1(jnp.tanh(wd), wup_ref[...])
    log_w = -jax.nn.softplus(-w_pre) - 0.5
    lw = -jnp.exp(log_w)
    a = jax.nn.sigmoid(a0_ref[...] + _dot1(ad, aup_ref[...]))
    g = _dot1(jax.nn.sigmoid(gd), gup_ref[...])

    hr = lax.broadcasted_iota(I32, (RW_W, RW_W), 0) // RW_DH
    hc = lax.broadcasted_iota(I32, (RW_W, RW_W), 1) // RW_DH
    head_ones = (hr == hc).astype(BF16)
    kk = k * kk_ref[...]
    ss = _dot_rsel(kk * kk, head_ones)
    kk = kk / jnp.maximum(jnp.sqrt(ss), 1e-12)
    ka = ka_ref[...]
    k_f = k * (1.0 + (a[:, 0:RW_W] - 1.0) * ka)
    k_b = k * (1.0 + (a[:, RW_W:2 * RW_W] - 1.0) * ka)
    rk = rk_ref[...]
    bonus = _dot_rsel(r * (k_f + k_b) * rk, head_ones) * v

    cols = (r, v, kk, lw[:, 0:RW_W], k_f, kk * a[:, 0:RW_W],
            lw[:, RW_W:2 * RW_W], k_b, kk * a[:, RW_W:2 * RW_W])
    for p in range(RW_PAIRS):
        for qi, arr in enumerate(cols):
            scan_ref[0, p, :, qi * LANES:(qi + 1) * LANES] = arr[:, p * LANES:(p + 1) * LANES]
    gb_ref[0, :, 0:RW_W] = g
    gb_ref[0, :, RW_W:2 * RW_W] = bonus


def _rw_in_call(x, ctx, mods3, norm_g, w, mu, w0, wup, a0, aup, gup, k_k, k_a, r_k):
    B, T, _ = x.shape
    assert ctx.shape[1] == TM
    nct, nlt = 1, T // TM
    ntot = nct + nlt
    rows = T // GRID_W
    cpt = TM // rows
    cb = max(SUBLANES, cpt)
    tpb = cb // cpt
    assert rows % SUBLANES == 0 and GRID_W % cb == 0 and cb % cpt == 0
    lat = lambda c: jnp.maximum(c - nct, 0)
    mrow = lambda b, c: jnp.where(c < nct, B, b)
    full = lambda shape: pl.BlockSpec(shape, lambda b, c: (0,) * len(shape))
    x4 = x.reshape(B, rows, GRID_W, D_MODEL)
    return pl.pallas_call(
        functools.partial(_rw_in_kernel, nct=nct, ntot=ntot, rows=rows),
        grid=(B, ntot),
        in_specs=[
            pl.BlockSpec((1, TM, D_MODEL), lambda b, c: (b, 0, 0)),
            pl.BlockSpec((1, rows, cb, D_MODEL), lambda b, c: (b, 0, lat(c) // tpb, 0)),
            pl.BlockSpec((1, SUBLANES, SUBLANES, D_MODEL),
                         lambda b, c: (b, rows // SUBLANES - 1,
                                       jnp.maximum(lat(c) * cpt - 1, 0) // SUBLANES, 0)),
            pl.BlockSpec((1, SUBLANES, SUBLANES, D_MODEL),
                         lambda b, c: (b, 0, jnp.minimum((lat(c) + 1) * cpt, GRID_W - 1) // SUBLANES, 0)),
            pl.BlockSpec((1, 1, D_MODEL), lambda b, c: (mrow(b, c), 0, 0)),
            pl.BlockSpec((1, 1, D_MODEL), lambda b, c: (mrow(b, c), 0, 1)),
            full((1, D_MODEL)), full((D_MODEL, RW_N)),
            full((2, RW_N)), full((1, 2 * RW_W)), full((2 * RW_LORA, 2 * RW_W)),
            full((1, 2 * RW_W)), full((2 * RW_LORA, 2 * RW_W)), full((256, RW_W)),
            full((1, RW_W)), full((1, RW_W)), full((1, RW_W)),
        ],
        out_specs=[
            pl.BlockSpec((1, RW_PAIRS, TM, 9 * LANES), lambda b, c: (b, 0, c, 0)),
            pl.BlockSpec((1, TM, 2 * RW_W), lambda b, c: (b, jnp.maximum(c - nct, 0), 0)),
        ],
        out_shape=[jax.ShapeDtypeStruct((B, RW_PAIRS, ntot * TM, 9 * LANES), F32),
                   jax.ShapeDtypeStruct((B, nlt * TM, 2 * RW_W), F32)],
        compiler_params=_cparams(2),
        name="rw_in",
    )(ctx, x4, x4, x4, mods3, mods3, norm_g, w, mu, w0, wup, a0, aup, gup, k_k, k_a, r_k)


def _rw_chunks(blks, states, lowers):
    n = len(blks)
    L = blks[0].shape[0]
    assert L == RW_DH
    idx = range(n)
    col = lambda i, c: blks[i][:, c * LANES:(c + 1) * LANES]
    r, v, kk, lw, kd, bh = ([col(i, c) for i in idx] for c in range(6))
    logp = [_cumsum_rows(lw[i], lowers[i]) for i in idx]
    logp_end = [logp[i][L - 1:L, :] if lowers[i] else logp[i][0:1, :] for i in idx]
    p_inv = [jnp.exp(-logp[i]) for i in idx]
    a_t = [-kk[i] * jnp.exp(logp[i] - lw[i]) for i in idx]
    r_t = [r[i] * jnp.exp(logp[i]) for i in idx]

    row = lax.broadcasted_iota(I32, (L, LANES), 0)
    lane = lax.broadcasted_iota(I32, (L, LANES), 1)
    head0 = lane < RW_DH
    src = jnp.where(head0, lane, lane - RW_DH)

    def split(y_pair):
        return jnp.concatenate([jnp.where(head0, y_pair, 0.0), jnp.where(head0, 0.0, y_pair)],
                               axis=0).astype(BF16)

    def mm(x_cat, y_pair):
        return _dot(x_cat.astype(BF16), split(y_pair))

    strict = {lo: (src < row) if lo else (src > row) for lo in set(lowers)}
    incl_c = {lo: (src <= row) if lo else (src >= row) for lo in set(lowers)}
    eye = jnp.where(src == row, 1.0, 0.0)

    a_s = [_dot1(a_t[i], states[i], NT) for i in idx]
    r_s = [_dot1(r_t[i], states[i], NT) for i in idx]
    bk = [jnp.concatenate([split(bh[i] * p_inv[i]), split(kd[i] * p_inv[i])], axis=0) for i in idx]
    g_ar = [_dot(jnp.concatenate([a_t[i], r_t[i]], axis=0).astype(BF16), bk[i], NT) for i in idx]
    g_a = [g_ar[i][0:L] for i in idx]
    g_r = [g_ar[i][L:2 * L] for i in idx]
    n_mat = [jnp.where(strict[lowers[i]], g_a[i][:, 0:LANES], 0.0) for i in idx]
    a_ak = [jnp.where(strict[lowers[i]], g_a[i][:, LANES:2 * LANES], 0.0) for i in idx]
    a_rb = [jnp.where(incl_c[lowers[i]], g_r[i][:, 0:LANES], 0.0) for i in idx]
    a_rk = [jnp.where(incl_c[lowers[i]], g_r[i][:, LANES:2 * LANES], 0.0) for i in idx]
    w_rhs = [a_s[i] + mm(a_ak[i], v[i]) for i in idx]
    base = RW_INV_BASE
    n_d = [jnp.where(row // base == src // base, n_mat[i], 0.0) for i in idx]
    inv = [eye + n_d[i] for i in idx]
    n_d2 = [mm(n_d[i], n_d[i]) for i in idx]
    inv = [inv[i] + mm(inv[i], n_d2[i]) for i in idx]
    blk = base
    while blk < L:
        link = (row // (2 * blk) == src // (2 * blk)) & (row // blk != src // blk)
        t_m = [mm(inv[i], jnp.where(link, n_mat[i], 0.0)) for i in idx]
        inv = [inv[i] + mm(t_m[i], inv[i]) for i in idx]
        blk *= 2
    u = [mm(inv[i], w_rhs[i]) for i in idx]
    uv = [jnp.concatenate([u[i], v[i]], axis=0) for i in idx]
    y = [r_s[i] + _dot(jnp.concatenate([a_rb[i], a_rk[i]], axis=1).astype(BF16),
                       jnp.concatenate([split(u[i]), split(v[i])], axis=0)) for i in idx]

    rr = lax.broadcasted_iota(I32, (LANES, LANES), 0) // RW_DH
    cc = lax.broadcasted_iota(I32, (LANES, LANES), 1) // RW_DH
    s_new = []
    for i in idx:
        to_end = jnp.exp(logp_end[i] - logp[i])
        bk_end = jnp.concatenate([bh[i] * to_end, kd[i] * to_end], axis=0)
        s_new.append(jnp.where(
            rr == cc, states[i] * jnp.exp(logp_end[i]) + _dot1(uv[i], bk_end, TN), 0.0))
    return y, s_new


def _rw_scan_kernel(f_ref, b_ref, yf_ref, yb_ref, s_ref):
    j = pl.program_id(2)

    @pl.when(j == 0)
    def _():
        s_ref[...] = jnp.zeros_like(s_ref)

    probs = [(bi, p) for bi in range(f_ref.shape[0]) for p in range(f_ref.shape[1])]
    blks, states, lowers = [], [], []
    for bi, p in probs:
        fb = f_ref[bi, p]
        bb = b_ref[bi, p]
        blks += [fb[:, 0:6 * LANES],
                 jnp.concatenate([bb[:, 0:3 * LANES], bb[:, 6 * LANES:9 * LANES]], axis=1)]
        states += [s_ref[bi, p, 0], s_ref[bi, p, 1]]
        lowers += [True, False]
    y, s_new = _rw_chunks(blks, states, lowers)
    for q, (bi, p) in enumerate(probs):
        yf_ref[bi, p] = y[2 * q]
        yb_ref[bi, p] = y[2 * q + 1]
        s_ref[bi, p, 0] = s_new[2 * q]
        s_ref[bi, p, 1] = s_new[2 * q + 1]


def _rw_scan_call(scan_in, B, t_ctx, t_lat):
    L = RW_CHUNK
    nct, nlt = t_ctx // L, t_lat // L
    ntot = nct + nlt
    fchunk = lambda j: j
    bchunk = lambda j: _seq_chunk(1, j, nct, ntot)
    pps = RW_PAIRS_PER_STEP
    bps = RW_BATCH_PER_STEP if B % RW_BATCH_PER_STEP == 0 else 1
    in_blk = (bps, pps, L, 9 * LANES)
    out_blk = (bps, pps, L, LANES)
    out_shape = jax.ShapeDtypeStruct((B, RW_PAIRS, t_lat, LANES), F32)
    return pl.pallas_call(
        _rw_scan_kernel,
        grid=(B // bps, RW_PAIRS // pps, ntot),
        in_specs=[pl.BlockSpec(in_blk, lambda b, p, j: (b, p, fchunk(j), 0)),
                  pl.BlockSpec(in_blk, lambda b, p, j: (b, p, bchunk(j), 0))],
        out_specs=[
            pl.BlockSpec(out_blk, lambda b, p, j: (b, p, jnp.maximum(fchunk(j) - nct, 0), 0)),
            pl.BlockSpec(out_blk,
                         lambda b, p, j: (b, p, jnp.where(j < nct, nlt - 1, bchunk(j) - nct), 0)),
        ],
        out_shape=[out_shape, out_shape],
        scratch_shapes=[pltpu.VMEM((bps, pps, 2, LANES, LANES), F32)],
        compiler_params=_cparams(3),
        name="rw_scan",
    )(scan_in, scan_in)


def _mix_out_kernel(x_ref, hf_ref, hb_ref, zo_ref, yf_ref, yb_ref, gb_ref, g1_ref, sh2_ref, sc2_ref,
                    mlg_ref, lng_ref, lnb_ref, n2g_ref, wo_ref, rw_ref,
                    h1_ref, hn_ref, aff_ref):
    hm = hf_ref[0] + hb_ref[0]
    parts = []
    for h in range(ML_HEADS):
        hh = hm[:, h * ML_DH:(h + 1) * ML_DH]
        parts.append(hh * lax.rsqrt(jnp.mean(hh * hh, axis=-1, keepdims=True) + EPS))
    ml = jnp.concatenate(parts, axis=1) * mlg_ref[...] * jax.nn.sigmoid(zo_ref[0])

    nr = yf_ref.shape[3]
    to_raster = lambda a: jnp.swapaxes(a, 0, 1).reshape(nr * GRID_W, a.shape[-1])
    y = jnp.concatenate([to_raster(yf_ref[0, p] + yb_ref[0, p]) for p in range(RW_PAIRS)], axis=1)
    gbr = to_raster(gb_ref[0])
    hr = lax.broadcasted_iota(I32, (RW_W, RW_W), 0) // RW_DH
    hc = lax.broadcasted_iota(I32, (RW_W, RW_W), 1) // RW_DH
    head_ones = (hr == hc).astype(BF16)
    mean = _dot_rsel(y, head_ones) * (1.0 / RW_DH)
    dy = y - mean
    var = _dot_rsel(dy * dy, head_ones) * (1.0 / RW_DH)
    rw = dy * lax.rsqrt(var + RW_LN_EPS) * lng_ref[...] + lnb_ref[...]
    rw = (rw + gbr[:, RW_W:2 * RW_W]) * gbr[:, 0:RW_W]

    mix = _dot(jnp.concatenate([ml, rw], axis=1).astype(BF16), wo_ref[...])
    h1 = x_ref[0] + g1_ref[0] * mix
    h1_ref[0] = h1
    hn = _rmsnorm(h1, n2g_ref[...]) * (1.0 + sc2_ref[0]) + sh2_ref[0]
    hn_ref[0] = hn.astype(BF16)
    logits = _dot3(hn, rw_ref[...])
    lane = lax.broadcasted_iota(I32, logits.shape, 1)
    logits = jnp.where(lane < N_EXPERTS, logits, -jnp.inf)
    e = jnp.exp(logits - jnp.max(logits, axis=-1, keepdims=True))
    aff = e / jnp.sum(e, axis=-1, keepdims=True)
    aff_ref[0] = aff.T[0:N_EXPERTS, :]


def _mix_out_call(x, h_f, h_b, zo, y_f, y_b, gb, mods3, ml_norm_g, ln_g, ln_b, norm2_g,
                  wo, router_pad):
    B, T, _ = x.shape
    rows = T // GRID_W
    rb = SUBLANES
    tmo = rb * GRID_W
    assert rows % rb == 0
    yv = lambda y: y.reshape(B, RW_PAIRS, GRID_W, rows, LANES)
    gbv = gb.reshape(B, GRID_W, rows, 2 * RW_W)
    row1 = lambda shape: pl.BlockSpec(shape, lambda b, i: (0,) * len(shape))
    mod = lambda k: pl.BlockSpec((1, 1, D_MODEL), lambda b, i: (b, 0, k))
    return pl.pallas_call(
        _mix_out_kernel,
        grid=(B, T // tmo),
        in_specs=[
            pl.BlockSpec((1, tmo, D_MODEL), lambda b, i: (b, i, 0)),
            pl.BlockSpec((1, tmo, ML_W), lambda b, i: (b, i, 0)),
            pl.BlockSpec((1, tmo, ML_W), lambda b, i: (b, i, 0)),
            pl.BlockSpec((1, tmo, ML_W), lambda b, i: (b, i, 0)),
            pl.BlockSpec((1, RW_PAIRS, GRID_W, rb, LANES), lambda b, i: (b, 0, 0, i, 0)),
            pl.BlockSpec((1, RW_PAIRS, GRID_W, rb, LANES), lambda b, i: (b, 0, 0, i, 0)),
            pl.BlockSpec((1, GRID_W, rb, 2 * RW_W), lambda b, i: (b, 0, i, 0)),
            mod(2), mod(3), mod(4),
            row1((1, ML_W)), row1((1, RW_W)), row1((1, RW_W)), row1((1, D_MODEL)),
            row1((D_MODEL, D_MODEL)), row1((D_MODEL, LANES)),
        ],
        out_specs=[
            pl.BlockSpec((1, tmo, D_MODEL), lambda b, i: (b, i, 0)),
            pl.BlockSpec((1, tmo, D_MODEL), lambda b, i: (b, i, 0)),
            pl.BlockSpec((1, N_EXPERTS, tmo), lambda b, i: (b, 0, i)),
        ],
        out_shape=[jax.ShapeDtypeStruct((B, T, D_MODEL), F32),
                   jax.ShapeDtypeStruct((B, T, D_MODEL), BF16),
                   jax.ShapeDtypeStruct((B, N_EXPERTS, T), F32)],
        compiler_params=_cparams(2),
        name="mix_out",
    )(x, h_f, h_b, zo, yv(y_f), yv(y_b), gbv, mods3, mods3, mods3,
      ml_norm_g, ln_g, ln_b, norm2_g, wo, router_pad)


def _route_kernel(aff_ref, pos_ref, st_ref, *, cap):
    a = aff_ref[0]
    T = a.shape[1]
    as_f32 = lambda bits: lax.bitcast_convert_type(bits, F32)

    def body(i, thr):
        cand = thr | jnp.left_shift(jnp.int32(1), 30 - i)
        cnt = jnp.sum(jnp.where(a >= as_f32(cand), 1.0, 0.0), axis=1, keepdims=True)
        return jnp.where(cnt >= cap, cand, thr)

    thr = lax.fori_loop(0, 31, body, jnp.zeros((N_EXPERTS, 1), I32))
    gt = a >= as_f32(thr + 1)
    eq = (a >= as_f32(thr)) & jnp.logical_not(gt)
    need = cap - jnp.sum(jnp.where(gt, 1.0, 0.0), axis=1, keepdims=True)
    tri = _tri(TM, False, False, BF16)

    def prefix_excl(mask):
        outs, carries = [], []
        carry = jnp.zeros((N_EXPERTS, 1), F32)
        for blk in range(T // TM):
            seg = jnp.where(mask[:, blk * TM:(blk + 1) * TM], 1.0, 0.0)
            inc = _dot(seg.astype(BF16), tri)
            outs.append(inc - seg + carry)
            carries.append(carry)
            carry = carry + jnp.sum(seg, axis=1, keepdims=True)
        return jnp.concatenate(outs, axis=1), carries + [carry]

    chosen = gt | (eq & (prefix_excl(eq)[0] < need))
    slot, block_starts = prefix_excl(chosen)
    pos_ref[0] = jnp.where(chosen, slot, -1.0).astype(I32)
    lane = lax.broadcasted_iota(I32, (N_EXPERTS, LANES), 1)
    st = jnp.zeros((N_EXPERTS, LANES), F32)
    for blk, start in enumerate(block_starts):
        st = jnp.where(lane == blk, start, st)
    st_ref[0] = st.astype(I32)


def _route_call(aff_t, cap):
    B, E, T = aff_t.shape
    assert T // TM < LANES
    return pl.pallas_call(
        functools.partial(_route_kernel, cap=cap),
        grid=(B,),
        in_specs=[pl.BlockSpec((1, E, T), lambda b: (b, 0, 0))],
        out_specs=[pl.BlockSpec((1, E, T), lambda b: (b, 0, 0)),
                   pl.BlockSpec((1, E, LANES), lambda b: (b, 0, 0))],
        out_shape=[jax.ShapeDtypeStruct((B, E, T), I32),
                   jax.ShapeDtypeStruct((B, E, LANES), I32)],
        compiler_params=_cparams(1),
        name="route",
    )(aff_t)


def _align_down(s, m):
    sh = m.bit_length() - 1
    return pl.multiple_of(lax.shift_left(lax.shift_right_logical(s, sh), sh), m)


def _gather_kernel(st_ref, pos_ref, hn_ref, o_ref, acc_ref, *, cap):
    b = pl.program_id(0)
    e = pl.program_id(1)
    nb = hn_ref.shape[1] // TM
    acc_ref[...] = jnp.zeros_like(acc_ref)
    win = lax.broadcasted_iota(I32, (GATHER_WIN, TM), 0)
    rows = []
    for k in range(nb):
        row0 = _align_down(st_ref[b, e, k], SUBLANES)
        onehot = jnp.where(pos_ref[0, :, k * TM:(k + 1) * TM] == row0 + win, 1.0, 0.0)
        acc_ref[pl.ds(row0, GATHER_WIN), :] += _dot(onehot.astype(BF16), hn_ref[0, k * TM:(k + 1) * TM, :])
        rows.append(row0)
    slot = lax.broadcasted_iota(I32, (cap, TM), 0)
    for k in range(nb):
        @pl.when(st_ref[b, e, k + 1] > rows[k] + GATHER_WIN)
        def _():
            rest = (pos_ref[0, :, k * TM:(k + 1) * TM] == slot) & (slot >= rows[k] + GATHER_WIN)
            acc_ref[0:cap, :] += _dot(jnp.where(rest, 1.0, 0.0).astype(BF16),
                                      hn_ref[0, k * TM:(k + 1) * TM, :])
    o_ref[0] = acc_ref[0:cap, :].astype(BF16)


def _gather_call(starts, pos, hn, cap):
    B, E, T = pos.shape
    grid_spec = pltpu.PrefetchScalarGridSpec(
        num_scalar_prefetch=1,
        grid=(B, E),
        in_specs=[pl.BlockSpec((1, 1, T), lambda b, e, st: (b * E + e, 0, 0)),
                  pl.BlockSpec((1, T, D_MODEL), lambda b, e, st: (b, 0, 0))],
        out_specs=pl.BlockSpec((1, cap, D_MODEL), lambda b, e, st: (e, b, 0)),
        scratch_shapes=[pltpu.VMEM((cap + GATHER_WIN, D_MODEL), F32)])
    return pl.pallas_call(
        functools.partial(_gather_kernel, cap=cap),
        grid_spec=grid_spec,
        out_shape=jax.ShapeDtypeStruct((E, B * cap, D_MODEL), BF16),
        compiler_params=_cparams(2),
        name="gather",
    )(starts, pos.reshape(B * E, 1, T), hn)


def _ffn_kernel(x_ref, w1_ref, w3_ref, w2_ref, o_ref, acc_ref):
    f = pl.program_id(2)

    @pl.when(f == 0)
    def _():
        acc_ref[...] = jnp.zeros_like(acc_ref)

    x = x_ref[0]
    h1 = _dot(x, w1_ref[0].astype(BF16))
    h3 = _dot(x, w3_ref[0].astype(BF16))
    hid = (_silu(h1) * h3).astype(BF16)
    acc_ref[...] += _dot(hid, w2_ref[0].astype(BF16))

    @pl.when(f == pl.num_programs(2) - 1)
    def _():
        o_ref[0] = acc_ref[...].astype(BF16)


def _ffn_call(xg, w1, w3, w2):
    E, M, _ = xg.shape
    tm = min(M, 2048)
    nf = D_EXPERT // FFN_TF
    return pl.pallas_call(
        _ffn_kernel,
        grid=(E, M // tm, nf),
        in_specs=[pl.BlockSpec((1, tm, D_MODEL), lambda e, m, f: (e, m, 0)),
                  pl.BlockSpec((1, D_MODEL, FFN_TF), lambda e, m, f: (e, 0, f)),
                  pl.BlockSpec((1, D_MODEL, FFN_TF), lambda e, m, f: (e, 0, f)),
                  pl.BlockSpec((1, FFN_TF, D_MODEL), lambda e, m, f: (e, f, 0))],
        out_specs=pl.BlockSpec((1, tm, D_MODEL), lambda e, m, f: (e, m, 0)),
        out_shape=jax.ShapeDtypeStruct((E, M, D_MODEL), BF16),
        scratch_shapes=[pltpu.VMEM((tm, D_MODEL), F32)],
        compiler_params=_cparams(3),
        name="ffn",
    )(xg, w1, w3, w2)


def _combine_kernel(st_ref, pos_ref, aff_ref, y_ref, h1_ref, g2_ref, fg_ref, o_ref, moe_ref, *, cap):
    b = pl.program_id(0)
    k = pl.program_id(1)
    pos = pos_ref[0]
    aff = aff_ref[0]
    win = lax.broadcasted_iota(I32, (COMBINE_WIN, TM), 0)
    ws, ys, rows = [], [], []
    for e in range(N_EXPERTS):
        row0 = jnp.minimum(_align_down(st_ref[b, e, k], BF16_ROWS), cap - COMBINE_WIN)
        row0 = pl.multiple_of(row0, BF16_ROWS)
        ws.append(jnp.where(pos[e:e + 1, :] == row0 + win, aff[e:e + 1, :], 0.0).astype(BF16))
        ys.append(y_ref[e, pl.ds(row0, COMBINE_WIN), :])
        rows.append(row0)
    moe_ref[...] = _dot(jnp.concatenate(ws, axis=0), jnp.concatenate(ys, axis=0), TN)
    slot = lax.broadcasted_iota(I32, (cap, TM), 0)
    for e in range(N_EXPERTS):
        @pl.when(st_ref[b, e, k + 1] > rows[e] + COMBINE_WIN)
        def _():
            rest = (pos[e:e + 1, :] == slot) & (slot >= rows[e] + COMBINE_WIN)
            moe_ref[...] += _dot(jnp.where(rest, aff[e:e + 1, :], 0.0).astype(BF16), y_ref[e], TN)
    h2 = h1_ref[0] + g2_ref[0] * moe_ref[...]
    o_ref[0] = _rmsnorm(h2, fg_ref[...])


def _combine_call(starts, pos, aff_t, ys, h1, mods3, final_g, cap):
    B, E, T = pos.shape
    assert cap >= COMBINE_WIN and (cap - COMBINE_WIN) % BF16_ROWS == 0
    grid_spec = pltpu.PrefetchScalarGridSpec(
        num_scalar_prefetch=1,
        grid=(B, T // TM),
        in_specs=[pl.BlockSpec((1, E, TM), lambda b, i, st: (b, 0, i)),
                  pl.BlockSpec((1, E, TM), lambda b, i, st: (b, 0, i)),
                  pl.BlockSpec((E, cap, D_MODEL), lambda b, i, st: (0, b, 0)),
                  pl.BlockSpec((1, TM, D_MODEL), lambda b, i, st: (b, i, 0)),
                  pl.BlockSpec((1, 1, D_MODEL), lambda b, i, st: (b, 0, 5)),
                  pl.BlockSpec((1, D_MODEL), lambda b, i, st: (0, 0))],
        out_specs=pl.BlockSpec((1, TM, D_MODEL), lambda b, i, st: (b, i, 0)),
        scratch_shapes=[pltpu.VMEM((TM, D_MODEL), F32)])
    return pl.pallas_call(
        functools.partial(_combine_kernel, cap=cap),
        grid_spec=grid_spec,
        out_shape=jax.ShapeDtypeStruct((B, T, D_MODEL), F32),
        compiler_params=_cparams(2),
        name="combine",
    )(starts, pos, aff_t, ys, h1, mods3, final_g)


def _pad_cols(w, n):
    return jnp.pad(w, ((0, 0), (0, n - w.shape[1])))


def _both_dirs(up):
    z = jnp.zeros_like(up[0])
    return jnp.concatenate([jnp.concatenate([up[0], z], axis=1),
                            jnp.concatenate([z, up[1]], axis=1)], axis=0)


def kernel(x, c, ctx, c_ctx, mod_w, mod_b, norm1_g, w_in, ml_conv_w, ml_conv_b, ml_gate_b, ml_norm_g,
           rw_mu, rw_w0, rw_w_up, rw_a0, rw_a_up, rw_g_up, rw_k_k, rw_k_a, rw_r_k, rw_ln_g, rw_ln_b,
           w_out, norm2_g, router_w, exp_w1, exp_w3, exp_w2, final_g):
    B, T, D = x.shape
    t_ctx = ctx.shape[1]
    assert D == D_MODEL and T % TM == 0 and t_ctx % TM == 0 and TM % (T // GRID_W) == 0
    assert mod_w.shape[0] == 1 and B < 16
    nct, nlt = t_ctx // TM, T // TM
    cap = EC_FACTOR * T // N_EXPERTS
    ml_cols = 4 * ML_W + ML_GATES

    cc = jnp.concatenate([c, c_ctx[None, :], jnp.zeros((16 - B - 1, D), F32)], axis=0)
    mods = _mod_call(cc, mod_w[0], mod_b)
    mods3 = mods.reshape(16, 1, 6 * D)
    g1n = norm1_g.reshape(1, D)

    w_ml = _pad_cols(w_in[0, :, :ml_cols], ML_N).astype(BF16)
    w_rw = _pad_cols(w_in[0, :, ml_cols:], RW_N).astype(BF16)
    qkv, gates, zo = _ml_in_call(x, ctx, mods3, g1n, w_ml, ml_conv_w[0], ml_conv_b,
                                 _pad_cols(ml_gate_b, LANES))
    h_f, h_b = _mlstm_call(qkv, gates, B, nct, nlt)

    gup = jnp.pad(rw_g_up[0], ((0, 256 - RW_LORA_G), (0, 0)))
    scan_in, gb = _rw_in_call(
        x, ctx, mods3, g1n, w_rw,
        _pad_cols(rw_mu[0], RW_N), rw_w0[0].reshape(1, 2 * RW_W), _both_dirs(rw_w_up[0]),
        rw_a0[0].reshape(1, 2 * RW_W), _both_dirs(rw_a_up[0]), gup,
        rw_k_k, rw_k_a, rw_r_k[0].reshape(1, RW_W))
    y_f, y_b = _rw_scan_call(scan_in, B, t_ctx, T)

    h1, hn, aff_t = _mix_out_call(
        x, h_f, h_b, zo, y_f, y_b, gb, mods3, ml_norm_g, rw_ln_g, rw_ln_b, norm2_g,
        w_out[0].astype(BF16), _pad_cols(router_w[0], LANES))

    pos, starts = _route_call(aff_t, cap)
    starts = starts[:, :, :nlt + 1]
    xg = _gather_call(starts, pos, hn, cap)
    ys = _ffn_call(xg, exp_w1[0], exp_w3[0], exp_w2[0])
    return _combine_call(starts, pos, aff_t, ys, h1, mods3, final_g.reshape(1, D), cap)
```

```python
import functools

import jax
import jax.numpy as jnp
from jax import lax
from jax.experimental import pallas as pl
from jax.experimental.pallas import tpu as pltpu

F32 = jnp.float32
BF16 = jnp.bfloat16
I32 = jnp.int32

D_MODEL = 1024
GRID_W = 64
EPS = 1e-6
ML_W = 512
ML_HEADS = 4
ML_DH = 128
ML_GATES = 16
ML_N = 4 * ML_W + 128
RW_W = 512
RW_HEADS = 8
RW_DH = 64
RW_LORA = 64
RW_LORA_G = 160
RW_LN_EPS = 64e-5
RW_COLS = 3 * RW_W + 2 * RW_LORA + 2 * RW_LORA + RW_LORA_G
RW_N = 2048
RW_PAIRS = RW_HEADS // 2
N_EXPERTS = 16
EC_FACTOR = 2
D_EXPERT = 2816

LANES = 128
SUBLANES = 8
HALO = SUBLANES
TM = 256
RW_CHUNK = 64
RW_PAIRS_PER_STEP = 4
RW_BATCH_PER_STEP = 4
RW_INV_BASE = 4
FFN_TF = 256
BF16_ROWS = 16
GATHER_WIN = 64
COMBINE_WIN = 80
VMEM_LIMIT = 56 * 1024 * 1024

NN = (((1,), (0,)), ((), ()))
NT = (((1,), (1,)), ((), ()))
TN = (((0,), (0,)), ((), ()))


def _cparams(n_axes):
    return pltpu.CompilerParams(
        dimension_semantics=("arbitrary",) * n_axes, vmem_limit_bytes=VMEM_LIMIT)


def _dot(a, b, dn=NN):
    return lax.dot_general(a, b, dn, preferred_element_type=F32)


def _split2(a):
    hi = a.astype(BF16)
    lo = (a - hi.astype(F32)).astype(BF16)
    return hi, lo


def _split3(a):
    hi = a.astype(BF16)
    r1 = a - hi.astype(F32)
    mid = r1.astype(BF16)
    lo = (r1 - mid.astype(F32)).astype(BF16)
    return hi, mid, lo


def _dot1(a, b, dn=NN):
    return _dot(a.astype(BF16), b.astype(BF16), dn)


def _dot3(a, b, dn=NN):
    ah, al = _split2(a)
    bh, bl = _split2(b)
    return _dot(ah, bh, dn) + (_dot(ah, bl, dn) + _dot(al, bh, dn))


def _dot_sel(sel_bf16, b, dn=NN):
    bh, bm, bl = _split3(b)
    return _dot(sel_bf16, bh, dn) + (_dot(sel_bf16, bm, dn) + _dot(sel_bf16, bl, dn))


def _dot_rsel(a, sel_bf16, dn=NN):
    ah, al = _split2(a)
    return _dot(ah, sel_bf16, dn) + _dot(al, sel_bf16, dn)


def _silu(x):
    return x * jax.nn.sigmoid(x)


def _log_sigmoid(x):
    return jnp.minimum(x, 0.0) - jnp.log1p(jnp.exp(-jnp.abs(x)))


def _rmsnorm(x, g):
    return x * lax.rsqrt(jnp.mean(x * x, axis=-1, keepdims=True) + EPS) * g


def _tri(n, lower, strict, dtype):
    r = lax.broadcasted_iota(I32, (n, n), 0)
    c = lax.broadcasted_iota(I32, (n, n), 1)
    if lower:
        m = (c < r) if strict else (c <= r)
    else:
        m = (c > r) if strict else (c >= r)
    return m if dtype is None else m.astype(dtype)


def _cumsum_rows(x, forward):
    n = x.shape[0]
    row = lax.broadcasted_iota(I32, x.shape, 0)
    s = 1
    while s < n:
        if forward:
            x = x + jnp.where(row >= s, pltpu.roll(x, s, 0), 0.0)
        else:
            x = x + jnp.where(row < n - s, pltpu.roll(x, n - s, 0), 0.0)
        s *= 2
    return x


def _shift_rows(z, prev_row, next_row):
    n = z.shape[0]
    row = lax.broadcasted_iota(I32, z.shape, 0)
    zm1 = jnp.where(row == 0, prev_row, pltpu.roll(z, 1, 0))
    zp1 = jnp.where(row == n - 1, next_row, pltpu.roll(z, n - 1, 0))
    return zm1, zp1


def _mod_kernel(c_ref, w_ref, b_ref, o_ref):
    o_ref[...] = _dot3(_silu(c_ref[...]), w_ref[...]) + b_ref[...]


def _mod_call(cc, mod_w, mod_b):
    n = mod_w.shape[1]
    tn = 1024
    return pl.pallas_call(
        _mod_kernel,
        grid=(n // tn,),
        in_specs=[pl.BlockSpec((16, D_MODEL), lambda j: (0, 0)),
                  pl.BlockSpec((D_MODEL, tn), lambda j: (0, j)),
                  pl.BlockSpec((1, tn), lambda j: (0, j))],
        out_specs=pl.BlockSpec((16, tn), lambda j: (0, j)),
        out_shape=jax.ShapeDtypeStruct((16, n), F32),
        compiler_params=_cparams(1),
        name="mod",
    )(cc, mod_w, mod_b)


def _project_tile(xin, prev_rows, next_rows, sh, sc, g, w_ref):
    xe = jnp.concatenate([prev_rows, xin, next_rows], axis=0)
    xn = _rmsnorm(xe, g) * (1.0 + sc) + sh
    return _dot(xn.astype(BF16), w_ref[...])


def _tile_and_edges(u, seg_first, seg_last):
    z = u[HALO:HALO + TM]
    prev_row = jnp.where(seg_first, 0.0, u[HALO - 1:HALO])
    next_row = jnp.where(seg_last, 0.0, u[HALO + TM:HALO + TM + 1])
    return z, prev_row, next_row


def _ml_in_kernel(ctx_ref, x_ref, xp_ref, xn_ref, sh_ref, sc_ref, g_ref, w_ref, cw_ref, cb_ref, gb_ref,
                  qkv_ref, gate_ref, zo_ref, *, nct, ntot):
    i = pl.program_id(1)
    xin = jnp.where(i < nct, ctx_ref[0], x_ref[0])
    u = _project_tile(xin, xp_ref[0], xn_ref[0], sh_ref[0], sc_ref[0], g_ref[...], w_ref)
    z, prev_row, next_row = _tile_and_edges(
        u, (i == 0) | (i == nct), (i == nct - 1) | (i == ntot - 1))
    zqk = z[:, 0:2 * ML_W]
    zm1, zp1 = _shift_rows(zqk, prev_row[:, 0:2 * ML_W], next_row[:, 0:2 * ML_W])
    cw = cw_ref[...]
    qk = _silu(zm1 * cw[0:1] + zqk * cw[1:2] + zp1 * cw[2:3] + cb_ref[...])
    qkv_ref[0, :, 0:ML_W] = qk[:, 0:ML_W].astype(BF16)
    qkv_ref[0, :, ML_W:2 * ML_W] = (qk[:, ML_W:2 * ML_W] * (ML_DH ** -0.5)).astype(BF16)
    qkv_ref[0, :, 2 * ML_W:3 * ML_W] = z[:, 2 * ML_W:3 * ML_W].astype(BF16)
    zo_ref[0] = z[:, 3 * ML_W:4 * ML_W]
    g = z[:, 4 * ML_W:4 * ML_W + LANES] + gb_ref[...]
    lane = lax.broadcasted_iota(I32, (TM, LANES), 1)
    is_f = ((lane >= 4) & (lane < 8)) | ((lane >= 12) & (lane < 16))
    gate_ref[0] = jnp.where(is_f, _log_sigmoid(g), g)


def _ml_in_call(x, ctx, mods3, norm_g, w, conv_w, conv_b, gate_b):
    B, T, _ = x.shape
    assert ctx.shape[1] == TM
    nct, nlt = 1, T // TM
    ntot = nct + nlt
    hb = TM // HALO
    lat = lambda i: jnp.maximum(i - nct, 0)
    mrow = lambda b, i: jnp.where(i < nct, B, b)
    full = lambda shape: pl.BlockSpec(shape, lambda b, i: (0,) * len(shape))
    return pl.pallas_call(
        functools.partial(_ml_in_kernel, nct=nct, ntot=ntot),
        grid=(B, ntot),
        in_specs=[
            pl.BlockSpec((1, TM, D_MODEL), lambda b, i: (b, 0, 0)),
            pl.BlockSpec((1, TM, D_MODEL), lambda b, i: (b, lat(i), 0)),
            pl.BlockSpec((1, HALO, D_MODEL), lambda b, i: (b, jnp.maximum(lat(i) * hb - 1, 0), 0)),
            pl.BlockSpec((1, HALO, D_MODEL),
                         lambda b, i: (b, jnp.minimum((lat(i) + 1) * hb, nlt * hb - 1), 0)),
            pl.BlockSpec((1, 1, D_MODEL), lambda b, i: (mrow(b, i), 0, 0)),
            pl.BlockSpec((1, 1, D_MODEL), lambda b, i: (mrow(b, i), 0, 1)),
            full((1, D_MODEL)), full((D_MODEL, ML_N)),
            full((3, 2 * ML_W)), full((1, 2 * ML_W)), full((1, LANES)),
        ],
        out_specs=[
            pl.BlockSpec((1, TM, 3 * ML_W), lambda b, i: (b, i, 0)),
            pl.BlockSpec((1, TM, LANES), lambda b, i: (b, i, 0)),
            pl.BlockSpec((1, TM, ML_W), lambda b, i: (b, lat(i), 0)),
        ],
        out_shape=[jax.ShapeDtypeStruct((B, ntot * TM, 3 * ML_W), BF16),
                   jax.ShapeDtypeStruct((B, ntot * TM, LANES), F32),
                   jax.ShapeDtypeStruct((B, nlt * TM, ML_W), F32)],
        compiler_params=_cparams(2),
        name="ml_in",
    )(ctx, x, x, x, mods3, mods3, norm_g, w, conv_w, conv_b, gate_b)


def _seq_chunk(d, j, nct, ntot):
    bwd = jnp.where(j < nct, nct - 1 - j, ntot - 1 - (j - nct))
    return jnp.where(d == 0, j, bwd)


def _mlstm_kernel(qf_ref, gf_ref, qb_ref, gb_ref, of_ref, ob_ref, c_ref, n_ref, m_ref):
    j = pl.program_id(1)
    L = TM

    @pl.when(j == 0)
    def _():
        c_ref[...] = jnp.zeros_like(c_ref)
        n_ref[...] = jnp.zeros_like(n_ref)
        m_ref[...] = jnp.zeros_like(m_ref)

    dirs = ((qf_ref, gf_ref, of_ref, True, 0), (qb_ref, gb_ref, ob_ref, False, 2 * ML_HEADS))
    keep_d = [_tri(L, lower, False, None) for _, _, _, lower, _ in dirs]
    q_all = [g_ref[0] for _, g_ref, _, _, _ in dirs]
    b_all = [_dot_sel(keep_d[di].astype(BF16), q_all[di]) for di in range(2)]
    q_t = [q_all[di].T for di in range(2)]
    b_t = [b_all[di].T for di in range(2)]

    chains = [(di, h) for di in range(2) for h in range(ML_HEADS)]
    cs = range(len(chains))
    sl = [slice(h * ML_DH, (h + 1) * ML_DH) for h in range(ML_HEADS)]
    qkv = [dirs[di][0] for di, _ in chains]
    q = [qkv[c][0, :, sl[h]] for c, (di, h) in enumerate(chains)]
    kb = [qkv[c][0, :, ML_W + h * ML_DH:ML_W + (h + 1) * ML_DH] for c, (di, h) in enumerate(chains)]
    vb = [qkv[c][0, :, 2 * ML_W + h * ML_DH:2 * ML_W + (h + 1) * ML_DH] for c, (di, h) in enumerate(chains)]
    v = [vb[c].astype(F32) for c in cs]
    icol = [dirs[di][4] + h for di, h in chains]
    fcol = [icol[c] + ML_HEADS for c in cs]
    b_col = [b_all[di][:, fcol[c]:fcol[c] + 1] for c, (di, h) in enumerate(chains)]
    li_row = [q_t[di][icol[c]:icol[c] + 1, :] for c, (di, h) in enumerate(chains)]
    b_row = [b_t[di][fcol[c]:fcol[c] + 1, :] for c, (di, h) in enumerate(chains)]
    keep = [keep_d[di] for di, _ in chains]
    m_prev = [m_ref[di, h, 0:1, 0:1] for di, h in chains]
    C = [c_ref[di, h] for di, h in chains]
    n_row = [n_ref[di, h, 0:1, :] for di, h in chains]

    col0 = lax.broadcasted_iota(I32, (L, ML_DH), 1) == 0
    row0 = lax.broadcasted_iota(I32, (ML_DH, ML_DH), 0) == 0
    v_ext = [jnp.concatenate([vb[c], jnp.where(col0, 1.0, 0.0).astype(BF16)], axis=1) for c in cs]
    c_ext = [jnp.concatenate([C[c], jnp.where(row0, n_row[c], 0.0)], axis=0).astype(BF16) for c in cs]
    qk_s = [_dot(q[c], kb[c], NT) for c in cs]
    q_cn = [_dot(q[c], c_ext[c], NT) for c in cs]
    rel = [jnp.where(keep[c], li_row[c] - b_row[c], -jnp.inf) for c in cs]
    inter = [b_col[c] + m_prev[c] for c in cs]
    m_t = [jnp.maximum(inter[c], b_col[c] + jnp.max(rel[c], axis=-1, keepdims=True)) for c in cs]
    wts = [jnp.exp(rel[c] + (b_col[c] - m_t[c])) * qk_s[c] for c in cs]
    w_inter = [jnp.exp(inter[c] - m_t[c]) for c in cs]
    nd = [_dot(wts[c].astype(BF16), v_ext[c]) + w_inter[c] * q_cn[c] for c in cs]
    for c, (di, h) in enumerate(chains):
        den = nd[c][:, ML_DH:ML_DH + 1]
        dirs[di][2][0, :, sl[h]] = nd[c][:, 0:ML_DH] / jnp.maximum(jnp.abs(den), jnp.exp(-m_t[c]))

    b_end = [b_col[c][L - 1:L, :] if dirs[di][3] else b_col[c][0:1, :] for c, (di, h) in enumerate(chains)]
    dec = [b_end[c] - b_row[c] + li_row[c] for c in cs]
    m_new = [jnp.maximum(b_end[c] + m_prev[c], jnp.max(dec[c], axis=-1, keepdims=True)) for c in cs]
    wk = [jnp.exp(dec[c] - m_new[c]) for c in cs]
    s_old = [jnp.exp(b_end[c] + m_prev[c] - m_new[c]) for c in cs]
    c_upd = [_dot((v[c].T * wk[c]).astype(BF16), kb[c]) for c in cs]
    n_upd = [_dot(jnp.broadcast_to(wk[c], (SUBLANES, L)).astype(BF16), kb[c]) for c in cs]
    for c, (di, h) in enumerate(chains):
        c_ref[di, h] = s_old[c] * C[c] + c_upd[c]
        n_ref[di, h] = s_old[c] * n_ref[di, h] + n_upd[c]
        m_ref[di, h] = jnp.broadcast_to(m_new[c], (SUBLANES, LANES))


def _mlstm_call(qkv, gates, B, nct, nlt):
    ntot = nct + nlt
    mirror = lambda j: _seq_chunk(1, j, nct, ntot)
    out_shape = jax.ShapeDtypeStruct((B, nlt * TM, ML_W), F32)
    return pl.pallas_call(
        _mlstm_kernel,
        grid=(B, ntot),
        in_specs=[
            pl.BlockSpec((1, TM, 3 * ML_W), lambda b, j: (b, j, 0)),
            pl.BlockSpec((1, TM, LANES), lambda b, j: (b, j, 0)),
            pl.BlockSpec((1, TM, 3 * ML_W), lambda b, j: (b, mirror(j), 0)),
            pl.BlockSpec((1, TM, LANES), lambda b, j: (b, mirror(j), 0)),
        ],
        out_specs=[
            pl.BlockSpec((1, TM, ML_W), lambda b, j: (b, jnp.maximum(j - nct, 0), 0)),
            pl.BlockSpec((1, TM, ML_W), lambda b, j: (b, jnp.where(j < nct, nlt - 1, mirror(j) - nct), 0)),
        ],
        out_shape=[out_shape, out_shape],
        scratch_shapes=[pltpu.VMEM((2, ML_HEADS, ML_DH, ML_DH), F32),
                        pltpu.VMEM((2, ML_HEADS, SUBLANES, ML_DH), F32),
                        pltpu.VMEM((2, ML_HEADS, SUBLANES, LANES), F32)],
        compiler_params=_cparams(2),
        name="mlstm",
    )(qkv, gates, qkv, gates)


def _rw_in_kernel(ctx_ref, x_ref, xp_ref, xn_ref, sh_ref, sc_ref, g_ref, w_ref,
                  mu_ref, w0_ref, wup_ref, a0_ref, aup_ref, gup_ref, kk_ref, ka_ref, rk_ref,
                  scan_ref, gb_ref, *, nct, ntot, rows):
    i = pl.program_id(1)
    jt = jnp.maximum(i - nct, 0)
    cpt = TM // rows
    tpb = x_ref.shape[2] // cpt
    c0 = (jt % tpb) * cpt if tpb > 1 else 0
    xcm = jnp.concatenate([x_ref[0, :, c0 + c, :] for c in range(cpt)], axis=0)
    xin = jnp.where(i < nct, ctx_ref[0], xcm)
    prev1 = xp_ref[0, SUBLANES - 1:SUBLANES, jnp.maximum(jt * cpt - 1, 0) % SUBLANES, :]
    next1 = xn_ref[0, 0:1, jnp.minimum((jt + 1) * cpt, GRID_W - 1) % SUBLANES, :]
    u = _project_tile(xin, jnp.broadcast_to(prev1, (HALO, D_MODEL)),
                      jnp.broadcast_to(next1, (HALO, D_MODEL)),
                      sh_ref[0], sc_ref[0], g_ref[...], w_ref)
    z, prev_row, next_row = _tile_and_edges(
        u, (i == 0) | (i == nct), (i == nct - 1) | (i == ntot - 1))
    zm1, zp1 = _shift_rows(z, prev_row, next_row)
    mu = mu_ref[...]
    xs = z * (1.0 - mu[0:1] - mu[1:2]) + mu[0:1] * zm1 + mu[1:2] * zp1

    r = xs[:, 0:RW_W]
    k = xs[:, RW_W:2 * RW_W]
    v = xs[:, 2 * RW_W:3 * RW_W]
    wd = xs[:, 3 * RW_W:3 * RW_W + 2 * RW_LORA]
    ad = xs[:, 3 * RW_W + 2 * RW_LORA:3 * RW_W + 4 * RW_LORA]
    gd = xs[:, 3 * RW_W + 4 * RW_LORA:RW_N]

    w_pre = w0_ref[...] + _dot1(jnp.tanh(wd), wup_ref[...])
    log_w = -jax.nn.softplus(-w_pre) - 0.5
    lw = -jnp.exp(log_w)
    a = jax.nn.sigmoid(a0_ref[...] + _dot1(ad, aup_ref[...]))
    g = _dot1(jax.nn.sigmoid(gd), gup_ref[...])

    hr = lax.broadcasted_iota(I32, (RW_W, RW_W), 0) // RW_DH
    hc = lax.broadcasted_iota(I32, (RW_W, RW_W), 1) // RW_DH
    head_ones = (hr == hc).astype(BF16)
    kk = k * kk_ref[...]
    ss = _dot_rsel(kk * kk, head_ones)
    kk = kk / jnp.maximum(jnp.sqrt(ss), 1e-12)
    ka = ka_ref[...]
    k_f = k * (1.0 + (a[:, 0:RW_W] - 1.0) * ka)
    k_b = k * (1.0 + (a[:, RW_W:2 * RW_W] - 1.0) * ka)
    rk = rk_ref[...]
    bonus = _dot_rsel(r * (k_f + k_b) * rk, head_ones) * v

    cols = (r, v, kk, lw[:, 0:RW_W], k_f, kk * a[:, 0:RW_W],
            lw[:, RW_W:2 * RW_W], k_b, kk * a[:, RW_W:2 * RW_W])
    for p in range(RW_PAIRS):
        for qi, arr in enumerate(cols):
            scan_ref[0, p, :, qi * LANES:(qi + 1) * LANES] = arr[:, p * LANES:(p + 1) * LANES]
    gb_ref[0, :, 0:RW_W] = g
    gb_ref[0, :, RW_W:2 * RW_W] = bonus


def _rw_in_call(x, ctx, mods3, norm_g, w, mu, w0, wup, a0, aup, gup, k_k, k_a, r_k):
    B, T, _ = x.shape
    assert ctx.shape[1] == TM
    nct, nlt = 1, T // TM
    ntot = nct + nlt
    rows = T // GRID_W
    cpt = TM // rows
    cb = max(SUBLANES, cpt)
    tpb = cb // cpt
    assert rows % SUBLANES == 0 and GRID_W % cb == 0 and cb % cpt == 0
    lat = lambda c: jnp.maximum(c - nct, 0)
    mrow = lambda b, c: jnp.where(c < nct, B, b)
    full = lambda shape: pl.BlockSpec(shape, lambda b, c: (0,) * len(shape))
    x4 = x.reshape(B, rows, GRID_W, D_MODEL)
    return pl.pallas_call(
        functools.partial(_rw_in_kernel, nct=nct, ntot=ntot, rows=rows),
        grid=(B, ntot),
        in_specs=[
            pl.BlockSpec((1, TM, D_MODEL), lambda b, c: (b, 0, 0)),
            pl.BlockSpec((1, rows, cb, D_MODEL), lambda b, c: (b, 0, lat(c) // tpb, 0)),
            pl.BlockSpec((1, SUBLANES, SUBLANES, D_MODEL),
                         lambda b, c: (b, rows // SUBLANES - 1,
                                       jnp.maximum(lat(c) * cpt - 1, 0) // SUBLANES, 0)),
            pl.BlockSpec((1, SUBLANES, SUBLANES, D_MODEL),
                         lambda b, c: (b, 0, jnp.minimum((lat(c) + 1) * cpt, GRID_W - 1) // SUBLANES, 0)),
            pl.BlockSpec((1, 1, D_MODEL), lambda b, c: (mrow(b, c), 0, 0)),
            pl.BlockSpec((1, 1, D_MODEL), lambda b, c: (mrow(b, c), 0, 1)),
            full((1, D_MODEL)), full((D_MODEL, RW_N)),
            full((2, RW_N)), full((1, 2 * RW_W)), full((2 * RW_LORA, 2 * RW_W)),
            full((1, 2 * RW_W)), full((2 * RW_LORA, 2 * RW_W)), full((256, RW_W)),
            full((1, RW_W)), full((1, RW_W)), full((1, RW_W)),
        ],
        out_specs=[
            pl.BlockSpec((1, RW_PAIRS, TM, 9 * LANES), lambda b, c: (b, 0, c, 0)),
            pl.BlockSpec((1, TM, 2 * RW_W), lambda b, c: (b, jnp.maximum(c - nct, 0), 0)),
        ],
        out_shape=[jax.ShapeDtypeStruct((B, RW_PAIRS, ntot * TM, 9 * LANES), F32),
                   jax.ShapeDtypeStruct((B, nlt * TM, 2 * RW_W), F32)],
        compiler_params=_cparams(2),
        name="rw_in",
    )(ctx, x4, x4, x4, mods3, mods3, norm_g, w, mu, w0, wup, a0, aup, gup, k_k, k_a, r_k)


def _rw_chunks(blks, states, lowers):
    n = len(blks)
    L = blks[0].shape[0]
    assert L == RW_DH
    idx = range(n)
    col = lambda i, c: blks[i][:, c * LANES:(c + 1) * LANES]
    r, v, kk, lw, kd, bh = ([col(i, c) for i in idx] for c in range(6))
    logp = [_cumsum_rows(lw[i], lowers[i]) for i in idx]
    logp_end = [logp[i][L - 1:L, :] if lowers[i] else logp[i][0:1, :] for i in idx]
    p_inv = [jnp.exp(-logp[i]) for i in idx]
    a_t = [-kk[i] * jnp.exp(logp[i] - lw[i]) for i in idx]
    r_t = [r[i] * jnp.exp(logp[i]) for i in idx]

    row = lax.broadcasted_iota(I32, (L, LANES), 0)
    lane = lax.broadcasted_iota(I32, (L, LANES), 1)
    head0 = lane < RW_DH
    src = jnp.where(head0, lane, lane - RW_DH)

    def split(y_pair):
        return jnp.concatenate([jnp.where(head0, y_pair, 0.0), jnp.where(head0, 0.0, y_pair)],
                               axis=0).astype(BF16)

    def mm(x_cat, y_pair):
        return _dot(x_cat.astype(BF16), split(y_pair))

    strict = {lo: (src < row) if lo else (src > row) for lo in set(lowers)}
    incl_c = {lo: (src <= row) if lo else (src >= row) for lo in set(lowers)}
    eye = jnp.where(src == row, 1.0, 0.0)

    st = [states[i].astype(BF16) for i in idx]
    sv = [split(v[i]) for i in idx]
    bk = [jnp.concatenate([split(bh[i] * p_inv[i]), split(kd[i] * p_inv[i])], axis=0) for i in idx]
    g_ar = [_dot(jnp.concatenate([a_t[i], r_t[i]], axis=0).astype(BF16), bk[i], NT) for i in idx]
    g_a = [g_ar[i][0:L] for i in idx]
    g_r = [g_ar[i][L:2 * L] for i in idx]
    n_mat = [jnp.where(strict[lowers[i]], g_a[i][:, 0:LANES], 0.0) for i in idx]
    a_ak = [jnp.where(strict[lowers[i]], g_a[i][:, LANES:2 * LANES], 0.0) for i in idx]
    a_rb = [jnp.where(incl_c[lowers[i]], g_r[i][:, 0:LANES], 0.0) for i in idx]
    a_rk = [jnp.where(incl_c[lowers[i]], g_r[i][:, LANES:2 * LANES], 0.0) for i in idx]
    w_rhs = [_dot(jnp.concatenate([a_t[i], a_ak[i]], axis=1).astype(BF16),
                  jnp.concatenate([st[i], sv[i]], axis=0)) for i in idx]
    base = RW_INV_BASE
    n_d = [jnp.where(row // base == src // base, n_mat[i], 0.0) for i in idx]
    inv = [eye + n_d[i] for i in idx]
    n_d2 = [mm(n_d[i], n_d[i]) for i in idx]
    inv = [inv[i] + mm(inv[i], n_d2[i]) for i in idx]
    blk = base
    while blk < L:
        link = (row // (2 * blk) == src // (2 * blk)) & (row // blk != src // blk)
        t_m = [mm(inv[i], jnp.where(link, n_mat[i], 0.0)) for i in idx]
        inv = [inv[i] + mm(t_m[i], inv[i]) for i in idx]
        blk *= 2
    u = [mm(inv[i], w_rhs[i]) for i in idx]
    uv = [jnp.concatenate([u[i], v[i]], axis=0) for i in idx]
    y = [_dot(jnp.concatenate([r_t[i], a_rb[i], a_rk[i]], axis=1).astype(BF16),
              jnp.concatenate([st[i], split(u[i]), sv[i]], axis=0)) for i in idx]

    rr = lax.broadcasted_iota(I32, (LANES, LANES), 0) // RW_DH
    cc = lax.broadcasted_iota(I32, (LANES, LANES), 1) // RW_DH
    s_new = []
    for i in idx:
        to_end = jnp.exp(logp_end[i] - logp[i])
        bk_end = jnp.concatenate([bh[i] * to_end, kd[i] * to_end], axis=0)
        decay_rows = jnp.broadcast_to(jnp.exp(logp_end[i]), (LANES, LANES)).T
        s_new.append(jnp.where(rr == cc, states[i] * decay_rows + _dot1(bk_end, uv[i], TN), 0.0))
    return y, s_new


def _rw_scan_kernel(f_ref, b_ref, yf_ref, yb_ref, s_ref):
    j = pl.program_id(2)

    @pl.when(j == 0)
    def _():
        s_ref[...] = jnp.zeros_like(s_ref)

    probs = [(bi, p) for bi in range(f_ref.shape[0]) for p in range(f_ref.shape[1])]
    blks, states, lowers = [], [], []
    for bi, p in probs:
        fb = f_ref[bi, p]
        bb = b_ref[bi, p]
        blks += [fb[:, 0:6 * LANES],
                 jnp.concatenate([bb[:, 0:3 * LANES], bb[:, 6 * LANES:9 * LANES]], axis=1)]
        states += [s_ref[bi, p, 0], s_ref[bi, p, 1]]
        lowers += [True, False]
    y, s_new = _rw_chunks(blks, states, lowers)
    for q, (bi, p) in enumerate(probs):
        yf_ref[bi, p] = y[2 * q]
        yb_ref[bi, p] = y[2 * q + 1]
        s_ref[bi, p, 0] = s_new[2 * q]
        s_ref[bi, p, 1] = s_new[2 * q + 1]


def _rw_scan_call(scan_in, B, t_ctx, t_lat):
    L = RW_CHUNK
    nct, nlt = t_ctx // L, t_lat // L
    ntot = nct + nlt
    fchunk = lambda j: j
    bchunk = lambda j: _seq_chunk(1, j, nct, ntot)
    pps = RW_PAIRS_PER_STEP
    bps = RW_BATCH_PER_STEP if B % RW_BATCH_PER_STEP == 0 else 1
    in_blk = (bps, pps, L, 9 * LANES)
    out_blk = (bps, pps, L, LANES)
    out_shape = jax.ShapeDtypeStruct((B, RW_PAIRS, t_lat, LANES), F32)
    return pl.pallas_call(
        _rw_scan_kernel,
        grid=(B // bps, RW_PAIRS // pps, ntot),
        in_specs=[pl.BlockSpec(in_blk, lambda b, p, j: (b, p, fchunk(j), 0)),
                  pl.BlockSpec(in_blk, lambda b, p, j: (b, p, bchunk(j), 0))],
        out_specs=[
            pl.BlockSpec(out_blk, lambda b, p, j: (b, p, jnp.maximum(fchunk(j) - nct, 0), 0)),
            pl.BlockSpec(out_blk,
                         lambda b, p, j: (b, p, jnp.where(j < nct, nlt - 1, bchunk(j) - nct), 0)),
        ],
        out_shape=[out_shape, out_shape],
        scratch_shapes=[pltpu.VMEM((bps, pps, 2, LANES, LANES), F32)],
        compiler_params=_cparams(3),
        name="rw_scan",
    )(scan_in, scan_in)


def _mix_out_kernel(x_ref, hf_ref, hb_ref, zo_ref, yf_ref, yb_ref, gb_ref, g1_ref, sh2_ref, sc2_ref,
                    mlg_ref, lng_ref, lnb_ref, n2g_ref, wo_ref, rw_ref,
                    h1_ref, hn_ref, aff_ref):
    hm = hf_ref[0] + hb_ref[0]
    parts = []
    for h in range(ML_HEADS):
        hh = hm[:, h * ML_DH:(h + 1) * ML_DH]
        parts.append(hh * lax.rsqrt(jnp.mean(hh * hh, axis=-1, keepdims=True) + EPS))
    ml = jnp.concatenate(parts, axis=1) * mlg_ref[...] * jax.nn.sigmoid(zo_ref[0])

    nr = yf_ref.shape[3]
    to_raster = lambda a: jnp.swapaxes(a, 0, 1).reshape(nr * GRID_W, a.shape[-1])
    y = jnp.concatenate([to_raster(yf_ref[0, p] + yb_ref[0, p]) for p in range(RW_PAIRS)], axis=1)
    gbr = to_raster(gb_ref[0])
    hr = lax.broadcasted_iota(I32, (RW_W, RW_W), 0) // RW_DH
    hc = lax.broadcasted_iota(I32, (RW_W, RW_W), 1) // RW_DH
    head_ones = (hr == hc).astype(BF16)
    mean = _dot_rsel(y, head_ones) * (1.0 / RW_DH)
    dy = y - mean
    var = _dot_rsel(dy * dy, head_ones) * (1.0 / RW_DH)
    rw = dy * lax.rsqrt(var + RW_LN_EPS) * lng_ref[...] + lnb_ref[...]
    rw = (rw + gbr[:, RW_W:2 * RW_W]) * gbr[:, 0:RW_W]

    mix = _dot(jnp.concatenate([ml, rw], axis=1).astype(BF16), wo_ref[...])
    h1 = x_ref[0] + g1_ref[0] * mix
    h1_ref[0] = h1
    hn = _rmsnorm(h1, n2g_ref[...]) * (1.0 + sc2_ref[0]) + sh2_ref[0]
    hn_ref[0] = hn.astype(BF16)
    logits = _dot3(hn, rw_ref[...])
    lane = lax.broadcasted_iota(I32, logits.shape, 1)
    logits = jnp.where(lane < N_EXPERTS, logits, -jnp.inf)
    e = jnp.exp(logits - jnp.max(logits, axis=-1, keepdims=True))
    aff = e / jnp.sum(e, axis=-1, keepdims=True)
    aff_ref[0] = aff.T[0:N_EXPERTS, :]


def _mix_out_call(x, h_f, h_b, zo, y_f, y_b, gb, mods3, ml_norm_g, ln_g, ln_b, norm2_g,
                  wo, router_pad):
    B, T, _ = x.shape
    rows = T // GRID_W
    rb = SUBLANES
    tmo = rb * GRID_W
    assert rows % rb == 0
    yv = lambda y: y.reshape(B, RW_PAIRS, GRID_W, rows, LANES)
    gbv = gb.reshape(B, GRID_W, rows, 2 * RW_W)
    row1 = lambda shape: pl.BlockSpec(shape, lambda b, i: (0,) * len(shape))
    mod = lambda k: pl.BlockSpec((1, 1, D_MODEL), lambda b, i: (b, 0, k))
    return pl.pallas_call(
        _mix_out_kernel,
        grid=(B, T // tmo),
        in_specs=[
            pl.BlockSpec((1, tmo, D_MODEL), lambda b, i: (b, i, 0)),
            pl.BlockSpec((1, tmo, ML_W), lambda b, i: (b, i, 0)),
            pl.BlockSpec((1, tmo, ML_W), lambda b, i: (b, i, 0)),
            pl.BlockSpec((1, tmo, ML_W), lambda b, i: (b, i, 0)),
            pl.BlockSpec((1, RW_PAIRS, GRID_W, rb, LANES), lambda b, i: (b, 0, 0, i, 0)),
            pl.BlockSpec((1, RW_PAIRS, GRID_W, rb, LANES), lambda b, i: (b, 0, 0, i, 0)),
            pl.BlockSpec((1, GRID_W, rb, 2 * RW_W), lambda b, i: (b, 0, i, 0)),
            mod(2), mod(3), mod(4),
            row1((1, ML_W)), row1((1, RW_W)), row1((1, RW_W)), row1((1, D_MODEL)),
            row1((D_MODEL, D_MODEL)), row1((D_MODEL, LANES)),
        ],
        out_specs=[
            pl.BlockSpec((1, tmo, D_MODEL), lambda b, i: (b, i, 0)),
            pl.BlockSpec((1, tmo, D_MODEL), lambda b, i: (b, i, 0)),
            pl.BlockSpec((1, N_EXPERTS, tmo), lambda b, i: (b, 0, i)),
        ],
        out_shape=[jax.ShapeDtypeStruct((B, T, D_MODEL), F32),
                   jax.ShapeDtypeStruct((B, T, D_MODEL), BF16),
                   jax.ShapeDtypeStruct((B, N_EXPERTS, T), F32)],
        compiler_params=_cparams(2),
        name="mix_out",
    )(x, h_f, h_b, zo, yv(y_f), yv(y_b), gbv, mods3, mods3, mods3,
      ml_norm_g, ln_g, ln_b, norm2_g, wo, router_pad)


def _route_kernel(aff_ref, pos_ref, st_ref, *, cap):
    a = aff_ref[0]
    T = a.shape[1]
    as_f32 = lambda bits: lax.bitcast_convert_type(bits, F32)

    def body(i, thr):
        cand = thr | jnp.left_shift(jnp.int32(1), 30 - i)
        cnt = jnp.sum(jnp.where(a >= as_f32(cand), 1.0, 0.0), axis=1, keepdims=True)
        return jnp.where(cnt >= cap, cand, thr)

    thr = lax.fori_loop(0, 31, body, jnp.zeros((N_EXPERTS, 1), I32))
    gt = a >= as_f32(thr + 1)
    eq = (a >= as_f32(thr)) & jnp.logical_not(gt)
    need = cap - jnp.sum(jnp.where(gt, 1.0, 0.0), axis=1, keepdims=True)
    tri = _tri(TM, False, False, BF16)

    def prefix_excl(mask):
        outs, carries = [], []
        carry = jnp.zeros((N_EXPERTS, 1), F32)
        for blk in range(T // TM):
            seg = jnp.where(mask[:, blk * TM:(blk + 1) * TM], 1.0, 0.0)
            inc = _dot(seg.astype(BF16), tri)
            outs.append(inc - seg + carry)
            carries.append(carry)
            carry = carry + jnp.sum(seg, axis=1, keepdims=True)
        return jnp.concatenate(outs, axis=1), carries + [carry]

    chosen = gt | (eq & (prefix_excl(eq)[0] < need))
    slot, block_starts = prefix_excl(chosen)
    pos_ref[0] = jnp.where(chosen, slot, -1.0).astype(I32)
    lane = lax.broadcasted_iota(I32, (N_EXPERTS, LANES), 1)
    st = jnp.zeros((N_EXPERTS, LANES), F32)
    for blk, start in enumerate(block_starts):
        st = jnp.where(lane == blk, start, st)
    st_ref[0] = st.astype(I32)


def _route_call(aff_t, cap):
    B, E, T = aff_t.shape
    assert T // TM < LANES
    return pl.pallas_call(
        functools.partial(_route_kernel, cap=cap),
        grid=(B,),
        in_specs=[pl.BlockSpec((1, E, T), lambda b: (b, 0, 0))],
        out_specs=[pl.BlockSpec((1, E, T), lambda b: (b, 0, 0)),
                   pl.BlockSpec((1, E, LANES), lambda b: (b, 0, 0))],
        out_shape=[jax.ShapeDtypeStruct((B, E, T), I32),
                   jax.ShapeDtypeStruct((B, E, LANES), I32)],
        compiler_params=_cparams(1),
        name="route",
    )(aff_t)


def _align_down(s, m):
    sh = m.bit_length() - 1
    return pl.multiple_of(lax.shift_left(lax.shift_right_logical(s, sh), sh), m)


def _gather_kernel(st_ref, pos_ref, hn_ref, o_ref, acc_ref, *, cap):
    b = pl.program_id(0)
    e = pl.program_id(1)
    nb = hn_ref.shape[1] // TM
    acc_ref[...] = jnp.zeros_like(acc_ref)
    win = lax.broadcasted_iota(I32, (GATHER_WIN, TM), 0)
    rows = []
    for k in range(nb):
        row0 = _align_down(st_ref[b, e, k], SUBLANES)
        onehot = jnp.where(pos_ref[0, :, k * TM:(k + 1) * TM] == row0 + win, 1.0, 0.0)
        acc_ref[pl.ds(row0, GATHER_WIN), :] += _dot(onehot.astype(BF16), hn_ref[0, k * TM:(k + 1) * TM, :])
        rows.append(row0)
    slot = lax.broadcasted_iota(I32, (cap, TM), 0)
    for k in range(nb):
        @pl.when(st_ref[b, e, k + 1] > rows[k] + GATHER_WIN)
        def _():
            rest = (pos_ref[0, :, k * TM:(k + 1) * TM] == slot) & (slot >= rows[k] + GATHER_WIN)
            acc_ref[0:cap, :] += _dot(jnp.where(rest, 1.0, 0.0).astype(BF16),
                                      hn_ref[0, k * TM:(k + 1) * TM, :])
    o_ref[0] = acc_ref[0:cap, :].astype(BF16)


def _gather_call(starts, pos, hn, cap):
    B, E, T = pos.shape
    grid_spec = pltpu.PrefetchScalarGridSpec(
        num_scalar_prefetch=1,
        grid=(B, E),
        in_specs=[pl.BlockSpec((1, 1, T), lambda b, e, st: (b * E + e, 0, 0)),
                  pl.BlockSpec((1, T, D_MODEL), lambda b, e, st: (b, 0, 0))],
        out_specs=pl.BlockSpec((1, cap, D_MODEL), lambda b, e, st: (e, b, 0)),
        scratch_shapes=[pltpu.VMEM((cap + GATHER_WIN, D_MODEL), F32)])
    return pl.pallas_call(
        functools.partial(_gather_kernel, cap=cap),
        grid_spec=grid_spec,
        out_shape=jax.ShapeDtypeStruct((E, B * cap, D_MODEL), BF16),
        compiler_params=_cparams(2),
        name="gather",
    )(starts, pos.reshape(B * E, 1, T), hn)


def _ffn_kernel(x_ref, w1_ref, w3_ref, w2_ref, o_ref, acc_ref):
    f = pl.program_id(2)

    @pl.when(f == 0)
    def _():
        acc_ref[...] = jnp.zeros_like(acc_ref)

    x = x_ref[0]
    h1 = _dot(x, w1_ref[0].astype(BF16))
    h3 = _dot(x, w3_ref[0].astype(BF16))
    hid = (_silu(h1) * h3).astype(BF16)
    acc_ref[...] += _dot(hid, w2_ref[0].astype(BF16))

    @pl.when(f == pl.num_programs(2) - 1)
    def _():
        o_ref[0] = acc_ref[...].astype(BF16)


def _ffn_call(xg, w1, w3, w2):
    E, M, _ = xg.shape
    tm = min(M, 2048)
    nf = D_EXPERT // FFN_TF
    return pl.pallas_call(
        _ffn_kernel,
        grid=(E, M // tm, nf),
        in_specs=[pl.BlockSpec((1, tm, D_MODEL), lambda e, m, f: (e, m, 0)),
                  pl.BlockSpec((1, D_MODEL, FFN_TF), lambda e, m, f: (e, 0, f)),
                  pl.BlockSpec((1, D_MODEL, FFN_TF), lambda e, m, f: (e, 0, f)),
                  pl.BlockSpec((1, FFN_TF, D_MODEL), lambda e, m, f: (e, f, 0))],
        out_specs=pl.BlockSpec((1, tm, D_MODEL), lambda e, m, f: (e, m, 0)),
        out_shape=jax.ShapeDtypeStruct((E, M, D_MODEL), BF16),
        scratch_shapes=[pltpu.VMEM((tm, D_MODEL), F32)],
        compiler_params=_cparams(3),
        name="ffn",
    )(xg, w1, w3, w2)


def _combine_kernel(st_ref, pos_ref, aff_ref, y_ref, h1_ref, g2_ref, fg_ref, o_ref, moe_ref, *, cap):
    b = pl.program_id(0)
    k = pl.program_id(1)
    pos = pos_ref[0]
    aff = aff_ref[0]
    win = lax.broadcasted_iota(I32, (COMBINE_WIN, TM), 0)
    ws, ys, rows = [], [], []
    for e in range(N_EXPERTS):
        row0 = jnp.minimum(_align_down(st_ref[b, e, k], BF16_ROWS), cap - COMBINE_WIN)
        row0 = pl.multiple_of(row0, BF16_ROWS)
        ws.append(jnp.where(pos[e:e + 1, :] == row0 + win, aff[e:e + 1, :], 0.0).astype(BF16))
        ys.append(y_ref[e, pl.ds(row0, COMBINE_WIN), :])
        rows.append(row0)
    moe_ref[...] = _dot(jnp.concatenate(ws, axis=0), jnp.concatenate(ys, axis=0), TN)
    slot = lax.broadcasted_iota(I32, (cap, TM), 0)
    for e in range(N_EXPERTS):
        @pl.when(st_ref[b, e, k + 1] > rows[e] + COMBINE_WIN)
        def _():
            rest = (pos[e:e + 1, :] == slot) & (slot >= rows[e] + COMBINE_WIN)
            moe_ref[...] += _dot(jnp.where(rest, aff[e:e + 1, :], 0.0).astype(BF16), y_ref[e], TN)
    h2 = h1_ref[0] + g2_ref[0] * moe_ref[...]
    o_ref[0] = _rmsnorm(h2, fg_ref[...])


def _combine_call(starts, pos, aff_t, ys, h1, mods3, final_g, cap):
    B, E, T = pos.shape
    assert cap >= COMBINE_WIN and (cap - COMBINE_WIN) % BF16_ROWS == 0
    grid_spec = pltpu.PrefetchScalarGridSpec(
        num_scalar_prefetch=1,
        grid=(B, T // TM),
        in_specs=[pl.BlockSpec((1, E, TM), lambda b, i, st: (b, 0, i)),
                  pl.BlockSpec((1, E, TM), lambda b, i, st: (b, 0, i)),
                  pl.BlockSpec((E, cap, D_MODEL), lambda b, i, st: (0, b, 0)),
                  pl.BlockSpec((1, TM, D_MODEL), lambda b, i, st: (b, i, 0)),
                  pl.BlockSpec((1, 1, D_MODEL), lambda b, i, st: (b, 0, 5)),
                  pl.BlockSpec((1, D_MODEL), lambda b, i, st: (0, 0))],
        out_specs=pl.BlockSpec((1, TM, D_MODEL), lambda b, i, st: (b, i, 0)),
        scratch_shapes=[pltpu.VMEM((TM, D_MODEL), F32)])
    return pl.pallas_call(
        functools.partial(_combine_kernel, cap=cap),
        grid_spec=grid_spec,
        out_shape=jax.ShapeDtypeStruct((B, T, D_MODEL), F32),
        compiler_params=_cparams(2),
        name="combine",
    )(starts, pos, aff_t, ys, h1, mods3, final_g)


def _pad_cols(w, n):
    return jnp.pad(w, ((0, 0), (0, n - w.shape[1])))


def _both_dirs(up):
    z = jnp.zeros_like(up[0])
    return jnp.concatenate([jnp.concatenate([up[0], z], axis=1),
                            jnp.concatenate([z, up[1]], axis=1)], axis=0)


def kernel(x, c, ctx, c_ctx, mod_w, mod_b, norm1_g, w_in, ml_conv_w, ml_conv_b, ml_gate_b, ml_norm_g,
           rw_mu, rw_w0, rw_w_up, rw_a0, rw_a_up, rw_g_up, rw_k_k, rw_k_a, rw_r_k, rw_ln_g, rw_ln_b,
           w_out, norm2_g, router_w, exp_w1, exp_w3, exp_w2, final_g):
    B, T, D = x.shape
    t_ctx = ctx.shape[1]
    assert D == D_MODEL and T % TM == 0 and t_ctx % TM == 0 and TM % (T // GRID_W) == 0
    assert mod_w.shape[0] == 1 and B < 16
    nct, nlt = t_ctx // TM, T // TM
    cap = EC_FACTOR * T // N_EXPERTS
    ml_cols = 4 * ML_W + ML_GATES

    cc = jnp.concatenate([c, c_ctx[None, :], jnp.zeros((16 - B - 1, D), F32)], axis=0)
    mods = _mod_call(cc, mod_w[0], mod_b)
    mods3 = mods.reshape(16, 1, 6 * D)
    g1n = norm1_g.reshape(1, D)

    w_ml = _pad_cols(w_in[0, :, :ml_cols], ML_N).astype(BF16)
    w_rw = _pad_cols(w_in[0, :, ml_cols:], RW_N).astype(BF16)
    qkv, gates, zo = _ml_in_call(x, ctx, mods3, g1n, w_ml, ml_conv_w[0], ml_conv_b,
                                 _pad_cols(ml_gate_b, LANES))
    h_f, h_b = _mlstm_call(qkv, gates, B, nct, nlt)

    gup = jnp.pad(rw_g_up[0], ((0, 256 - RW_LORA_G), (0, 0)))
    scan_in, gb = _rw_in_call(
        x, ctx, mods3, g1n, w_rw,
        _pad_cols(rw_mu[0], RW_N), rw_w0[0].reshape(1, 2 * RW_W), _both_dirs(rw_w_up[0]),
        rw_a0[0].reshape(1, 2 * RW_W), _both_dirs(rw_a_up[0]), gup,
        rw_k_k, rw_k_a, rw_r_k[0].reshape(1, RW_W))
    y_f, y_b = _rw_scan_call(scan_in, B, t_ctx, T)

    h1, hn, aff_t = _mix_out_call(
        x, h_f, h_b, zo, y_f, y_b, gb, mods3, ml_norm_g, rw_ln_g, rw_ln_b, norm2_g,
        w_out[0].astype(BF16), _pad_cols(router_w[0], LANES))

    pos, starts = _route_call(aff_t, cap)
    starts = starts[:, :, :nlt + 1]
    xg = _gather_call(starts, pos, hn, cap)
    ys = _ffn_call(xg, exp_w1[0], exp_w3[0], exp_w2[0])
    return _combine_call(starts, pos, aff_t, ys, h1, mods3, final_g.reshape(1, D), cap)
```

```python
import functools
import math

import jax
import jax.numpy as jnp
from jax import lax
from jax.experimental import pallas as pl
from jax.experimental.pallas import tpu as pltpu

F32 = jnp.float32
BF16 = jnp.bfloat16
I32 = jnp.int32

D_MODEL = 1024
GRID_W = 64
EPS = 1e-6
ML_W = 512
ML_HEADS = 4
ML_DH = 128
ML_GATES = 16
ML_N = 4 * ML_W + 128
RW_W = 512
RW_HEADS = 8
RW_DH = 64
RW_LORA = 64
RW_LORA_G = 160
RW_LN_EPS = 64e-5
RW_COLS = 3 * RW_W + 2 * RW_LORA + 2 * RW_LORA + RW_LORA_G
RW_N = 2048
RW_PAIRS = RW_HEADS // 2
N_EXPERTS = 16
EC_FACTOR = 2
D_EXPERT = 2816

LANES = 128
SUBLANES = 8
HALO = SUBLANES
TM = 256
RW_CHUNK = 64
RW_PAIRS_PER_STEP = 4
RW_BATCH_PER_STEP = 4
RW_INV_BASE = 4
FFN_TF = 256
BF16_ROWS = 16
GATHER_WIN = 64
COMBINE_WIN = 80
VMEM_LIMIT = 56 * 1024 * 1024

NN = (((1,), (0,)), ((), ()))
NT = (((1,), (1,)), ((), ()))
TN = (((0,), (0,)), ((), ()))


def _cparams(n_axes):
    return pltpu.CompilerParams(
        dimension_semantics=("arbitrary",) * n_axes, vmem_limit_bytes=VMEM_LIMIT)


def _dot(a, b, dn=NN):
    return lax.dot_general(a, b, dn, preferred_element_type=F32)


def _split2(a):
    hi = a.astype(BF16)
    lo = (a - hi.astype(F32)).astype(BF16)
    return hi, lo


def _split3(a):
    hi = a.astype(BF16)
    r1 = a - hi.astype(F32)
    mid = r1.astype(BF16)
    lo = (r1 - mid.astype(F32)).astype(BF16)
    return hi, mid, lo


def _dot1(a, b, dn=NN):
    return _dot(a.astype(BF16), b.astype(BF16), dn)


def _dot3(a, b, dn=NN):
    ah, al = _split2(a)
    bh, bl = _split2(b)
    return _dot(ah, bh, dn) + (_dot(ah, bl, dn) + _dot(al, bh, dn))


def _dot_sel(sel_bf16, b, dn=NN):
    bh, bm, bl = _split3(b)
    return _dot(sel_bf16, bh, dn) + (_dot(sel_bf16, bm, dn) + _dot(sel_bf16, bl, dn))


def _dot_rsel(a, sel_bf16, dn=NN):
    ah, al = _split2(a)
    return _dot(ah, sel_bf16, dn) + _dot(al, sel_bf16, dn)


def _silu(x):
    return x * jax.nn.sigmoid(x)


def _log_sigmoid(x):
    return jnp.minimum(x, 0.0) - jnp.log1p(jnp.exp(-jnp.abs(x)))


def _rmsnorm(x, g):
    return x * lax.rsqrt(jnp.mean(x * x, axis=-1, keepdims=True) + EPS) * g


def _tri(n, lower, strict, dtype):
    r = lax.broadcasted_iota(I32, (n, n), 0)
    c = lax.broadcasted_iota(I32, (n, n), 1)
    if lower:
        m = (c < r) if strict else (c <= r)
    else:
        m = (c > r) if strict else (c >= r)
    return m if dtype is None else m.astype(dtype)


def _cumsum_rows(x, forward):
    n = x.shape[0]
    row = lax.broadcasted_iota(I32, x.shape, 0)
    s = 1
    while s < n:
        if forward:
            x = x + jnp.where(row >= s, pltpu.roll(x, s, 0), 0.0)
        else:
            x = x + jnp.where(row < n - s, pltpu.roll(x, n - s, 0), 0.0)
        s *= 2
    return x


def _shift_rows(z, prev_row, next_row):
    n = z.shape[0]
    row = lax.broadcasted_iota(I32, z.shape, 0)
    zm1 = jnp.where(row == 0, prev_row, pltpu.roll(z, 1, 0))
    zp1 = jnp.where(row == n - 1, next_row, pltpu.roll(z, n - 1, 0))
    return zm1, zp1


def _mod_kernel(c_ref, w_ref, b_ref, o_ref):
    o_ref[...] = _dot3(_silu(c_ref[...]), w_ref[...]) + b_ref[...]


def _mod_call(cc, mod_w, mod_b):
    n = mod_w.shape[1]
    tn = 1024
    return pl.pallas_call(
        _mod_kernel,
        grid=(n // tn,),
        in_specs=[pl.BlockSpec((16, D_MODEL), lambda j: (0, 0)),
                  pl.BlockSpec((D_MODEL, tn), lambda j: (0, j)),
                  pl.BlockSpec((1, tn), lambda j: (0, j))],
        out_specs=pl.BlockSpec((16, tn), lambda j: (0, j)),
        out_shape=jax.ShapeDtypeStruct((16, n), F32),
        compiler_params=_cparams(1),
        name="mod",
    )(cc, mod_w, mod_b)


def _project_tile(xin, prev_rows, next_rows, sh, sc, g, w_ref):
    xe = jnp.concatenate([prev_rows, xin, next_rows], axis=0)
    xn = _rmsnorm(xe, g) * (1.0 + sc) + sh
    return _dot(xn.astype(BF16), w_ref[...])


def _tile_and_edges(u, seg_first, seg_last):
    z = u[HALO:HALO + TM]
    prev_row = jnp.where(seg_first, 0.0, u[HALO - 1:HALO])
    next_row = jnp.where(seg_last, 0.0, u[HALO + TM:HALO + TM + 1])
    return z, prev_row, next_row


def _ml_in_kernel(ctx_ref, x_ref, xp_ref, xn_ref, sh_ref, sc_ref, g_ref, w_ref, cw_ref, cb_ref, gb_ref,
                  qkv_ref, gate_ref, zo_ref, *, nct, ntot):
    i = pl.program_id(1)
    xin = jnp.where(i < nct, ctx_ref[0], x_ref[0])
    u = _project_tile(xin, xp_ref[0], xn_ref[0], sh_ref[0], sc_ref[0], g_ref[...], w_ref)
    z, prev_row, next_row = _tile_and_edges(
        u, (i == 0) | (i == nct), (i == nct - 1) | (i == ntot - 1))
    zqk = z[:, 0:2 * ML_W]
    zm1, zp1 = _shift_rows(zqk, prev_row[:, 0:2 * ML_W], next_row[:, 0:2 * ML_W])
    cw = cw_ref[...]
    qk = _silu(zm1 * cw[0:1] + zqk * cw[1:2] + zp1 * cw[2:3] + cb_ref[...])
    qkv_ref[0, :, 0:ML_W] = qk[:, 0:ML_W].astype(BF16)
    qkv_ref[0, :, ML_W:2 * ML_W] = (qk[:, ML_W:2 * ML_W] * (ML_DH ** -0.5)).astype(BF16)
    qkv_ref[0, :, 2 * ML_W:3 * ML_W] = z[:, 2 * ML_W:3 * ML_W].astype(BF16)
    zo_ref[0] = z[:, 3 * ML_W:4 * ML_W]
    g = z[:, 4 * ML_W:4 * ML_W + LANES] + gb_ref[...]
    lane = lax.broadcasted_iota(I32, (TM, LANES), 1)
    is_f = ((lane >= 4) & (lane < 8)) | ((lane >= 12) & (lane < 16))
    gate_ref[0] = jnp.where(is_f, _log_sigmoid(g), g)


def _ml_in_call(x, ctx, mods3, norm_g, w, conv_w, conv_b, gate_b):
    B, T, _ = x.shape
    assert ctx.shape[1] == TM
    nct, nlt = 1, T // TM
    ntot = nct + nlt
    hb = TM // HALO
    lat = lambda i: jnp.maximum(i - nct, 0)
    mrow = lambda b, i: jnp.where(i < nct, B, b)
    full = lambda shape: pl.BlockSpec(shape, lambda b, i: (0,) * len(shape))
    return pl.pallas_call(
        functools.partial(_ml_in_kernel, nct=nct, ntot=ntot),
        grid=(B, ntot),
        in_specs=[
            pl.BlockSpec((1, TM, D_MODEL), lambda b, i: (b, 0, 0)),
            pl.BlockSpec((1, TM, D_MODEL), lambda b, i: (b, lat(i), 0)),
            pl.BlockSpec((1, HALO, D_MODEL), lambda b, i: (b, jnp.maximum(lat(i) * hb - 1, 0), 0)),
            pl.BlockSpec((1, HALO, D_MODEL),
                         lambda b, i: (b, jnp.minimum((lat(i) + 1) * hb, nlt * hb - 1), 0)),
            pl.BlockSpec((1, 1, D_MODEL), lambda b, i: (mrow(b, i), 0, 0)),
            pl.BlockSpec((1, 1, D_MODEL), lambda b, i: (mrow(b, i), 0, 1)),
            full((1, D_MODEL)), full((D_MODEL, ML_N)),
            full((3, 2 * ML_W)), full((1, 2 * ML_W)), full((1, LANES)),
        ],
        out_specs=[
            pl.BlockSpec((1, TM, 3 * ML_W), lambda b, i: (b, i, 0)),
            pl.BlockSpec((1, TM, LANES), lambda b, i: (b, i, 0)),
            pl.BlockSpec((1, TM, ML_W), lambda b, i: (b, lat(i), 0)),
        ],
        out_shape=[jax.ShapeDtypeStruct((B, ntot * TM, 3 * ML_W), BF16),
                   jax.ShapeDtypeStruct((B, ntot * TM, LANES), F32),
                   jax.ShapeDtypeStruct((B, nlt * TM, ML_W), F32)],
        compiler_params=_cparams(2),
        name="ml_in",
    )(ctx, x, x, x, mods3, mods3, norm_g, w, conv_w, conv_b, gate_b)


def _seq_chunk(d, j, nct, ntot):
    bwd = jnp.where(j < nct, nct - 1 - j, ntot - 1 - (j - nct))
    return jnp.where(d == 0, j, bwd)


def _mlstm_kernel(qf_ref, gf_ref, qb_ref, gb_ref, of_ref, ob_ref, c_ref, n_ref, m_ref):
    j = pl.program_id(1)
    L = TM

    @pl.when(j == 0)
    def _():
        c_ref[...] = jnp.zeros_like(c_ref)
        n_ref[...] = jnp.zeros_like(n_ref)
        m_ref[...] = jnp.zeros_like(m_ref)

    dirs = ((qf_ref, gf_ref, of_ref, True, 0), (qb_ref, gb_ref, ob_ref, False, 2 * ML_HEADS))
    keep_d = [_tri(L, lower, False, None) for _, _, _, lower, _ in dirs]
    q_all = [g_ref[0] for _, g_ref, _, _, _ in dirs]
    b_all = [_dot_sel(keep_d[di].astype(BF16), q_all[di]) for di in range(2)]
    q_t = [q_all[di].T for di in range(2)]
    b_t = [b_all[di].T for di in range(2)]

    chains = [(di, h) for di in range(2) for h in range(ML_HEADS)]
    cs = range(len(chains))
    sl = [slice(h * ML_DH, (h + 1) * ML_DH) for h in range(ML_HEADS)]
    qkv = [dirs[di][0] for di, _ in chains]
    q = [qkv[c][0, :, sl[h]] for c, (di, h) in enumerate(chains)]
    kb = [qkv[c][0, :, ML_W + h * ML_DH:ML_W + (h + 1) * ML_DH] for c, (di, h) in enumerate(chains)]
    vb = [qkv[c][0, :, 2 * ML_W + h * ML_DH:2 * ML_W + (h + 1) * ML_DH] for c, (di, h) in enumerate(chains)]
    v = [vb[c].astype(F32) for c in cs]
    icol = [dirs[di][4] + h for di, h in chains]
    fcol = [icol[c] + ML_HEADS for c in cs]
    b_col = [b_all[di][:, fcol[c]:fcol[c] + 1] for c, (di, h) in enumerate(chains)]
    li_row = [q_t[di][icol[c]:icol[c] + 1, :] for c, (di, h) in enumerate(chains)]
    b_row = [b_t[di][fcol[c]:fcol[c] + 1, :] for c, (di, h) in enumerate(chains)]
    keep = [keep_d[di] for di, _ in chains]
    m_prev = [m_ref[di, h, 0:1, 0:1] for di, h in chains]
    C = [c_ref[di, h] for di, h in chains]
    n_row = [n_ref[di, h, 0:1, :] for di, h in chains]

    col0 = lax.broadcasted_iota(I32, (L, ML_DH), 1) == 0
    row0 = lax.broadcasted_iota(I32, (ML_DH, ML_DH), 0) == 0
    v_ext = [jnp.concatenate([vb[c], jnp.where(col0, 1.0, 0.0).astype(BF16)], axis=1) for c in cs]
    c_ext = [jnp.concatenate([C[c], jnp.where(row0, n_row[c], 0.0)], axis=0).astype(BF16) for c in cs]
    qk_s = [_dot(q[c], kb[c], NT) for c in cs]
    q_cn = [_dot(q[c], c_ext[c], NT) for c in cs]
    rel = [jnp.where(keep[c], li_row[c] - b_row[c], -jnp.inf) for c in cs]
    inter = [b_col[c] + m_prev[c] for c in cs]
    m_t = [jnp.maximum(inter[c], b_col[c] + jnp.max(rel[c], axis=-1, keepdims=True)) for c in cs]
    wts = [jnp.exp(rel[c] + (b_col[c] - m_t[c])) * qk_s[c] for c in cs]
    w_inter = [jnp.exp(inter[c] - m_t[c]) for c in cs]
    nd = [_dot(wts[c].astype(BF16), v_ext[c]) + w_inter[c] * q_cn[c] for c in cs]
    for c, (di, h) in enumerate(chains):
        den = nd[c][:, ML_DH:ML_DH + 1]
        dirs[di][2][0, :, sl[h]] = nd[c][:, 0:ML_DH] / jnp.maximum(jnp.abs(den), jnp.exp(-m_t[c]))

    b_end = [b_col[c][L - 1:L, :] if dirs[di][3] else b_col[c][0:1, :] for c, (di, h) in enumerate(chains)]
    dec = [b_end[c] - b_row[c] + li_row[c] for c in cs]
    m_new = [jnp.maximum(b_end[c] + m_prev[c], jnp.max(dec[c], axis=-1, keepdims=True)) for c in cs]
    wk = [jnp.exp(dec[c] - m_new[c]) for c in cs]
    s_old = [jnp.exp(b_end[c] + m_prev[c] - m_new[c]) for c in cs]
    c_upd = [_dot((v[c].T * wk[c]).astype(BF16), kb[c]) for c in cs]
    n_upd = [_dot(jnp.broadcast_to(wk[c], (SUBLANES, L)).astype(BF16), kb[c]) for c in cs]
    for c, (di, h) in enumerate(chains):
        c_ref[di, h] = s_old[c] * C[c] + c_upd[c]
        n_ref[di, h] = s_old[c] * n_ref[di, h] + n_upd[c]
        m_ref[di, h] = jnp.broadcast_to(m_new[c], (SUBLANES, LANES))


def _mlstm_call(qkv, gates, B, nct, nlt):
    ntot = nct + nlt
    mirror = lambda j: _seq_chunk(1, j, nct, ntot)
    out_shape = jax.ShapeDtypeStruct((B, nlt * TM, ML_W), F32)
    return pl.pallas_call(
        _mlstm_kernel,
        grid=(B, ntot),
        in_specs=[
            pl.BlockSpec((1, TM, 3 * ML_W), lambda b, j: (b, j, 0)),
            pl.BlockSpec((1, TM, LANES), lambda b, j: (b, j, 0)),
            pl.BlockSpec((1, TM, 3 * ML_W), lambda b, j: (b, mirror(j), 0)),
            pl.BlockSpec((1, TM, LANES), lambda b, j: (b, mirror(j), 0)),
        ],
        out_specs=[
            pl.BlockSpec((1, TM, ML_W), lambda b, j: (b, jnp.maximum(j - nct, 0), 0)),
            pl.BlockSpec((1, TM, ML_W), lambda b, j: (b, jnp.where(j < nct, nlt - 1, mirror(j) - nct), 0)),
        ],
        out_shape=[out_shape, out_shape],
        scratch_shapes=[pltpu.VMEM((2, ML_HEADS, ML_DH, ML_DH), F32),
                        pltpu.VMEM((2, ML_HEADS, SUBLANES, ML_DH), F32),
                        pltpu.VMEM((2, ML_HEADS, SUBLANES, LANES), F32)],
        compiler_params=_cparams(2),
        name="mlstm",
    )(qkv, gates, qkv, gates)


def _rw_in_kernel(ctx_ref, x_ref, xp_ref, xn_ref, sh_ref, sc_ref, g_ref, w_ref,
                  mu_ref, w0_ref, wup_ref, a0_ref, aup_ref, gup_ref, kk_ref, ka_ref, rk_ref,
                  scan_ref, gb_ref, *, nct, ntot, rows):
    i = pl.program_id(1)
    jt = jnp.maximum(i - nct, 0)
    cpt = TM // rows
    tpb = x_ref.shape[2] // cpt
    c0 = (jt % tpb) * cpt if tpb > 1 else 0
    xcm = jnp.concatenate([x_ref[0, :, c0 + c, :] for c in range(cpt)], axis=0)
    xin = jnp.where(i < nct, ctx_ref[0], xcm)
    prev1 = xp_ref[0, SUBLANES - 1:SUBLANES, jnp.maximum(jt * cpt - 1, 0) % SUBLANES, :]
    next1 = xn_ref[0, 0:1, jnp.minimum((jt + 1) * cpt, GRID_W - 1) % SUBLANES, :]
    u = _project_tile(xin, jnp.broadcast_to(prev1, (HALO, D_MODEL)),
                      jnp.broadcast_to(next1, (HALO, D_MODEL)),
                      sh_ref[0], sc_ref[0], g_ref[...], w_ref)
    z, prev_row, next_row = _tile_and_edges(
        u, (i == 0) | (i == nct), (i == nct - 1) | (i == ntot - 1))
    zm1, zp1 = _shift_rows(z, prev_row, next_row)
    mu = mu_ref[...]
    xs = z * (1.0 - mu[0:1] - mu[1:2]) + mu[0:1] * zm1 + mu[1:2] * zp1

    r = xs[:, 0:RW_W]
    k = xs[:, RW_W:2 * RW_W]
    v = xs[:, 2 * RW_W:3 * RW_W]
    wd = xs[:, 3 * RW_W:3 * RW_W + 2 * RW_LORA]
    ad = xs[:, 3 * RW_W + 2 * RW_LORA:3 * RW_W + 4 * RW_LORA]
    gd = xs[:, 3 * RW_W + 4 * RW_LORA:RW_N]

    w_pre = w0_ref[...] + _dot1(jnp.tanh(wd), wup_ref[...])
    lw = jax.nn.sigmoid(w_pre) * (-math.exp(-0.5))
    a = jax.nn.sigmoid(a0_ref[...] + _dot1(ad, aup_ref[...]))
    g = _dot1(jax.nn.sigmoid(gd), gup_ref[...])

    hr = lax.broadcasted_iota(I32, (RW_W, RW_W), 0) // RW_DH
    hc = lax.broadcasted_iota(I32, (RW_W, RW_W), 1) // RW_DH
    head_ones = (hr == hc).astype(BF16)
    kk = k * kk_ref[...]
    ss = _dot_rsel(kk * kk, head_ones)
    kk = kk * jnp.minimum(lax.rsqrt(ss), 1e12)
    ka = ka_ref[...]
    k_f = k * (1.0 + (a[:, 0:RW_W] - 1.0) * ka)
    k_b = k * (1.0 + (a[:, RW_W:2 * RW_W] - 1.0) * ka)
    rk = rk_ref[...]
    bonus = _dot_rsel(r * (k_f + k_b) * rk, head_ones) * v

    cols = (r, v, kk, lw[:, 0:RW_W], k_f, kk * a[:, 0:RW_W],
            lw[:, RW_W:2 * RW_W], k_b, kk * a[:, RW_W:2 * RW_W])
    for p in range(RW_PAIRS):
        for qi, arr in enumerate(cols):
            scan_ref[0, p, :, qi * LANES:(qi + 1) * LANES] = arr[:, p * LANES:(p + 1) * LANES]
    gb_ref[0, :, 0:RW_W] = g
    gb_ref[0, :, RW_W:2 * RW_W] = bonus


def _rw_in_call(x, ctx, mods3, norm_g, w, mu, w0, wup, a0, aup, gup, k_k, k_a, r_k):
    B, T, _ = x.shape
    assert ctx.shape[1] == TM
    nct, nlt = 1, T // TM
    ntot = nct + nlt
    rows = T // GRID_W
    cpt = TM // rows
    cb = max(SUBLANES, cpt)
    tpb = cb // cpt
    assert rows % SUBLANES == 0 and GRID_W % cb == 0 and cb % cpt == 0
    lat = lambda c: jnp.maximum(c - nct, 0)
    mrow = lambda b, c: jnp.where(c < nct, B, b)
    full = lambda shape: pl.BlockSpec(shape, lambda b, c: (0,) * len(shape))
    x4 = x.reshape(B, rows, GRID_W, D_MODEL)
    return pl.pallas_call(
        functools.partial(_rw_in_kernel, nct=nct, ntot=ntot, rows=rows),
        grid=(B, ntot),
        in_specs=[
            pl.BlockSpec((1, TM, D_MODEL), lambda b, c: (b, 0, 0)),
            pl.BlockSpec((1, rows, cb, D_MODEL), lambda b, c: (b, 0, lat(c) // tpb, 0)),
            pl.BlockSpec((1, SUBLANES, SUBLANES, D_MODEL),
                         lambda b, c: (b, rows // SUBLANES - 1,
                                       jnp.maximum(lat(c) * cpt - 1, 0) // SUBLANES, 0)),
            pl.BlockSpec((1, SUBLANES, SUBLANES, D_MODEL),
                         lambda b, c: (b, 0, jnp.minimum((lat(c) + 1) * cpt, GRID_W - 1) // SUBLANES, 0)),
            pl.BlockSpec((1, 1, D_MODEL), lambda b, c: (mrow(b, c), 0, 0)),
            pl.BlockSpec((1, 1, D_MODEL), lambda b, c: (mrow(b, c), 0, 1)),
            full((1, D_MODEL)), full((D_MODEL, RW_N)),
            full((2, RW_N)), full((1, 2 * RW_W)), full((2 * RW_LORA, 2 * RW_W)),
            full((1, 2 * RW_W)), full((2 * RW_LORA, 2 * RW_W)), full((256, RW_W)),
            full((1, RW_W)), full((1, RW_W)), full((1, RW_W)),
        ],
        out_specs=[
            pl.BlockSpec((1, RW_PAIRS, TM, 9 * LANES), lambda b, c: (b, 0, c, 0)),
            pl.BlockSpec((1, TM, 2 * RW_W), lambda b, c: (b, jnp.maximum(c - nct, 0), 0)),
        ],
        out_shape=[jax.ShapeDtypeStruct((B, RW_PAIRS, ntot * TM, 9 * LANES), F32),
                   jax.ShapeDtypeStruct((B, nlt * TM, 2 * RW_W), F32)],
        compiler_params=_cparams(2),
        name="rw_in",
    )(ctx, x4, x4, x4, mods3, mods3, norm_g, w, mu, w0, wup, a0, aup, gup, k_k, k_a, r_k)


def _rw_chunks(blks, states, lowers):
    n = len(blks)
    L = blks[0].shape[0]
    assert L == RW_DH
    idx = range(n)
    col = lambda i, c: blks[i][:, c * LANES:(c + 1) * LANES]
    r, v, kk, lw, kd, bh = ([col(i, c) for i in idx] for c in range(6))
    logp = [_cumsum_rows(lw[i], lowers[i]) for i in idx]
    logp_end = [logp[i][L - 1:L, :] if lowers[i] else logp[i][0:1, :] for i in idx]
    p_inv = [jnp.exp(-logp[i]) for i in idx]
    a_t = [-kk[i] * jnp.exp(logp[i] - lw[i]) for i in idx]
    r_t = [r[i] * jnp.exp(logp[i]) for i in idx]

    row = lax.broadcasted_iota(I32, (L, LANES), 0)
    lane = lax.broadcasted_iota(I32, (L, LANES), 1)
    head0 = lane < RW_DH
    src = jnp.where(head0, lane, lane - RW_DH)

    def split(y_pair):
        return jnp.concatenate([jnp.where(head0, y_pair, 0.0), jnp.where(head0, 0.0, y_pair)],
                               axis=0).astype(BF16)

    def mm(x_cat, y_pair):
        return _dot(x_cat.astype(BF16), split(y_pair))

    strict = {lo: (src < row) if lo else (src > row) for lo in set(lowers)}
    incl_c = {lo: (src <= row) if lo else (src >= row) for lo in set(lowers)}
    eye = jnp.where(src == row, 1.0, 0.0)

    st = [states[i].astype(BF16) for i in idx]
    sv = [split(v[i]) for i in idx]
    bk = [jnp.concatenate([split(bh[i] * p_inv[i]), split(kd[i] * p_inv[i])], axis=0) for i in idx]
    g_ar = [_dot(jnp.concatenate([a_t[i], r_t[i]], axis=0).astype(BF16), bk[i], NT) for i in idx]
    g_a = [g_ar[i][0:L] for i in idx]
    g_r = [g_ar[i][L:2 * L] for i in idx]
    n_mat = [jnp.where(strict[lowers[i]], g_a[i][:, 0:LANES], 0.0) for i in idx]
    a_ak = [jnp.where(strict[lowers[i]], g_a[i][:, LANES:2 * LANES], 0.0) for i in idx]
    a_rb = [jnp.where(incl_c[lowers[i]], g_r[i][:, 0:LANES], 0.0) for i in idx]
    a_rk = [jnp.where(incl_c[lowers[i]], g_r[i][:, LANES:2 * LANES], 0.0) for i in idx]
    w_rhs = [_dot(jnp.concatenate([a_t[i], a_ak[i]], axis=1).astype(BF16),
                  jnp.concatenate([st[i], sv[i]], axis=0)) for i in idx]
    base = RW_INV_BASE
    n_d = [jnp.where(row // base == src // base, n_mat[i], 0.0) for i in idx]
    inv = [eye + n_d[i] for i in idx]
    n_d2 = [mm(n_d[i], n_d[i]) for i in idx]
    inv = [inv[i] + mm(inv[i], n_d2[i]) for i in idx]
    blk = base
    while blk < L:
        link = (row // (2 * blk) == src // (2 * blk)) & (row // blk != src // blk)
        t_m = [mm(inv[i], jnp.where(link, n_mat[i], 0.0)) for i in idx]
        inv = [inv[i] + mm(t_m[i], inv[i]) for i in idx]
        blk *= 2
    u = [mm(inv[i], w_rhs[i]) for i in idx]
    uv = [jnp.concatenate([u[i], v[i]], axis=0) for i in idx]
    y = [_dot(jnp.concatenate([r_t[i], a_rb[i], a_rk[i]], axis=1).astype(BF16),
              jnp.concatenate([st[i], split(u[i]), sv[i]], axis=0)) for i in idx]

    rr = lax.broadcasted_iota(I32, (LANES, LANES), 0) // RW_DH
    cc = lax.broadcasted_iota(I32, (LANES, LANES), 1) // RW_DH
    s_new = []
    for i in idx:
        to_end = jnp.exp(logp_end[i] - logp[i])
        bk_end = jnp.concatenate([bh[i] * to_end, kd[i] * to_end], axis=0)
        decay_rows = jnp.broadcast_to(jnp.exp(logp_end[i]), (LANES, LANES)).T
        s_new.append(jnp.where(rr == cc, states[i] * decay_rows + _dot1(bk_end, uv[i], TN), 0.0))
    return y, s_new


def _rw_scan_kernel(f_ref, b_ref, yf_ref, yb_ref, s_ref):
    j = pl.program_id(2)

    @pl.when(j == 0)
    def _():
        s_ref[...] = jnp.zeros_like(s_ref)

    probs = [(bi, p) for bi in range(f_ref.shape[0]) for p in range(f_ref.shape[1])]
    blks, states, lowers = [], [], []
    for bi, p in probs:
        fb = f_ref[bi, p]
        bb = b_ref[bi, p]
        blks += [fb[:, 0:6 * LANES],
                 jnp.concatenate([bb[:, 0:3 * LANES], bb[:, 6 * LANES:9 * LANES]], axis=1)]
        states += [s_ref[bi, p, 0], s_ref[bi, p, 1]]
        lowers += [True, False]
    y, s_new = _rw_chunks(blks, states, lowers)
    for q, (bi, p) in enumerate(probs):
        yf_ref[bi, p] = y[2 * q]
        yb_ref[bi, p] = y[2 * q + 1]
        s_ref[bi, p, 0] = s_new[2 * q]
        s_ref[bi, p, 1] = s_new[2 * q + 1]


def _rw_scan_call(scan_in, B, t_ctx, t_lat):
    L = RW_CHUNK
    nct, nlt = t_ctx // L, t_lat // L
    ntot = nct + nlt
    fchunk = lambda j: j
    bchunk = lambda j: _seq_chunk(1, j, nct, ntot)
    pps = RW_PAIRS_PER_STEP
    bps = RW_BATCH_PER_STEP if B % RW_BATCH_PER_STEP == 0 else 1
    in_blk = (bps, pps, L, 9 * LANES)
    out_blk = (bps, pps, L, LANES)
    out_shape = jax.ShapeDtypeStruct((B, RW_PAIRS, t_lat, LANES), F32)
    return pl.pallas_call(
        _rw_scan_kernel,
        grid=(B // bps, RW_PAIRS // pps, ntot),
        in_specs=[pl.BlockSpec(in_blk, lambda b, p, j: (b, p, fchunk(j), 0)),
                  pl.BlockSpec(in_blk, lambda b, p, j: (b, p, bchunk(j), 0))],
        out_specs=[
            pl.BlockSpec(out_blk, lambda b, p, j: (b, p, jnp.maximum(fchunk(j) - nct, 0), 0)),
            pl.BlockSpec(out_blk,
                         lambda b, p, j: (b, p, jnp.where(j < nct, nlt - 1, bchunk(j) - nct), 0)),
        ],
        out_shape=[out_shape, out_shape],
        scratch_shapes=[pltpu.VMEM((bps, pps, 2, LANES, LANES), F32)],
        compiler_params=_cparams(3),
        name="rw_scan",
    )(scan_in, scan_in)


def _mix_out_kernel(x_ref, hf_ref, hb_ref, zo_ref, yf_ref, yb_ref, gb_ref, g1_ref, sh2_ref, sc2_ref,
                    mlg_ref, lng_ref, lnb_ref, n2g_ref, wo_ref, rw_ref,
                    h1_ref, hn_ref, aff_ref):
    hm = hf_ref[0] + hb_ref[0]
    parts = []
    for h in range(ML_HEADS):
        hh = hm[:, h * ML_DH:(h + 1) * ML_DH]
        parts.append(hh * lax.rsqrt(jnp.mean(hh * hh, axis=-1, keepdims=True) + EPS))
    ml = jnp.concatenate(parts, axis=1) * mlg_ref[...] * jax.nn.sigmoid(zo_ref[0])

    nr = yf_ref.shape[3]
    to_raster = lambda a: jnp.swapaxes(a, 0, 1).reshape(nr * GRID_W, a.shape[-1])
    y = jnp.concatenate([to_raster(yf_ref[0, p] + yb_ref[0, p]) for p in range(RW_PAIRS)], axis=1)
    gbr = to_raster(gb_ref[0])
    hr = lax.broadcasted_iota(I32, (RW_W, RW_W), 0) // RW_DH
    hc = lax.broadcasted_iota(I32, (RW_W, RW_W), 1) // RW_DH
    head_ones = (hr == hc).astype(BF16)
    mean = _dot_rsel(y, head_ones) * (1.0 / RW_DH)
    dy = y - mean
    var = _dot_rsel(dy * dy, head_ones) * (1.0 / RW_DH)
    rw = dy * lax.rsqrt(var + RW_LN_EPS) * lng_ref[...] + lnb_ref[...]
    rw = (rw + gbr[:, RW_W:2 * RW_W]) * gbr[:, 0:RW_W]

    mix = _dot(jnp.concatenate([ml, rw], axis=1).astype(BF16), wo_ref[...])
    h1 = x_ref[0] + g1_ref[0] * mix
    h1_ref[0] = h1
    hn = _rmsnorm(h1, n2g_ref[...]) * (1.0 + sc2_ref[0]) + sh2_ref[0]
    hn_ref[0] = hn.astype(BF16)
    logits = _dot3(hn, rw_ref[...])
    lane = lax.broadcasted_iota(I32, logits.shape, 1)
    logits = jnp.where(lane < N_EXPERTS, logits, -jnp.inf)
    e = jnp.exp(logits - jnp.max(logits, axis=-1, keepdims=True))
    aff = e / jnp.sum(e, axis=-1, keepdims=True)
    aff_ref[0] = aff.T[0:N_EXPERTS, :]


def _mix_out_call(x, h_f, h_b, zo, y_f, y_b, gb, mods3, ml_norm_g, ln_g, ln_b, norm2_g,
                  wo, router_pad):
    B, T, _ = x.shape
    rows = T // GRID_W
    rb = SUBLANES
    tmo = rb * GRID_W
    assert rows % rb == 0
    yv = lambda y: y.reshape(B, RW_PAIRS, GRID_W, rows, LANES)
    gbv = gb.reshape(B, GRID_W, rows, 2 * RW_W)
    row1 = lambda shape: pl.BlockSpec(shape, lambda b, i: (0,) * len(shape))
    mod = lambda k: pl.BlockSpec((1, 1, D_MODEL), lambda b, i: (b, 0, k))
    return pl.pallas_call(
        _mix_out_kernel,
        grid=(B, T // tmo),
        in_specs=[
            pl.BlockSpec((1, tmo, D_MODEL), lambda b, i: (b, i, 0)),
            pl.BlockSpec((1, tmo, ML_W), lambda b, i: (b, i, 0)),
            pl.BlockSpec((1, tmo, ML_W), lambda b, i: (b, i, 0)),
            pl.BlockSpec((1, tmo, ML_W), lambda b, i: (b, i, 0)),
            pl.BlockSpec((1, RW_PAIRS, GRID_W, rb, LANES), lambda b, i: (b, 0, 0, i, 0)),
            pl.BlockSpec((1, RW_PAIRS, GRID_W, rb, LANES), lambda b, i: (b, 0, 0, i, 0)),
            pl.BlockSpec((1, GRID_W, rb, 2 * RW_W), lambda b, i: (b, 0, i, 0)),
            mod(2), mod(3), mod(4),
            row1((1, ML_W)), row1((1, RW_W)), row1((1, RW_W)), row1((1, D_MODEL)),
            row1((D_MODEL, D_MODEL)), row1((D_MODEL, LANES)),
        ],
        out_specs=[
            pl.BlockSpec((1, tmo, D_MODEL), lambda b, i: (b, i, 0)),
            pl.BlockSpec((1, tmo, D_MODEL), lambda b, i: (b, i, 0)),
            pl.BlockSpec((1, N_EXPERTS, tmo), lambda b, i: (b, 0, i)),
        ],
        out_shape=[jax.ShapeDtypeStruct((B, T, D_MODEL), F32),
                   jax.ShapeDtypeStruct((B, T, D_MODEL), BF16),
                   jax.ShapeDtypeStruct((B, N_EXPERTS, T), F32)],
        compiler_params=_cparams(2),
        name="mix_out",
    )(x, h_f, h_b, zo, yv(y_f), yv(y_b), gbv, mods3, mods3, mods3,
      ml_norm_g, ln_g, ln_b, norm2_g, wo, router_pad)


def _route_kernel(aff_ref, pos_ref, st_ref, *, cap):
    a = aff_ref[0]
    T = a.shape[1]
    as_f32 = lambda bits: lax.bitcast_convert_type(bits, F32)

    def body(i, thr):
        cand = thr | jnp.left_shift(jnp.int32(1), 30 - i)
        cnt = jnp.sum(jnp.where(a >= as_f32(cand), 1.0, 0.0), axis=1, keepdims=True)
        return jnp.where(cnt >= cap, cand, thr)

    thr = lax.fori_loop(0, 31, body, jnp.zeros((N_EXPERTS, 1), I32))
    gt = a >= as_f32(thr + 1)
    eq = (a >= as_f32(thr)) & jnp.logical_not(gt)
    need = cap - jnp.sum(jnp.where(gt, 1.0, 0.0), axis=1, keepdims=True)
    tri = _tri(TM, False, False, BF16)

    def prefix_excl(mask):
        outs, carries = [], []
        carry = jnp.zeros((N_EXPERTS, 1), F32)
        for blk in range(T // TM):
            seg = jnp.where(mask[:, blk * TM:(blk + 1) * TM], 1.0, 0.0)
            inc = _dot(seg.astype(BF16), tri)
            outs.append(inc - seg + carry)
            carries.append(carry)
            carry = carry + jnp.sum(seg, axis=1, keepdims=True)
        return jnp.concatenate(outs, axis=1), carries + [carry]

    chosen = gt | (eq & (prefix_excl(eq)[0] < need))
    slot, block_starts = prefix_excl(chosen)
    pos_ref[0] = jnp.where(chosen, slot, -1.0).astype(I32)
    lane = lax.broadcasted_iota(I32, (N_EXPERTS, LANES), 1)
    st = jnp.zeros((N_EXPERTS, LANES), F32)
    for blk, start in enumerate(block_starts):
        st = jnp.where(lane == blk, start, st)
    st_ref[0] = st.astype(I32)


def _route_call(aff_t, cap):
    B, E, T = aff_t.shape
    assert T // TM < LANES
    return pl.pallas_call(
        functools.partial(_route_kernel, cap=cap),
        grid=(B,),
        in_specs=[pl.BlockSpec((1, E, T), lambda b: (b, 0, 0))],
        out_specs=[pl.BlockSpec((1, E, T), lambda b: (b, 0, 0)),
                   pl.BlockSpec((1, E, LANES), lambda b: (b, 0, 0))],
        out_shape=[jax.ShapeDtypeStruct((B, E, T), I32),
                   jax.ShapeDtypeStruct((B, E, LANES), I32)],
        compiler_params=_cparams(1),
        name="route",
    )(aff_t)


def _align_down(s, m):
    sh = m.bit_length() - 1
    return pl.multiple_of(lax.shift_left(lax.shift_right_logical(s, sh), sh), m)


def _gather_kernel(st_ref, pos_ref, hn_ref, o_ref, acc_ref, *, cap):
    b = pl.program_id(0)
    e = pl.program_id(1)
    nb = hn_ref.shape[1] // TM
    acc_ref[...] = jnp.zeros_like(acc_ref)
    win = lax.broadcasted_iota(I32, (GATHER_WIN, TM), 0)
    rows = []
    for k in range(nb):
        row0 = _align_down(st_ref[b, e, k], SUBLANES)
        onehot = jnp.where(pos_ref[0, :, k * TM:(k + 1) * TM] == row0 + win, 1.0, 0.0)
        acc_ref[pl.ds(row0, GATHER_WIN), :] += _dot(onehot.astype(BF16), hn_ref[0, k * TM:(k + 1) * TM, :])
        rows.append(row0)
    slot = lax.broadcasted_iota(I32, (cap, TM), 0)
    for k in range(nb):
        @pl.when(st_ref[b, e, k + 1] > rows[k] + GATHER_WIN)
        def _():
            rest = (pos_ref[0, :, k * TM:(k + 1) * TM] == slot) & (slot >= rows[k] + GATHER_WIN)
            acc_ref[0:cap, :] += _dot(jnp.where(rest, 1.0, 0.0).astype(BF16),
                                      hn_ref[0, k * TM:(k + 1) * TM, :])
    o_ref[0] = acc_ref[0:cap, :].astype(BF16)


def _gather_call(starts, pos, hn, cap):
    B, E, T = pos.shape
    grid_spec = pltpu.PrefetchScalarGridSpec(
        num_scalar_prefetch=1,
        grid=(B, E),
        in_specs=[pl.BlockSpec((1, 1, T), lambda b, e, st: (b * E + e, 0, 0)),
                  pl.BlockSpec((1, T, D_MODEL), lambda b, e, st: (b, 0, 0))],
        out_specs=pl.BlockSpec((1, cap, D_MODEL), lambda b, e, st: (e, b, 0)),
        scratch_shapes=[pltpu.VMEM((cap + GATHER_WIN, D_MODEL), F32)])
    return pl.pallas_call(
        functools.partial(_gather_kernel, cap=cap),
        grid_spec=grid_spec,
        out_shape=jax.ShapeDtypeStruct((E, B * cap, D_MODEL), BF16),
        compiler_params=_cparams(2),
        name="gather",
    )(starts, pos.reshape(B * E, 1, T), hn)


def _ffn_kernel(x_ref, w1_ref, w3_ref, w2_ref, o_ref, acc_ref):
    f = pl.program_id(2)

    @pl.when(f == 0)
    def _():
        acc_ref[...] = jnp.zeros_like(acc_ref)

    x = x_ref[0]
    h1 = _dot(x, w1_ref[0].astype(BF16))
    h3 = _dot(x, w3_ref[0].astype(BF16))
    hid = (_silu(h1) * h3).astype(BF16)
    acc_ref[...] += _dot(hid, w2_ref[0].astype(BF16))

    @pl.when(f == pl.num_programs(2) - 1)
    def _():
        o_ref[0] = acc_ref[...].astype(BF16)


def _ffn_call(xg, w1, w3, w2):
    E, M, _ = xg.shape
    tm = min(M, 2048)
    nf = D_EXPERT // FFN_TF
    return pl.pallas_call(
        _ffn_kernel,
        grid=(E, M // tm, nf),
        in_specs=[pl.BlockSpec((1, tm, D_MODEL), lambda e, m, f: (e, m, 0)),
                  pl.BlockSpec((1, D_MODEL, FFN_TF), lambda e, m, f: (e, 0, f)),
                  pl.BlockSpec((1, D_MODEL, FFN_TF), lambda e, m, f: (e, 0, f)),
                  pl.BlockSpec((1, FFN_TF, D_MODEL), lambda e, m, f: (e, f, 0))],
        out_specs=pl.BlockSpec((1, tm, D_MODEL), lambda e, m, f: (e, m, 0)),
        out_shape=jax.ShapeDtypeStruct((E, M, D_MODEL), BF16),
        scratch_shapes=[pltpu.VMEM((tm, D_MODEL), F32)],
        compiler_params=_cparams(3),
        name="ffn",
    )(xg, w1, w3, w2)


def _combine_kernel(st_ref, pos_ref, aff_ref, y_ref, h1_ref, g2_ref, fg_ref, o_ref, moe_ref, *, cap):
    b = pl.program_id(0)
    k = pl.program_id(1)
    pos = pos_ref[0]
    aff = aff_ref[0]
    win = lax.broadcasted_iota(I32, (COMBINE_WIN, TM), 0)
    ws, ys, rows = [], [], []
    for e in range(N_EXPERTS):
        row0 = jnp.minimum(_align_down(st_ref[b, e, k], BF16_ROWS), cap - COMBINE_WIN)
        row0 = pl.multiple_of(row0, BF16_ROWS)
        ws.append(jnp.where(pos[e:e + 1, :] == row0 + win, aff[e:e + 1, :], 0.0).astype(BF16))
        ys.append(y_ref[e, pl.ds(row0, COMBINE_WIN), :])
        rows.append(row0)
    moe_ref[...] = _dot(jnp.concatenate(ws, axis=0), jnp.concatenate(ys, axis=0), TN)
    slot = lax.broadcasted_iota(I32, (cap, TM), 0)
    for e in range(N_EXPERTS):
        @pl.when(st_ref[b, e, k + 1] > rows[e] + COMBINE_WIN)
        def _():
            rest = (pos[e:e + 1, :] == slot) & (slot >= rows[e] + COMBINE_WIN)
            moe_ref[...] += _dot(jnp.where(rest, aff[e:e + 1, :], 0.0).astype(BF16), y_ref[e], TN)
    h2 = h1_ref[0] + g2_ref[0] * moe_ref[...]
    o_ref[0] = _rmsnorm(h2, fg_ref[...])


def _combine_call(starts, pos, aff_t, ys, h1, mods3, final_g, cap):
    B, E, T = pos.shape
    assert cap >= COMBINE_WIN and (cap - COMBINE_WIN) % BF16_ROWS == 0
    grid_spec = pltpu.PrefetchScalarGridSpec(
        num_scalar_prefetch=1,
        grid=(B, T // TM),
        in_specs=[pl.BlockSpec((1, E, TM), lambda b, i, st: (b, 0, i)),
                  pl.BlockSpec((1, E, TM), lambda b, i, st: (b, 0, i)),
                  pl.BlockSpec((E, cap, D_MODEL), lambda b, i, st: (0, b, 0)),
                  pl.BlockSpec((1, TM, D_MODEL), lambda b, i, st: (b, i, 0)),
                  pl.BlockSpec((1, 1, D_MODEL), lambda b, i, st: (b, 0, 5)),
                  pl.BlockSpec((1, D_MODEL), lambda b, i, st: (0, 0))],
        out_specs=pl.BlockSpec((1, TM, D_MODEL), lambda b, i, st: (b, i, 0)),
        scratch_shapes=[pltpu.VMEM((TM, D_MODEL), F32)])
    return pl.pallas_call(
        functools.partial(_combine_kernel, cap=cap),
        grid_spec=grid_spec,
        out_shape=jax.ShapeDtypeStruct((B, T, D_MODEL), F32),
        compiler_params=_cparams(2),
        name="combine",
    )(starts, pos, aff_t, ys, h1, mods3, final_g)


def _pad_cols(w, n):
    return jnp.pad(w, ((0, 0), (0, n - w.shape[1])))


def _both_dirs(up):
    z = jnp.zeros_like(up[0])
    return jnp.concatenate([jnp.concatenate([up[0], z], axis=1),
                            jnp.concatenate([z, up[1]], axis=1)], axis=0)


def kernel(x, c, ctx, c_ctx, mod_w, mod_b, norm1_g, w_in, ml_conv_w, ml_conv_b, ml_gate_b, ml_norm_g,
           rw_mu, rw_w0, rw_w_up, rw_a0, rw_a_up, rw_g_up, rw_k_k, rw_k_a, rw_r_k, rw_ln_g, rw_ln_b,
           w_out, norm2_g, router_w, exp_w1, exp_w3, exp_w2, final_g):
    B, T, D = x.shape
    t_ctx = ctx.shape[1]
    assert D == D_MODEL and T % TM == 0 and t_ctx % TM == 0 and TM % (T // GRID_W) == 0
    assert mod_w.shape[0] == 1 and B < 16
    nct, nlt = t_ctx // TM, T // TM
    cap = EC_FACTOR * T // N_EXPERTS
    ml_cols = 4 * ML_W + ML_GATES

    cc = jnp.concatenate([c, c_ctx[None, :], jnp.zeros((16 - B - 1, D), F32)], axis=0)
    mods = _mod_call(cc, mod_w[0], mod_b)
    mods3 = mods.reshape(16, 1, 6 * D)
    g1n = norm1_g.reshape(1, D)

    w_ml = _pad_cols(w_in[0, :, :ml_cols], ML_N).astype(BF16)
    w_rw = _pad_cols(w_in[0, :, ml_cols:], RW_N).astype(BF16)
    qkv, gates, zo = _ml_in_call(x, ctx, mods3, g1n, w_ml, ml_conv_w[0], ml_conv_b,
                                 _pad_cols(ml_gate_b, LANES))
    h_f, h_b = _mlstm_call(qkv, gates, B, nct, nlt)

    gup = jnp.pad(rw_g_up[0], ((0, 256 - RW_LORA_G), (0, 0)))
    scan_in, gb = _rw_in_call(
        x, ctx, mods3, g1n, w_rw,
        _pad_cols(rw_mu[0], RW_N), rw_w0[0].reshape(1, 2 * RW_W), _both_dirs(rw_w_up[0]),
        rw_a0[0].reshape(1, 2 * RW_W), _both_dirs(rw_a_up[0]), gup,
        rw_k_k, rw_k_a, rw_r_k[0].reshape(1, RW_W))
    y_f, y_b = _rw_scan_call(scan_in, B, t_ctx, T)

    h1, hn, aff_t = _mix_out_call(
        x, h_f, h_b, zo, y_f, y_b, gb, mods3, ml_norm_g, rw_ln_g, rw_ln_b, norm2_g,
        w_out[0].astype(BF16), _pad_cols(router_w[0], LANES))

    pos, starts = _route_call(aff_t, cap)
    starts = starts[:, :, :nlt + 1]
    xg = _gather_call(starts, pos, hn, cap)
    ys = _ffn_call(xg, exp_w1[0], exp_w3[0], exp_w2[0])
    return _combine_call(starts, pos, aff_t, ys, h1, mods3, final_g.reshape(1, D), cap)
```

```python
import functools
import math

import jax
import jax.numpy as jnp
from jax import lax
from jax.experimental import pallas as pl
from jax.experimental.pallas import tpu as pltpu

F32 = jnp.float32
BF16 = jnp.bfloat16
I32 = jnp.int32

D_MODEL = 1024
GRID_W = 64
EPS = 1e-6
ML_W = 512
ML_HEADS = 4
ML_DH = 128
ML_GATES = 16
ML_N = 4 * ML_W + 128
RW_W = 512
RW_HEADS = 8
RW_DH = 64
RW_LORA = 64
RW_LORA_G = 160
RW_LN_EPS = 64e-5
RW_COLS = 3 * RW_W + 2 * RW_LORA + 2 * RW_LORA + RW_LORA_G
RW_N = 2048
RW_PAIRS = RW_HEADS // 2
N_EXPERTS = 16
EC_FACTOR = 2
D_EXPERT = 2816

LANES = 128
SUBLANES = 8
HALO = SUBLANES
TM = 256
RW_CHUNK = 64
RW_PAIRS_PER_STEP = 4
RW_BATCH_PER_STEP = 4
RW_INV_BASE = 4
FFN_TF = 256
BF16_ROWS = 16
GATHER_WIN = 64
COMBINE_WIN = 64
VMEM_LIMIT = 56 * 1024 * 1024

NN = (((1,), (0,)), ((), ()))
NT = (((1,), (1,)), ((), ()))
TN = (((0,), (0,)), ((), ()))


def _cparams(n_axes):
    return pltpu.CompilerParams(
        dimension_semantics=("arbitrary",) * n_axes, vmem_limit_bytes=VMEM_LIMIT)


def _dot(a, b, dn=NN):
    return lax.dot_general(a, b, dn, preferred_element_type=F32)


def _split2(a):
    hi = a.astype(BF16)
    lo = (a - hi.astype(F32)).astype(BF16)
    return hi, lo


def _split3(a):
    hi = a.astype(BF16)
    r1 = a - hi.astype(F32)
    mid = r1.astype(BF16)
    lo = (r1 - mid.astype(F32)).astype(BF16)
    return hi, mid, lo


def _dot1(a, b, dn=NN):
    return _dot(a.astype(BF16), b.astype(BF16), dn)


def _dot3(a, b, dn=NN):
    ah, al = _split2(a)
    bh, bl = _split2(b)
    return _dot(ah, bh, dn) + (_dot(ah, bl, dn) + _dot(al, bh, dn))


def _dot_sel(sel_bf16, b, dn=NN):
    bh, bm, bl = _split3(b)
    return _dot(sel_bf16, bh, dn) + (_dot(sel_bf16, bm, dn) + _dot(sel_bf16, bl, dn))


def _dot_rsel(a, sel_bf16, dn=NN):
    ah, al = _split2(a)
    return _dot(ah, sel_bf16, dn) + _dot(al, sel_bf16, dn)


def _silu(x):
    return x * jax.nn.sigmoid(x)


def _log_sigmoid(x):
    return jnp.minimum(x, 0.0) - jnp.log1p(jnp.exp(-jnp.abs(x)))


def _rmsnorm(x, g):
    return x * lax.rsqrt(jnp.mean(x * x, axis=-1, keepdims=True) + EPS) * g


def _tri(n, lower, strict, dtype):
    r = lax.broadcasted_iota(I32, (n, n), 0)
    c = lax.broadcasted_iota(I32, (n, n), 1)
    if lower:
        m = (c < r) if strict else (c <= r)
    else:
        m = (c > r) if strict else (c >= r)
    return m if dtype is None else m.astype(dtype)


def _cumsum_rows(x, forward):
    n = x.shape[0]
    row = lax.broadcasted_iota(I32, x.shape, 0)
    s = 1
    while s < n:
        if forward:
            x = x + jnp.where(row >= s, pltpu.roll(x, s, 0), 0.0)
        else:
            x = x + jnp.where(row < n - s, pltpu.roll(x, n - s, 0), 0.0)
        s *= 2
    return x


def _mod_kernel(c_ref, w_ref, b_ref, o_ref):
    o_ref[...] = _dot3(_silu(c_ref[...]), w_ref[...]) + b_ref[...]


def _mod_call(cc, mod_w, mod_b):
    n = mod_w.shape[1]
    tn = 1024
    return pl.pallas_call(
        _mod_kernel,
        grid=(n // tn,),
        in_specs=[pl.BlockSpec((16, D_MODEL), lambda j: (0, 0)),
                  pl.BlockSpec((D_MODEL, tn), lambda j: (0, j)),
                  pl.BlockSpec((1, tn), lambda j: (0, j))],
        out_specs=pl.BlockSpec((16, tn), lambda j: (0, j)),
        out_shape=jax.ShapeDtypeStruct((16, n), F32),
        compiler_params=_cparams(1),
        name="mod",
    )(cc, mod_w, mod_b)


def _project_tile(xin, prev_rows, next_rows, sh, sc, g, w_ref):
    xe = jnp.concatenate([prev_rows, xin, next_rows], axis=0)
    xn = _rmsnorm(xe, g) * (1.0 + sc) + sh
    return _dot(xn.astype(BF16), w_ref[...])


def _tile_and_neighbours(u, seg_first, seg_last):
    row = lax.broadcasted_iota(I32, (TM, u.shape[1]), 0)
    z = u[HALO:HALO + TM]
    zm1 = jnp.where((row == 0) & seg_first, 0.0, u[HALO - 1:HALO - 1 + TM])
    zp1 = jnp.where((row == TM - 1) & seg_last, 0.0, u[HALO + 1:HALO + 1 + TM])
    return z, zm1, zp1


def _ml_in_kernel(ctx_ref, x_ref, xp_ref, xn_ref, sh_ref, sc_ref, g_ref, w_ref, cw_ref, cb_ref, gb_ref,
                  qkv_ref, gate_ref, zo_ref, *, nct, ntot):
    i = pl.program_id(1)
    xin = jnp.where(i < nct, ctx_ref[0], x_ref[0])
    u = _project_tile(xin, xp_ref[0], xn_ref[0], sh_ref[0], sc_ref[0], g_ref[...], w_ref)
    z = u[HALO:HALO + TM]
    zqk, zm1, zp1 = _tile_and_neighbours(
        u[:, 0:2 * ML_W], (i == 0) | (i == nct), (i == nct - 1) | (i == ntot - 1))
    cw = cw_ref[...]
    qk = _silu(zm1 * cw[0:1] + zqk * cw[1:2] + zp1 * cw[2:3] + cb_ref[...])
    qkv_ref[0, :, 0:ML_W] = qk[:, 0:ML_W].astype(BF16)
    qkv_ref[0, :, ML_W:2 * ML_W] = (qk[:, ML_W:2 * ML_W] * (ML_DH ** -0.5)).astype(BF16)
    qkv_ref[0, :, 2 * ML_W:3 * ML_W] = z[:, 2 * ML_W:3 * ML_W].astype(BF16)
    zo_ref[0] = z[:, 3 * ML_W:4 * ML_W]
    g = z[:, 4 * ML_W:4 * ML_W + LANES] + gb_ref[...]
    lane = lax.broadcasted_iota(I32, (TM, LANES), 1)
    is_f = ((lane >= 4) & (lane < 8)) | ((lane >= 12) & (lane < 16))
    gate_ref[0] = jnp.where(is_f, _log_sigmoid(g), g)


def _ml_in_call(x, ctx, mods3, norm_g, w, conv_w, conv_b, gate_b):
    B, T, _ = x.shape
    assert ctx.shape[1] == TM
    nct, nlt = 1, T // TM
    ntot = nct + nlt
    hb = TM // HALO
    lat = lambda i: jnp.maximum(i - nct, 0)
    mrow = lambda b, i: jnp.where(i < nct, B, b)
    full = lambda shape: pl.BlockSpec(shape, lambda b, i: (0,) * len(shape))
    return pl.pallas_call(
        functools.partial(_ml_in_kernel, nct=nct, ntot=ntot),
        grid=(B, ntot),
        in_specs=[
            pl.BlockSpec((1, TM, D_MODEL), lambda b, i: (b, 0, 0)),
            pl.BlockSpec((1, TM, D_MODEL), lambda b, i: (b, lat(i), 0)),
            pl.BlockSpec((1, HALO, D_MODEL), lambda b, i: (b, jnp.maximum(lat(i) * hb - 1, 0), 0)),
            pl.BlockSpec((1, HALO, D_MODEL),
                         lambda b, i: (b, jnp.minimum((lat(i) + 1) * hb, nlt * hb - 1), 0)),
            pl.BlockSpec((1, 1, D_MODEL), lambda b, i: (mrow(b, i), 0, 0)),
            pl.BlockSpec((1, 1, D_MODEL), lambda b, i: (mrow(b, i), 0, 1)),
            full((1, D_MODEL)), full((D_MODEL, ML_N)),
            full((3, 2 * ML_W)), full((1, 2 * ML_W)), full((1, LANES)),
        ],
        out_specs=[
            pl.BlockSpec((1, TM, 3 * ML_W), lambda b, i: (b, i, 0)),
            pl.BlockSpec((1, TM, LANES), lambda b, i: (b, i, 0)),
            pl.BlockSpec((1, TM, ML_W), lambda b, i: (b, lat(i), 0)),
        ],
        out_shape=[jax.ShapeDtypeStruct((B, ntot * TM, 3 * ML_W), BF16),
                   jax.ShapeDtypeStruct((B, ntot * TM, LANES), F32),
                   jax.ShapeDtypeStruct((B, nlt * TM, ML_W), F32)],
        compiler_params=_cparams(2),
        name="ml_in",
    )(ctx, x, x, x, mods3, mods3, norm_g, w, conv_w, conv_b, gate_b)


def _seq_chunk(d, j, nct, ntot):
    bwd = jnp.where(j < nct, nct - 1 - j, ntot - 1 - (j - nct))
    return jnp.where(d == 0, j, bwd)


def _mlstm_kernel(qf_ref, gf_ref, qb_ref, gb_ref, of_ref, ob_ref, c_ref, n_ref, m_ref):
    j = pl.program_id(1)
    L = TM

    @pl.when(j == 0)
    def _():
        c_ref[...] = jnp.zeros_like(c_ref)
        n_ref[...] = jnp.zeros_like(n_ref)
        m_ref[...] = jnp.zeros_like(m_ref)

    dirs = ((qf_ref, gf_ref, of_ref, True, 0), (qb_ref, gb_ref, ob_ref, False, 2 * ML_HEADS))
    keep_d = [_tri(L, lower, False, None) for _, _, _, lower, _ in dirs]
    q_all = [g_ref[0] for _, g_ref, _, _, _ in dirs]
    b_all = [_dot_sel(keep_d[di].astype(BF16), q_all[di]) for di in range(2)]
    q_t = [q_all[di].T for di in range(2)]
    b_t = [b_all[di].T for di in range(2)]

    chains = [(di, h) for di in range(2) for h in range(ML_HEADS)]
    cs = range(len(chains))
    sl = [slice(h * ML_DH, (h + 1) * ML_DH) for h in range(ML_HEADS)]
    qkv = [dirs[di][0] for di, _ in chains]
    q = [qkv[c][0, :, sl[h]] for c, (di, h) in enumerate(chains)]
    kb = [qkv[c][0, :, ML_W + h * ML_DH:ML_W + (h + 1) * ML_DH] for c, (di, h) in enumerate(chains)]
    vb = [qkv[c][0, :, 2 * ML_W + h * ML_DH:2 * ML_W + (h + 1) * ML_DH] for c, (di, h) in enumerate(chains)]
    v = [vb[c].astype(F32) for c in cs]
    icol = [dirs[di][4] + h for di, h in chains]
    fcol = [icol[c] + ML_HEADS for c in cs]
    b_col = [b_all[di][:, fcol[c]:fcol[c] + 1] for c, (di, h) in enumerate(chains)]
    li_row = [q_t[di][icol[c]:icol[c] + 1, :] for c, (di, h) in enumerate(chains)]
    b_row = [b_t[di][fcol[c]:fcol[c] + 1, :] for c, (di, h) in enumerate(chains)]
    keep = [keep_d[di] for di, _ in chains]
    m_prev = [m_ref[di, h, 0:1, 0:1] for di, h in chains]
    C = [c_ref[di, h] for di, h in chains]
    n_row = [n_ref[di, h, 0:1, :] for di, h in chains]

    col0 = lax.broadcasted_iota(I32, (L, ML_DH), 1) == 0
    row0 = lax.broadcasted_iota(I32, (ML_DH, ML_DH), 0) == 0
    v_ext = [jnp.concatenate([vb[c], jnp.where(col0, 1.0, 0.0).astype(BF16)], axis=1) for c in cs]
    c_ext = [jnp.concatenate([C[c], jnp.where(row0, n_row[c], 0.0)], axis=0).astype(BF16) for c in cs]
    qk_s = [_dot(q[c], kb[c], NT) for c in cs]
    q_cn = [_dot(q[c], c_ext[c], NT) for c in cs]
    rel = [jnp.where(keep[c], li_row[c] - b_row[c], -jnp.inf) for c in cs]
    inter = [b_col[c] + m_prev[c] for c in cs]
    m_t = [jnp.maximum(inter[c], b_col[c] + jnp.max(rel[c], axis=-1, keepdims=True)) for c in cs]
    wts = [jnp.exp(rel[c] + (b_col[c] - m_t[c])) * qk_s[c] for c in cs]
    w_inter = [jnp.exp(inter[c] - m_t[c]) for c in cs]
    nd = [_dot(wts[c].astype(BF16), v_ext[c]) + w_inter[c] * q_cn[c] for c in cs]
    for c, (di, h) in enumerate(chains):
        den = nd[c][:, ML_DH:ML_DH + 1]
        dirs[di][2][0, :, sl[h]] = nd[c][:, 0:ML_DH] / jnp.maximum(jnp.abs(den), jnp.exp(-m_t[c]))

    b_end = [b_col[c][L - 1:L, :] if dirs[di][3] else b_col[c][0:1, :] for c, (di, h) in enumerate(chains)]
    dec = [b_end[c] - b_row[c] + li_row[c] for c in cs]
    m_new = [jnp.maximum(b_end[c] + m_prev[c], jnp.max(dec[c], axis=-1, keepdims=True)) for c in cs]
    wk = [jnp.exp(dec[c] - m_new[c]) for c in cs]
    s_old = [jnp.exp(b_end[c] + m_prev[c] - m_new[c]) for c in cs]
    c_upd = [_dot((v[c].T * wk[c]).astype(BF16), kb[c]) for c in cs]
    n_upd = [_dot(jnp.broadcast_to(wk[c], (SUBLANES, L)).astype(BF16), kb[c]) for c in cs]
    for c, (di, h) in enumerate(chains):
        c_ref[di, h] = s_old[c] * C[c] + c_upd[c]
        n_ref[di, h] = s_old[c] * n_ref[di, h] + n_upd[c]
        m_ref[di, h] = jnp.broadcast_to(m_new[c], (SUBLANES, LANES))


def _mlstm_call(qkv, gates, B, nct, nlt):
    ntot = nct + nlt
    mirror = lambda j: _seq_chunk(1, j, nct, ntot)
    out_shape = jax.ShapeDtypeStruct((B, nlt * TM, ML_W), F32)
    return pl.pallas_call(
        _mlstm_kernel,
        grid=(B, ntot),
        in_specs=[
            pl.BlockSpec((1, TM, 3 * ML_W), lambda b, j: (b, j, 0)),
            pl.BlockSpec((1, TM, LANES), lambda b, j: (b, j, 0)),
            pl.BlockSpec((1, TM, 3 * ML_W), lambda b, j: (b, mirror(j), 0)),
            pl.BlockSpec((1, TM, LANES), lambda b, j: (b, mirror(j), 0)),
        ],
        out_specs=[
            pl.BlockSpec((1, TM, ML_W), lambda b, j: (b, jnp.maximum(j - nct, 0), 0)),
            pl.BlockSpec((1, TM, ML_W), lambda b, j: (b, jnp.where(j < nct, nlt - 1, mirror(j) - nct), 0)),
        ],
        out_shape=[out_shape, out_shape],
        scratch_shapes=[pltpu.VMEM((2, ML_HEADS, ML_DH, ML_DH), F32),
                        pltpu.VMEM((2, ML_HEADS, SUBLANES, ML_DH), F32),
                        pltpu.VMEM((2, ML_HEADS, SUBLANES, LANES), F32)],
        compiler_params=_cparams(2),
        name="mlstm",
    )(qkv, gates, qkv, gates)


def _rw_in_kernel(ctx_ref, x_ref, xp_ref, xn_ref, sh_ref, sc_ref, g_ref, w_ref,
                  mu_ref, w0_ref, wup_ref, a0_ref, aup_ref, gup_ref, kk_ref, ka_ref, rk_ref,
                  scan_ref, gb_ref, *, nct, ntot, rows):
    i = pl.program_id(1)
    jt = jnp.maximum(i - nct, 0)
    cpt = TM // rows
    tpb = x_ref.shape[2] // cpt
    c0 = (jt % tpb) * cpt if tpb > 1 else 0
    xcm = jnp.concatenate([x_ref[0, :, c0 + c, :] for c in range(cpt)], axis=0)
    xin = jnp.where(i < nct, ctx_ref[0], xcm)
    prev1 = xp_ref[0, SUBLANES - 1:SUBLANES, jnp.maximum(jt * cpt - 1, 0) % SUBLANES, :]
    next1 = xn_ref[0, 0:1, jnp.minimum((jt + 1) * cpt, GRID_W - 1) % SUBLANES, :]
    u = _project_tile(xin, jnp.broadcast_to(prev1, (HALO, D_MODEL)),
                      jnp.broadcast_to(next1, (HALO, D_MODEL)),
                      sh_ref[0], sc_ref[0], g_ref[...], w_ref)
    z, zm1, zp1 = _tile_and_neighbours(
        u, (i == 0) | (i == nct), (i == nct - 1) | (i == ntot - 1))
    mu = mu_ref[...]
    xs = z * (1.0 - mu[0:1] - mu[1:2]) + mu[0:1] * zm1 + mu[1:2] * zp1

    r = xs[:, 0:RW_W]
    k = xs[:, RW_W:2 * RW_W]
    v = xs[:, 2 * RW_W:3 * RW_W]
    wd = xs[:, 3 * RW_W:3 * RW_W + 2 * RW_LORA]
    ad = xs[:, 3 * RW_W + 2 * RW_LORA:3 * RW_W + 4 * RW_LORA]
    gd = xs[:, 3 * RW_W + 4 * RW_LORA:RW_N]

    w_pre = w0_ref[...] + _dot1(jnp.tanh(wd), wup_ref[...])
    lw = jax.nn.sigmoid(w_pre) * (-math.exp(-0.5))
    a = jax.nn.sigmoid(a0_ref[...] + _dot1(ad, aup_ref[...]))
    g = _dot1(jax.nn.sigmoid(gd), gup_ref[...])

    hr = lax.broadcasted_iota(I32, (RW_W, RW_W), 0) // RW_DH
    hc = lax.broadcasted_iota(I32, (RW_W, RW_W), 1) // RW_DH
    head_ones = (hr == hc).astype(BF16)
    kk = k * kk_ref[...]
    ss = _dot_rsel(kk * kk, head_ones)
    kk = kk * jnp.minimum(lax.rsqrt(ss), 1e12)
    ka = ka_ref[...]
    k_f = k * (1.0 + (a[:, 0:RW_W] - 1.0) * ka)
    k_b = k * (1.0 + (a[:, RW_W:2 * RW_W] - 1.0) * ka)
    rk = rk_ref[...]
    bonus = _dot_rsel(r * (k_f + k_b) * rk, head_ones) * v

    cols = (r, v, kk, lw[:, 0:RW_W], k_f, kk * a[:, 0:RW_W],
            lw[:, RW_W:2 * RW_W], k_b, kk * a[:, RW_W:2 * RW_W])
    for p in range(RW_PAIRS):
        for qi, arr in enumerate(cols):
            scan_ref[0, p, :, qi * LANES:(qi + 1) * LANES] = arr[:, p * LANES:(p + 1) * LANES]
    gb_ref[0, :, 0:RW_W] = g
    gb_ref[0, :, RW_W:2 * RW_W] = bonus


def _rw_in_call(x, ctx, mods3, norm_g, w, mu, w0, wup, a0, aup, gup, k_k, k_a, r_k):
    B, T, _ = x.shape
    assert ctx.shape[1] == TM
    nct, nlt = 1, T // TM
    ntot = nct + nlt
    rows = T // GRID_W
    cpt = TM // rows
    cb = max(SUBLANES, cpt)
    tpb = cb // cpt
    assert rows % SUBLANES == 0 and GRID_W % cb == 0 and cb % cpt == 0
    lat = lambda c: jnp.maximum(c - nct, 0)
    mrow = lambda b, c: jnp.where(c < nct, B, b)
    full = lambda shape: pl.BlockSpec(shape, lambda b, c: (0,) * len(shape))
    x4 = x.reshape(B, rows, GRID_W, D_MODEL)
    return pl.pallas_call(
        functools.partial(_rw_in_kernel, nct=nct, ntot=ntot, rows=rows),
        grid=(B, ntot),
        in_specs=[
            pl.BlockSpec((1, TM, D_MODEL), lambda b, c: (b, 0, 0)),
            pl.BlockSpec((1, rows, cb, D_MODEL), lambda b, c: (b, 0, lat(c) // tpb, 0)),
            pl.BlockSpec((1, SUBLANES, SUBLANES, D_MODEL),
                         lambda b, c: (b, rows // SUBLANES - 1,
                                       jnp.maximum(lat(c) * cpt - 1, 0) // SUBLANES, 0)),
            pl.BlockSpec((1, SUBLANES, SUBLANES, D_MODEL),
                         lambda b, c: (b, 0, jnp.minimum((lat(c) + 1) * cpt, GRID_W - 1) // SUBLANES, 0)),
            pl.BlockSpec((1, 1, D_MODEL), lambda b, c: (mrow(b, c), 0, 0)),
            pl.BlockSpec((1, 1, D_MODEL), lambda b, c: (mrow(b, c), 0, 1)),
            full((1, D_MODEL)), full((D_MODEL, RW_N)),
            full((2, RW_N)), full((1, 2 * RW_W)), full((2 * RW_LORA, 2 * RW_W)),
            full((1, 2 * RW_W)), full((2 * RW_LORA, 2 * RW_W)), full((256, RW_W)),
            full((1, RW_W)), full((1, RW_W)), full((1, RW_W)),
        ],
        out_specs=[
            pl.BlockSpec((1, RW_PAIRS, TM, 9 * LANES), lambda b, c: (b, 0, c, 0)),
            pl.BlockSpec((1, TM, 2 * RW_W), lambda b, c: (b, jnp.maximum(c - nct, 0), 0)),
        ],
        out_shape=[jax.ShapeDtypeStruct((B, RW_PAIRS, ntot * TM, 9 * LANES), F32),
                   jax.ShapeDtypeStruct((B, nlt * TM, 2 * RW_W), F32)],
        compiler_params=_cparams(2),
        name="rw_in",
    )(ctx, x4, x4, x4, mods3, mods3, norm_g, w, mu, w0, wup, a0, aup, gup, k_k, k_a, r_k)


def _rw_chunks(blks, states, lowers):
    n = len(blks)
    L = blks[0].shape[0]
    assert L == RW_DH
    idx = range(n)
    col = lambda i, c: blks[i][:, c * LANES:(c + 1) * LANES]
    r, v, kk, lw, kd, bh = ([col(i, c) for i in idx] for c in range(6))
    logp = [_cumsum_rows(lw[i], lowers[i]) for i in idx]
    logp_end = [logp[i][L - 1:L, :] if lowers[i] else logp[i][0:1, :] for i in idx]
    p_inv = [jnp.exp(-logp[i]) for i in idx]
    a_t = [-kk[i] * jnp.exp(logp[i] - lw[i]) for i in idx]
    r_t = [r[i] * jnp.exp(logp[i]) for i in idx]

    row = lax.broadcasted_iota(I32, (L, LANES), 0)
    lane = lax.broadcasted_iota(I32, (L, LANES), 1)
    head0 = lane < RW_DH
    src = jnp.where(head0, lane, lane - RW_DH)

    def split(y_pair):
        return jnp.concatenate([jnp.where(head0, y_pair, 0.0), jnp.where(head0, 0.0, y_pair)],
                               axis=0).astype(BF16)

    def mm(x_cat, y_pair):
        return _dot(x_cat.astype(BF16), split(y_pair))

    strict = {lo: (src < row) if lo else (src > row) for lo in set(lowers)}
    incl_c = {lo: (src <= row) if lo else (src >= row) for lo in set(lowers)}
    eye = jnp.where(src == row, 1.0, 0.0)

    st = [states[i].astype(BF16) for i in idx]
    sv = [split(v[i]) for i in idx]
    bk = [jnp.concatenate([split(bh[i] * p_inv[i]), split(kd[i] * p_inv[i])], axis=0) for i in idx]
    g_ar = [_dot(jnp.concatenate([a_t[i], r_t[i]], axis=0).astype(BF16), bk[i], NT) for i in idx]
    g_a = [g_ar[i][0:L] for i in idx]
    g_r = [g_ar[i][L:2 * L] for i in idx]
    n_mat = [jnp.where(strict[lowers[i]], g_a[i][:, 0:LANES], 0.0) for i in idx]
    a_ak = [jnp.where(strict[lowers[i]], g_a[i][:, LANES:2 * LANES], 0.0) for i in idx]
    a_rb = [jnp.where(incl_c[lowers[i]], g_r[i][:, 0:LANES], 0.0) for i in idx]
    a_rk = [jnp.where(incl_c[lowers[i]], g_r[i][:, LANES:2 * LANES], 0.0) for i in idx]
    w_rhs = [_dot(jnp.concatenate([a_t[i], a_ak[i]], axis=1).astype(BF16),
                  jnp.concatenate([st[i], sv[i]], axis=0)) for i in idx]
    base = RW_INV_BASE
    n_d = [jnp.where(row // base == src // base, n_mat[i], 0.0) for i in idx]
    inv = [eye + n_d[i] for i in idx]
    n_d2 = [mm(n_d[i], n_d[i]) for i in idx]
    inv = [inv[i] + mm(inv[i], n_d2[i]) for i in idx]
    blk = base
    while blk < L:
        link = (row // (2 * blk) == src // (2 * blk)) & (row // blk != src // blk)
        t_m = [mm(inv[i], jnp.where(link, n_mat[i], 0.0)) for i in idx]
        inv = [inv[i] + mm(t_m[i], inv[i]) for i in idx]
        blk *= 2
    u = [mm(inv[i], w_rhs[i]) for i in idx]
    uv = [jnp.concatenate([u[i], v[i]], axis=0) for i in idx]
    y = [_dot(jnp.concatenate([r_t[i], a_rb[i], a_rk[i]], axis=1).astype(BF16),
              jnp.concatenate([st[i], split(u[i]), sv[i]], axis=0)) for i in idx]

    rr = lax.broadcasted_iota(I32, (LANES, LANES), 0) // RW_DH
    cc = lax.broadcasted_iota(I32, (LANES, LANES), 1) // RW_DH
    s_new = []
    for i in idx:
        to_end = jnp.exp(logp_end[i] - logp[i])
        bk_end = jnp.concatenate([bh[i] * to_end, kd[i] * to_end], axis=0)
        decay_rows = jnp.broadcast_to(jnp.exp(logp_end[i]), (LANES, LANES)).T
        s_new.append(jnp.where(rr == cc, states[i] * decay_rows + _dot1(bk_end, uv[i], TN), 0.0))
    return y, s_new


def _rw_scan_kernel(f_ref, b_ref, yf_ref, yb_ref, s_ref):
    j = pl.program_id(2)

    @pl.when(j == 0)
    def _():
        s_ref[...] = jnp.zeros_like(s_ref)

    probs = [(bi, p) for bi in range(f_ref.shape[0]) for p in range(f_ref.shape[1])]
    blks, states, lowers = [], [], []
    for bi, p in probs:
        fb = f_ref[bi, p]
        bb = b_ref[bi, p]
        blks += [fb[:, 0:6 * LANES],
                 jnp.concatenate([bb[:, 0:3 * LANES], bb[:, 6 * LANES:9 * LANES]], axis=1)]
        states += [s_ref[bi, p, 0], s_ref[bi, p, 1]]
        lowers += [True, False]
    y, s_new = _rw_chunks(blks, states, lowers)
    for q, (bi, p) in enumerate(probs):
        yf_ref[bi, p] = y[2 * q]
        yb_ref[bi, p] = y[2 * q + 1]
        s_ref[bi, p, 0] = s_new[2 * q]
        s_ref[bi, p, 1] = s_new[2 * q + 1]


def _rw_scan_call(scan_in, B, t_ctx, t_lat):
    L = RW_CHUNK
    nct, nlt = t_ctx // L, t_lat // L
    ntot = nct + nlt
    fchunk = lambda j: j
    bchunk = lambda j: _seq_chunk(1, j, nct, ntot)
    pps = RW_PAIRS_PER_STEP
    bps = RW_BATCH_PER_STEP if B % RW_BATCH_PER_STEP == 0 else 1
    in_blk = (bps, pps, L, 9 * LANES)
    out_blk = (bps, pps, L, LANES)
    out_shape = jax.ShapeDtypeStruct((B, RW_PAIRS, t_lat, LANES), F32)
    return pl.pallas_call(
        _rw_scan_kernel,
        grid=(B // bps, RW_PAIRS // pps, ntot),
        in_specs=[pl.BlockSpec(in_blk, lambda b, p, j: (b, p, fchunk(j), 0)),
                  pl.BlockSpec(in_blk, lambda b, p, j: (b, p, bchunk(j), 0))],
        out_specs=[
            pl.BlockSpec(out_blk, lambda b, p, j: (b, p, jnp.maximum(fchunk(j) - nct, 0), 0)),
            pl.BlockSpec(out_blk,
                         lambda b, p, j: (b, p, jnp.where(j < nct, nlt - 1, bchunk(j) - nct), 0)),
        ],
        out_shape=[out_shape, out_shape],
        scratch_shapes=[pltpu.VMEM((bps, pps, 2, LANES, LANES), F32)],
        compiler_params=_cparams(3),
        name="rw_scan",
    )(scan_in, scan_in)


def _mix_out_kernel(x_ref, hf_ref, hb_ref, zo_ref, yf_ref, yb_ref, gb_ref, g1_ref, sh2_ref, sc2_ref,
                    mlg_ref, lng_ref, lnb_ref, n2g_ref, wo_ref, rw_ref,
                    h1_ref, hn_ref, aff_ref):
    hm = hf_ref[0] + hb_ref[0]
    parts = []
    for h in range(ML_HEADS):
        hh = hm[:, h * ML_DH:(h + 1) * ML_DH]
        parts.append(hh * lax.rsqrt(jnp.mean(hh * hh, axis=-1, keepdims=True) + EPS))
    ml = jnp.concatenate(parts, axis=1) * mlg_ref[...] * jax.nn.sigmoid(zo_ref[0])

    nr = yf_ref.shape[3]
    to_raster = lambda a: jnp.swapaxes(a, 0, 1).reshape(nr * GRID_W, a.shape[-1])
    y = jnp.concatenate([to_raster(yf_ref[0, p] + yb_ref[0, p]) for p in range(RW_PAIRS)], axis=1)
    gbr = to_raster(gb_ref[0])
    hr = lax.broadcasted_iota(I32, (RW_W, RW_W), 0) // RW_DH
    hc = lax.broadcasted_iota(I32, (RW_W, RW_W), 1) // RW_DH
    head_ones = (hr == hc).astype(BF16)
    mean = _dot_rsel(y, head_ones) * (1.0 / RW_DH)
    dy = y - mean
    var = _dot_rsel(dy * dy, head_ones) * (1.0 / RW_DH)
    rw = dy * lax.rsqrt(var + RW_LN_EPS) * lng_ref[...] + lnb_ref[...]
    rw = (rw + gbr[:, RW_W:2 * RW_W]) * gbr[:, 0:RW_W]

    mix = _dot(jnp.concatenate([ml, rw], axis=1).astype(BF16), wo_ref[...])
    h1 = x_ref[0] + g1_ref[0] * mix
    h1_ref[0] = h1
    hn = _rmsnorm(h1, n2g_ref[...]) * (1.0 + sc2_ref[0]) + sh2_ref[0]
    hn_ref[0] = hn.astype(BF16)
    logits = _dot3(hn, rw_ref[...])
    lane = lax.broadcasted_iota(I32, logits.shape, 1)
    logits = jnp.where(lane < N_EXPERTS, logits, -jnp.inf)
    e = jnp.exp(logits - jnp.max(logits, axis=-1, keepdims=True))
    aff = e / jnp.sum(e, axis=-1, keepdims=True)
    aff_ref[0] = aff.T[0:N_EXPERTS, :]


def _mix_out_call(x, h_f, h_b, zo, y_f, y_b, gb, mods3, ml_norm_g, ln_g, ln_b, norm2_g,
                  wo, router_pad):
    B, T, _ = x.shape
    rows = T // GRID_W
    rb = SUBLANES
    tmo = rb * GRID_W
    assert rows % rb == 0
    yv = lambda y: y.reshape(B, RW_PAIRS, GRID_W, rows, LANES)
    gbv = gb.reshape(B, GRID_W, rows, 2 * RW_W)
    row1 = lambda shape: pl.BlockSpec(shape, lambda b, i: (0,) * len(shape))
    mod = lambda k: pl.BlockSpec((1, 1, D_MODEL), lambda b, i: (b, 0, k))
    return pl.pallas_call(
        _mix_out_kernel,
        grid=(B, T // tmo),
        in_specs=[
            pl.BlockSpec((1, tmo, D_MODEL), lambda b, i: (b, i, 0)),
            pl.BlockSpec((1, tmo, ML_W), lambda b, i: (b, i, 0)),
            pl.BlockSpec((1, tmo, ML_W), lambda b, i: (b, i, 0)),
            pl.BlockSpec((1, tmo, ML_W), lambda b, i: (b, i, 0)),
            pl.BlockSpec((1, RW_PAIRS, GRID_W, rb, LANES), lambda b, i: (b, 0, 0, i, 0)),
            pl.BlockSpec((1, RW_PAIRS, GRID_W, rb, LANES), lambda b, i: (b, 0, 0, i, 0)),
            pl.BlockSpec((1, GRID_W, rb, 2 * RW_W), lambda b, i: (b, 0, i, 0)),
            mod(2), mod(3), mod(4),
            row1((1, ML_W)), row1((1, RW_W)), row1((1, RW_W)), row1((1, D_MODEL)),
            row1((D_MODEL, D_MODEL)), row1((D_MODEL, LANES)),
        ],
        out_specs=[
            pl.BlockSpec((1, tmo, D_MODEL), lambda b, i: (b, i, 0)),
            pl.BlockSpec((1, tmo, D_MODEL), lambda b, i: (b, i, 0)),
            pl.BlockSpec((1, N_EXPERTS, tmo), lambda b, i: (b, 0, i)),
        ],
        out_shape=[jax.ShapeDtypeStruct((B, T, D_MODEL), F32),
                   jax.ShapeDtypeStruct((B, T, D_MODEL), BF16),
                   jax.ShapeDtypeStruct((B, N_EXPERTS, T), F32)],
        compiler_params=_cparams(2),
        name="mix_out",
    )(x, h_f, h_b, zo, yv(y_f), yv(y_b), gbv, mods3, mods3, mods3,
      ml_norm_g, ln_g, ln_b, norm2_g, wo, router_pad)


def _route_kernel(aff_ref, pos_ref, st_ref, *, cap):
    a = aff_ref[0]
    T = a.shape[1]
    as_f32 = lambda bits: lax.bitcast_convert_type(bits, F32)

    def body(i, thr):
        cand = thr | jnp.left_shift(jnp.int32(1), 30 - i)
        cnt = jnp.sum(jnp.where(a >= as_f32(cand), 1.0, 0.0), axis=1, keepdims=True)
        return jnp.where(cnt >= cap, cand, thr)

    thr = lax.fori_loop(0, 31, body, jnp.zeros((N_EXPERTS, 1), I32))
    gt = a >= as_f32(thr + 1)
    eq = (a >= as_f32(thr)) & jnp.logical_not(gt)
    need = cap - jnp.sum(jnp.where(gt, 1.0, 0.0), axis=1, keepdims=True)
    tri = _tri(TM, False, False, BF16)

    def prefix_excl(mask):
        outs, carries = [], []
        carry = jnp.zeros((N_EXPERTS, 1), F32)
        for blk in range(T // TM):
            seg = jnp.where(mask[:, blk * TM:(blk + 1) * TM], 1.0, 0.0)
            inc = _dot(seg.astype(BF16), tri)
            outs.append(inc - seg + carry)
            carries.append(carry)
            carry = carry + jnp.sum(seg, axis=1, keepdims=True)
        return jnp.concatenate(outs, axis=1), carries + [carry]

    chosen = gt | (eq & (prefix_excl(eq)[0] < need))
    slot, block_starts = prefix_excl(chosen)
    pos_ref[0] = jnp.where(chosen, slot, -1.0).astype(I32)
    lane = lax.broadcasted_iota(I32, (N_EXPERTS, LANES), 1)
    st = jnp.zeros((N_EXPERTS, LANES), F32)
    for blk, start in enumerate(block_starts):
        st = jnp.where(lane == blk, start, st)
    st_ref[0] = st.astype(I32)


def _route_call(aff_t, cap):
    B, E, T = aff_t.shape
    assert T // TM < LANES
    return pl.pallas_call(
        functools.partial(_route_kernel, cap=cap),
        grid=(B,),
        in_specs=[pl.BlockSpec((1, E, T), lambda b: (b, 0, 0))],
        out_specs=[pl.BlockSpec((1, E, T), lambda b: (b, 0, 0)),
                   pl.BlockSpec((1, E, LANES), lambda b: (b, 0, 0))],
        out_shape=[jax.ShapeDtypeStruct((B, E, T), I32),
                   jax.ShapeDtypeStruct((B, E, LANES), I32)],
        compiler_params=_cparams(1),
        name="route",
    )(aff_t)


def _align_down(s, m):
    sh = m.bit_length() - 1
    return pl.multiple_of(lax.shift_left(lax.shift_right_logical(s, sh), sh), m)


def _gather_kernel(st_ref, pos_ref, hn_ref, o_ref, acc_ref, *, cap):
    b = pl.program_id(0)
    e = pl.program_id(1)
    nb = hn_ref.shape[1] // TM
    acc_ref[...] = jnp.zeros_like(acc_ref)
    win = lax.broadcasted_iota(I32, (GATHER_WIN, TM), 0)
    rows = []
    for k in range(nb):
        row0 = _align_down(st_ref[b, e, k], SUBLANES)
        onehot = jnp.where(pos_ref[0, :, k * TM:(k + 1) * TM] == row0 + win, 1.0, 0.0)
        acc_ref[pl.ds(row0, GATHER_WIN), :] += _dot(onehot.astype(BF16), hn_ref[0, k * TM:(k + 1) * TM, :])
        rows.append(row0)
    slot = lax.broadcasted_iota(I32, (cap, TM), 0)
    for k in range(nb):
        @pl.when(st_ref[b, e, k + 1] > rows[k] + GATHER_WIN)
        def _():
            rest = (pos_ref[0, :, k * TM:(k + 1) * TM] == slot) & (slot >= rows[k] + GATHER_WIN)
            acc_ref[0:cap, :] += _dot(jnp.where(rest, 1.0, 0.0).astype(BF16),
                                      hn_ref[0, k * TM:(k + 1) * TM, :])
    o_ref[0] = acc_ref[0:cap, :].astype(BF16)


def _gather_call(starts, pos, hn, cap):
    B, E, T = pos.shape
    grid_spec = pltpu.PrefetchScalarGridSpec(
        num_scalar_prefetch=1,
        grid=(B, E),
        in_specs=[pl.BlockSpec((1, 1, T), lambda b, e, st: (b * E + e, 0, 0)),
                  pl.BlockSpec((1, T, D_MODEL), lambda b, e, st: (b, 0, 0))],
        out_specs=pl.BlockSpec((1, cap, D_MODEL), lambda b, e, st: (e, b, 0)),
        scratch_shapes=[pltpu.VMEM((cap + GATHER_WIN, D_MODEL), F32)])
    return pl.pallas_call(
        functools.partial(_gather_kernel, cap=cap),
        grid_spec=grid_spec,
        out_shape=jax.ShapeDtypeStruct((E, B * cap, D_MODEL), BF16),
        compiler_params=_cparams(2),
        name="gather",
    )(starts, pos.reshape(B * E, 1, T), hn)


def _ffn_kernel(x_ref, w1_ref, w3_ref, w2_ref, o_ref, acc_ref):
    f = pl.program_id(2)

    @pl.when(f == 0)
    def _():
        acc_ref[...] = jnp.zeros_like(acc_ref)

    x = x_ref[0]
    h1 = _dot(x, w1_ref[0].astype(BF16))
    h3 = _dot(x, w3_ref[0].astype(BF16))
    hid = (_silu(h1) * h3).astype(BF16)
    acc_ref[...] += _dot(hid, w2_ref[0].astype(BF16))

    @pl.when(f == pl.num_programs(2) - 1)
    def _():
        o_ref[0] = acc_ref[...].astype(BF16)


def _ffn_call(xg, w1, w3, w2):
    E, M, _ = xg.shape
    tm = min(M, 2048)
    nf = D_EXPERT // FFN_TF
    return pl.pallas_call(
        _ffn_kernel,
        grid=(E, M // tm, nf),
        in_specs=[pl.BlockSpec((1, tm, D_MODEL), lambda e, m, f: (e, m, 0)),
                  pl.BlockSpec((1, D_MODEL, FFN_TF), lambda e, m, f: (e, 0, f)),
                  pl.BlockSpec((1, D_MODEL, FFN_TF), lambda e, m, f: (e, 0, f)),
                  pl.BlockSpec((1, FFN_TF, D_MODEL), lambda e, m, f: (e, f, 0))],
        out_specs=pl.BlockSpec((1, tm, D_MODEL), lambda e, m, f: (e, m, 0)),
        out_shape=jax.ShapeDtypeStruct((E, M, D_MODEL), BF16),
        scratch_shapes=[pltpu.VMEM((tm, D_MODEL), F32)],
        compiler_params=_cparams(3),
        name="ffn",
    )(xg, w1, w3, w2)


def _combine_kernel(st_ref, pos_ref, aff_ref, y_ref, h1_ref, g2_ref, fg_ref, o_ref, moe_ref, *, cap):
    b = pl.program_id(0)
    k = pl.program_id(1)
    pos = pos_ref[0]
    aff = aff_ref[0]
    win = lax.broadcasted_iota(I32, (COMBINE_WIN, TM), 0)
    ws, ys, rows = [], [], []
    for e in range(N_EXPERTS):
        row0 = jnp.minimum(_align_down(st_ref[b, e, k], BF16_ROWS), cap - COMBINE_WIN)
        row0 = pl.multiple_of(row0, BF16_ROWS)
        ws.append(jnp.where(pos[e:e + 1, :] == row0 + win, aff[e:e + 1, :], 0.0).astype(BF16))
        ys.append(y_ref[e, pl.ds(row0, COMBINE_WIN), :])
        rows.append(row0)
    moe_ref[...] = _dot(jnp.concatenate(ws, axis=0), jnp.concatenate(ys, axis=0), TN)
    slot = lax.broadcasted_iota(I32, (cap, TM), 0)
    for e in range(N_EXPERTS):
        @pl.when(st_ref[b, e, k + 1] > rows[e] + COMBINE_WIN)
        def _():
            rest = (pos[e:e + 1, :] == slot) & (slot >= rows[e] + COMBINE_WIN)
            moe_ref[...] += _dot(jnp.where(rest, aff[e:e + 1, :], 0.0).astype(BF16), y_ref[e], TN)
    h2 = h1_ref[0] + g2_ref[0] * moe_ref[...]
    o_ref[0] = _rmsnorm(h2, fg_ref[...])


def _combine_call(starts, pos, aff_t, ys, h1, mods3, final_g, cap):
    B, E, T = pos.shape
    assert cap >= COMBINE_WIN and (cap - COMBINE_WIN) % BF16_ROWS == 0
    grid_spec = pltpu.PrefetchScalarGridSpec(
        num_scalar_prefetch=1,
        grid=(B, T // TM),
        in_specs=[pl.BlockSpec((1, E, TM), lambda b, i, st: (b, 0, i)),
                  pl.BlockSpec((1, E, TM), lambda b, i, st: (b, 0, i)),
                  pl.BlockSpec((E, cap, D_MODEL), lambda b, i, st: (0, b, 0)),
                  pl.BlockSpec((1, TM, D_MODEL), lambda b, i, st: (b, i, 0)),
                  pl.BlockSpec((1, 1, D_MODEL), lambda b, i, st: (b, 0, 5)),
                  pl.BlockSpec((1, D_MODEL), lambda b, i, st: (0, 0))],
        out_specs=pl.BlockSpec((1, TM, D_MODEL), lambda b, i, st: (b, i, 0)),
        scratch_shapes=[pltpu.VMEM((TM, D_MODEL), F32)])
    return pl.pallas_call(
        functools.partial(_combine_kernel, cap=cap),
        grid_spec=grid_spec,
        out_shape=jax.ShapeDtypeStruct((B, T, D_MODEL), F32),
        compiler_params=_cparams(2),
        name="combine",
    )(starts, pos, aff_t, ys, h1, mods3, final_g)


def _pad_cols(w, n):
    return jnp.pad(w, ((0, 0), (0, n - w.shape[1])))


def _both_dirs(up):
    z = jnp.zeros_like(up[0])
    return jnp.concatenate([jnp.concatenate([up[0], z], axis=1),
                            jnp.concatenate([z, up[1]], axis=1)], axis=0)


def kernel(x, c, ctx, c_ctx, mod_w, mod_b, norm1_g, w_in, ml_conv_w, ml_conv_b, ml_gate_b, ml_norm_g,
           rw_mu, rw_w0, rw_w_up, rw_a0, rw_a_up, rw_g_up, rw_k_k, rw_k_a, rw_r_k, rw_ln_g, rw_ln_b,
           w_out, norm2_g, router_w, exp_w1, exp_w3, exp_w2, final_g):
    B, T, D = x.shape
    t_ctx = ctx.shape[1]
    assert D == D_MODEL and T % TM == 0 and t_ctx % TM == 0 and TM % (T // GRID_W) == 0
    assert mod_w.shape[0] == 1 and B < 16
    nct, nlt = t_ctx // TM, T // TM
    cap = EC_FACTOR * T // N_EXPERTS
    ml_cols = 4 * ML_W + ML_GATES

    cc = jnp.concatenate([c, c_ctx[None, :], jnp.zeros((16 - B - 1, D), F32)], axis=0)
    mods = _mod_call(cc, mod_w[0], mod_b)
    mods3 = mods.reshape(16, 1, 6 * D)
    g1n = norm1_g.reshape(1, D)

    w_ml = _pad_cols(w_in[0, :, :ml_cols], ML_N).astype(BF16)
    w_rw = _pad_cols(w_in[0, :, ml_cols:], RW_N).astype(BF16)
    qkv, gates, zo = _ml_in_call(x, ctx, mods3, g1n, w_ml, ml_conv_w[0], ml_conv_b,
                                 _pad_cols(ml_gate_b, LANES))
    h_f, h_b = _mlstm_call(qkv, gates, B, nct, nlt)

    gup = jnp.pad(rw_g_up[0], ((0, 256 - RW_LORA_G), (0, 0)))
    scan_in, gb = _rw_in_call(
        x, ctx, mods3, g1n, w_rw,
        _pad_cols(rw_mu[0], RW_N), rw_w0[0].reshape(1, 2 * RW_W), _both_dirs(rw_w_up[0]),
        rw_a0[0].reshape(1, 2 * RW_W), _both_dirs(rw_a_up[0]), gup,
        rw_k_k, rw_k_a, rw_r_k[0].reshape(1, RW_W))
    y_f, y_b = _rw_scan_call(scan_in, B, t_ctx, T)

    h1, hn, aff_t = _mix_out_call(
        x, h_f, h_b, zo, y_f, y_b, gb, mods3, ml_norm_g, rw_ln_g, rw_ln_b, norm2_g,
        w_out[0].astype(BF16), _pad_cols(router_w[0], LANES))

    pos, starts = _route_call(aff_t, cap)
    starts = starts[:, :, :nlt + 1]
    xg = _gather_call(starts, pos, hn, cap)
    ys = _ffn_call(xg, exp_w1[0], exp_w3[0], exp_w2[0])
    return _combine_call(starts, pos, aff_t, ys, h1, mods3, final_g.reshape(1, D), cap)
```
